```python
import math
import jax
import jax.numpy as jnp
from jax import lax
import numpy as np

D_MODEL = 1024
BATCH = 8
SEQ = 2048
DEPTH = 2

W_A = 256
A_GROUPS = 4
CHUNK = 128
W_B = 256
CONV_WIDTH = 3
DILATION_PATTERNS = ((128, 1), (512, 4), (2048, 16))
C_GROUPS = 3
C_HEADS_PER_GROUP = 4
C_HEADS = C_GROUPS * C_HEADS_PER_GROUP
C_HEAD_DIM = 64
W_C_QKV = C_HEADS * C_HEAD_DIM
W_C = C_HEADS_PER_GROUP * C_HEAD_DIM
POOL_WINDOWS = (2, 4, 8, 16)
POOL_GROUPS = 4
W_D = 256
D_GROUP = W_D // POOL_GROUPS
N_BRANCH = 4
D_FF = 4 * D_MODEL
N_BUCKETS = 32
MAX_DISTANCE = 2048
EPS = 1e-6

IN_SPLITS = (W_A, W_A, W_B, W_B, W_B, W_C_QKV, W_C_QKV, W_C_QKV, W_D, N_BRANCH * D_MODEL)
D_IN = sum(IN_SPLITS)

kernel_name = "hybrid_gated_parallel_mixers"


def rmsnorm(x, g):
    x32 = x.astype(jnp.float32)
    y = x32 * lax.rsqrt(jnp.mean(jnp.square(x32), axis=-1, keepdims=True) + EPS)
    return y.astype(x.dtype) * g


def layernorm(x, g, b):
    x32 = x.astype(jnp.float32)
    mu = jnp.mean(x32, axis=-1, keepdims=True)
    var = jnp.mean(jnp.square(x32 - mu), axis=-1, keepdims=True)
    return ((x32 - mu) * lax.rsqrt(var + EPS)).astype(x.dtype) * g + b


def gmlp_spatial_gate(u, v, ln_g, ln_b, w_s, b_s):
    B, S, _ = v.shape
    u = jax.nn.gelu(u)
    v = layernorm(jax.nn.gelu(v), ln_g, ln_b)
    vc = v.reshape(B, S // CHUNK, CHUNK, A_GROUPS, W_A // A_GROUPS)
    causal = jnp.tril(jnp.ones((CHUNK, CHUNK), dtype=bool))
    w = jnp.where(causal[None], w_s, jnp.zeros_like(w_s))
    sv = jnp.einsum('gts,bcsgd->bctgd', w, vc) + b_s.T[None, None, :, :, None]
    return u * sv.reshape(B, S, W_A)


def causal_short_conv(z, w):
    S = z.shape[1]
    zp = jnp.pad(z, ((0, 0), (CONV_WIDTH - 1, 0), (0, 0)))
    y = w[0] * zp[:, 0:S]
    for j in range(1, CONV_WIDTH):
        y = y + w[j] * zp[:, j:j + S]
    return y


def t5_causal_bucket(dist):
    max_exact = N_BUCKETS // 2
    d_f = jnp.maximum(dist, 1).astype(jnp.float32)
    large = max_exact + (jnp.log(d_f / max_exact) / math.log(MAX_DISTANCE / max_exact)
                         * (N_BUCKETS - max_exact)).astype(jnp.int32)
    large = jnp.minimum(large, N_BUCKETS - 1)
    return jnp.where(dist < max_exact, dist, large)


def dilated_window_attention(q, k, v, bias_table, window, dil):
    B, S, H, Dh = q.shape
    n = window // dil
    Sp = -(-S // window) * window
    M = Sp // dil
    nb = M // n
    pad = ((0, 0), (0, Sp - S), (0, 0), (0, 0))

    def to_blocks(t):
        return jnp.pad(t, pad).reshape(B, nb, n, dil, H, Dh)

    def with_prev(t):
        prev = jnp.pad(t, ((0, 0), (1, 0), (0, 0), (0, 0), (0, 0), (0, 0)))[:, :nb]
        return jnp.concatenate([prev, t], axis=2)

    qb = to_blocks(q)
    kw = with_prev(to_blocks(k))
    vw = with_prev(to_blocks(v))
    s = jnp.einsum('bnqrhd,bnkrhd->bnrhqk', qb, kw).astype(jnp.float32) * (Dh ** -0.5)

    qi = jnp.arange(n)[:, None]
    ki = jnp.arange(2 * n)[None, :]
    delta = qi + n - ki
    band = (delta >= 0) & (delta <= n)
    blk = jnp.arange(nb)[:, None, None]
    valid = band[None] & ((blk > 0) | (ki >= n)[None])
    bias = bias_table[t5_causal_bucket(jnp.clip(delta, 0, None) * dil)]
    s = s + jnp.transpose(bias, (2, 0, 1)).astype(jnp.float32)[None, None, None]
    s = jnp.where(valid[None, :, None, None], s, -jnp.inf)
    m = jnp.max(s, axis=-1, keepdims=True)
    e = jnp.exp(s - m)
    den = jnp.sum(e, axis=-1, keepdims=True)
    o = jnp.einsum('bnrhqk,bnkrhd->bnqrhd', (e / den).astype(v.dtype), vw)
    lse = (m + jnp.log(den))[..., 0]
    o = o.reshape(B, Sp, H, Dh)[:, :S]
    lse = jnp.transpose(lse, (0, 1, 4, 2, 3)).reshape(B, Sp, H)[:, :S]
    return o, lse


def dilated_mixture(q, k, v, rel_bias):
    B, S, _ = q.shape
    shp = (B, S, C_GROUPS, C_HEADS_PER_GROUP, C_HEAD_DIM)
    q, k, v = q.reshape(shp), k.reshape(shp), v.reshape(shp)
    outs, lses = [], []
    for g, (window, dil) in enumerate(DILATION_PATTERNS):
        hs = slice(g * C_HEADS_PER_GROUP, (g + 1) * C_HEADS_PER_GROUP)
        o, lse = dilated_window_attention(q[:, :, g], k[:, :, g], v[:, :, g],
                                          rel_bias[:, hs], window, dil)
        outs.append(o)
        lses.append(lse)
    o = jnp.stack(outs, axis=2)
    alpha = jax.nn.softmax(jnp.stack(lses, axis=2), axis=2)
    out = jnp.einsum('bsgh,bsghd->bshd', alpha.astype(o.dtype), o)
    return out.reshape(B, S, W_C)


def multiscale_pool(z, w_pool, scale):
    B, S, _ = z.shape
    zg = z.reshape(B, S, POOL_GROUPS, D_GROUP).astype(jnp.float32)
    t = jnp.arange(S, dtype=jnp.float32)
    outs = []
    for gi, w in enumerate(POOL_WINDOWS):
        c = jnp.cumsum(zg[:, :, gi], axis=1)
        c_lag = jnp.pad(c, ((0, 0), (w, 0), (0, 0)))[:, :S]
        cnt = jnp.minimum(t + 1.0, float(w))
        outs.append((c - c_lag) / cnt[None, :, None])
    pooled = jnp.stack(outs, axis=2)
    y = (pooled - zg).astype(z.dtype)
    y = jnp.einsum('bsgd,gde->bsge', y, w_pool).reshape(B, S, W_D)
    return y * scale


def setup_inputs(seed: int = 0) -> dict:
    key = jax.random.key(seed)
    ks = jax.random.split(key, 21)
    L = DEPTH

    def nrm(k, shape, scale):
        return jax.random.normal(k, shape, jnp.float32) * scale

    return {
        "x": nrm(ks[0], (BATCH, SEQ, D_MODEL), 1.0),
        "norm_mix_g": 1.0 + nrm(ks[1], (L, D_MODEL), 0.05),
        "w_in": nrm(ks[2], (L, D_MODEL, D_IN), D_MODEL ** -0.5),
        "a_ln_g": 1.0 + nrm(ks[3], (L, W_A), 0.05),
        "a_ln_b": nrm(ks[4], (L, W_A), 0.02),
        "a_ws": nrm(ks[5], (L, A_GROUPS, CHUNK, CHUNK), CHUNK ** -0.5),
        "a_bs": 1.0 + nrm(ks[6], (L, A_GROUPS, CHUNK), 0.1),
        "w_a_out": nrm(ks[7], (L, W_A, D_MODEL), W_A ** -0.5),
        "b_conv": nrm(ks[8], (L, CONV_WIDTH, W_B), CONV_WIDTH ** -0.5),
        "w_b_out": nrm(ks[9], (L, W_B, D_MODEL), W_B ** -0.5),
        "rel_bias": nrm(ks[10], (N_BUCKETS, C_HEADS), 0.5),
        "w_c_out": nrm(ks[11], (L, W_C, D_MODEL), W_C ** -0.5),
        "d_w": nrm(ks[12], (L, POOL_GROUPS, D_GROUP, D_GROUP), D_GROUP ** -0.5),
        "d_scale": 1.0 + nrm(ks[13], (L, W_D), 0.1),
        "w_d_out": nrm(ks[14], (L, W_D, D_MODEL), W_D ** -0.5),
        "w_o": nrm(ks[15], (L, D_MODEL, D_MODEL), D_MODEL ** -0.5),
        "norm_ff_g": 1.0 + nrm(ks[16], (L, D_MODEL), 0.05),
        "w_ff1": nrm(ks[17], (L, D_MODEL, D_FF), D_MODEL ** -0.5),
        "w_ff2": nrm(ks[18], (L, D_FF, D_MODEL), D_FF ** -0.5),
        "final_g": 1.0 + nrm(ks[19], (D_MODEL,), 0.05),
    }


def reference(x, norm_mix_g, w_in, a_ln_g, a_ln_b, a_ws, a_bs, w_a_out, b_conv, w_b_out,
              rel_bias, w_c_out, d_w, d_scale, w_d_out, w_o, norm_ff_g, w_ff1, w_ff2, final_g):
    B, S, _ = x.shape
    split_at = [sum(IN_SPLITS[:i + 1]) for i in range(len(IN_SPLITS) - 1)]
    for l in range(DEPTH):
        h = rmsnorm(x, norm_mix_g[l])
        z = h @ w_in[l]
        a_u, a_v, b_b, b_c, b_x, c_q, c_k, c_v, d_z, gates = jnp.split(z, split_at, axis=-1)

        y_a = gmlp_spatial_gate(a_u, a_v, a_ln_g[l], a_ln_b[l], a_ws[l], a_bs[l]) @ w_a_out[l]
        y_b = (b_b * causal_short_conv(b_c * b_x, b_conv[l])) @ w_b_out[l]
        y_c = dilated_mixture(c_q, c_k, c_v, rel_bias) @ w_c_out[l]
        y_d = multiscale_pool(d_z, d_w[l], d_scale[l]) @ w_d_out[l]

        g = jax.nn.sigmoid(gates).reshape(B, S, N_BRANCH, D_MODEL)
        merged = (g[:, :, 0] * y_a + g[:, :, 1] * y_b
                  + g[:, :, 2] * y_c + g[:, :, 3] * y_d)
        x = x + merged @ w_o[l]

        h2 = rmsnorm(x, norm_ff_g[l])
        x = x + jnp.square(jax.nn.relu(h2 @ w_ff1[l])) @ w_ff2[l]
    return rmsnorm(x, final_g)
```

```python
import functools
import math

import jax
import jax.numpy as jnp
import numpy as np
from jax import lax
from jax.experimental import pallas as pl
from jax.experimental.pallas import tpu as pltpu

F32 = jnp.float32
BF16 = jnp.bfloat16

D_MODEL = 1024
W_BRANCH = 256
A_GROUPS = 4
CHUNK = 128
CONV_WIDTH = 3
DILATIONS = (1, 4, 16)
WINDOWS = (128, 512, 2048)
HEADS_PER_GROUP = 4
HEAD_DIM = 64
POOL_WINDOWS = (2, 4, 8, 16)
POOL_HALO = 16
CONV_HALO = 8
N_BRANCH = 4
D_FF = 4 * D_MODEL
N_BUCKETS = 32
MAX_DISTANCE = 2048
EPS = 1e-6
NEG_BIG = -1e30

W_LOCAL = 6 * W_BRANCH
W_QKV = 3 * W_BRANCH
W_QKV_OUT = 4 * W_BRANCH
GATE_COL0 = 5 * W_BRANCH + 3 * W_QKV + W_BRANCH

MIX_TILE = 512
TOK_TILE = 512
VMEM_LIMIT = 56 * 1024 * 1024


def _rmsnorm(x, g):
    return x * lax.rsqrt(jnp.mean(x * x, axis=-1, keepdims=True) + EPS) * g


def _gelu_tanh(x):
    c = math.sqrt(2.0 / math.pi)
    return x * (0.5 * (1.0 + jnp.tanh(c * (x + 0.044715 * (x * x * x)))))


def _dot(a, b):
    return jnp.dot(a, b, preferred_element_type=F32)


def _const_spec(shape):
    n = len(shape)
    return pl.BlockSpec(shape, lambda *_: (0,) * n, pipeline_mode=pl.Buffered(1))


def _mix_kernel(x_ref, g_ref, w_ref, lng_ref, lnb_ref, ws_ref, bs_ref, conv_ref, wpool_ref,
                dscale_ref, pa_ref, pb_ref, pd_ref, q0_ref, q1_ref, q2_ref,
                h_scr, bbuf, dbuf):
    ts = MIX_TILE
    i = pl.program_id(1)
    x = x_ref[0]
    h = _rmsnorm(x, g_ref[...])
    for c in range(D_MODEL // 128):
        h_scr[c] = h[:, c * 128:(c + 1) * 128]
    hb = h.astype(BF16)
    zl = _dot(hb, w_ref[:, 0:W_LOCAL])

    u = _gelu_tanh(zl[:, 0:256])
    v = _gelu_tanh(zl[:, 256:512])
    mu = jnp.mean(v, axis=-1, keepdims=True)
    vc = v - mu
    var = jnp.mean(vc * vc, axis=-1, keepdims=True)
    v = vc * lax.rsqrt(var + EPS) * lng_ref[...] + lnb_ref[...]
    row = lax.broadcasted_iota(jnp.int32, (CHUNK, A_GROUPS * CHUNK), 0)
    col = lax.broadcasted_iota(jnp.int32, (CHUNK, A_GROUPS * CHUNK), 1) % CHUNK
    wtril = jnp.where(row >= col, ws_ref[...], 0.0).astype(BF16)
    grp = lax.broadcasted_iota(jnp.int32, (CHUNK, W_BRANCH), 1) // (W_BRANCH // A_GROUPS)
    svs = []
    for c in range(ts // CHUNK):
        vch = v[c * CHUNK:(c + 1) * CHUNK]
        stacked = jnp.concatenate(
            [jnp.where(grp == g, vch, 0.0) for g in range(A_GROUPS)], axis=0).astype(BF16)
        svs.append(_dot(wtril, stacked) + bs_ref[...])
    sv = jnp.concatenate(svs, axis=0)
    pa_ref[0] = (u * sv).astype(BF16)

    prod = zl[:, 768:1024] * zl[:, 1024:1280]

    @pl.when(i == 0)
    def _():
        bbuf[0:CONV_HALO] = jnp.zeros((CONV_HALO, W_BRANCH), F32)
        dbuf[0:POOL_HALO] = jnp.zeros((POOL_HALO, W_BRANCH), F32)

    @pl.when(i > 0)
    def _():
        bbuf[0:CONV_HALO] = bbuf[ts:ts + CONV_HALO]
        dbuf[0:POOL_HALO] = dbuf[ts:ts + POOL_HALO]

    bbuf[CONV_HALO:ts + CONV_HALO] = prod
    cw = conv_ref[...]
    conv = (cw[0:1] * bbuf[pl.ds(CONV_HALO - 2, ts), :]
            + cw[1:2] * bbuf[pl.ds(CONV_HALO - 1, ts), :]
            + cw[2:3] * prod)
    pb_ref[0] = (zl[:, 512:768] * conv).astype(BF16)

    dz = zl[:, 1280:1536]
    dbuf[POOL_HALO:ts + POOL_HALO] = dz
    lane = lax.broadcasted_iota(jnp.int32, (ts, 128), 1)
    first_half = lane < 64
    tpos = (lax.broadcasted_iota(jnp.int32, (ts, 128), 0) + (i * ts + 1)).astype(F32)

    def window_sums(e, levels):
        out = []
        s = e
        for k in range(levels):
            s = s + pltpu.roll(s, 1 << k, 0)
            out.append(s[POOL_HALO:])
        return out

    lo = window_sums(dbuf[:, 0:128], 2)
    hi = window_sums(dbuf[:, 128:256], 4)
    pooled_lo = jnp.where(first_half, lo[0], lo[1]) / jnp.minimum(tpos, jnp.where(first_half, 2.0, 4.0))
    pooled_hi = jnp.where(first_half, hi[2], hi[3]) / jnp.minimum(tpos, jnp.where(first_half, 8.0, 16.0))
    y = jnp.concatenate([pooled_lo, pooled_hi], axis=1) - dz
    pd_ref[0] = (_dot(y.astype(BF16), wpool_ref[...]) * dscale_ref[...]).astype(BF16)

    lane_q = lax.broadcasted_iota(jnp.int32, (ts, 128), 1)
    sub0 = lane_q < HEAD_DIM

    def emit_qkv(z, store):
        q = z[:, 0:256] * (HEAD_DIM ** -0.5)
        pieces = []
        for pr in range(2):
            qp = q[:, pr * 128:(pr + 1) * 128]
            pieces.append(jnp.where(sub0, qp, 0.0))
            pieces.append(jnp.where(sub0, 0.0, qp))
            pieces.append(z[:, 256 + pr * 128:256 + (pr + 1) * 128])
            pieces.append(z[:, 512 + pr * 128:512 + (pr + 1) * 128])
        store(jnp.concatenate(pieces, axis=1).astype(BF16))

    def store0(val):
        q0_ref[0] = val

    emit_qkv(_dot(hb, w_ref[:, W_LOCAL:W_LOCAL + W_QKV]), store0)

    for gi, (q_ref, dil) in enumerate(((q1_ref, DILATIONS[1]), (q2_ref, DILATIONS[2]))):
        n = ts // dil
        hp = jnp.concatenate(
            [jnp.concatenate([h_scr[c, pl.ds(r, n, stride=dil), :] for r in range(dil)], axis=0)
             for c in range(D_MODEL // 128)], axis=1).astype(BF16)
        c0 = W_LOCAL + (gi + 1) * W_QKV

        def store_perm(val, q_ref=q_ref, dil=dil, n=n):
            for r in range(dil):
                q_ref[0, r] = val[r * n:(r + 1) * n]

        emit_qkv(_dot(hp, w_ref[:, c0:c0 + W_QKV]), store_perm)


def _mix_call(x, g, w_mix, lng, lnb, ws, bs_full, conv, wpool_bd, dscale):
    B, S, D = x.shape
    ts = MIX_TILE
    nt = S // ts
    tok = lambda b, i: (b, i, 0)
    out_shape = (
        jax.ShapeDtypeStruct((B, S, W_BRANCH), BF16),
        jax.ShapeDtypeStruct((B, S, W_BRANCH), BF16),
        jax.ShapeDtypeStruct((B, S, W_BRANCH), BF16),
        jax.ShapeDtypeStruct((B, S, W_QKV_OUT), BF16),
        jax.ShapeDtypeStruct((B, DILATIONS[1], S // DILATIONS[1], W_QKV_OUT), BF16),
        jax.ShapeDtypeStruct((B, DILATIONS[2], S // DILATIONS[2], W_QKV_OUT), BF16),
    )
    in_specs = [
        pl.BlockSpec((1, ts, D), tok),
        _const_spec((1, D)),
        _const_spec(w_mix.shape),
        _const_spec((1, W_BRANCH)),
        _const_spec((1, W_BRANCH)),
        _const_spec((CHUNK, A_GROUPS * CHUNK)),
        _const_spec((CHUNK, W_BRANCH)),
        _const_spec((CONV_WIDTH, W_BRANCH)),
        _const_spec((W_BRANCH, W_BRANCH)),
        _const_spec((1, W_BRANCH)),
    ]
    out_specs = (
        pl.BlockSpec((1, ts, W_BRANCH), tok),
        pl.BlockSpec((1, ts, W_BRANCH), tok),
        pl.BlockSpec((1, ts, W_BRANCH), tok),
        pl.BlockSpec((1, ts, W_QKV_OUT), tok),
        pl.BlockSpec((1, DILATIONS[1], ts // DILATIONS[1], W_QKV_OUT), lambda b, i: (b, 0, i, 0)),
        pl.BlockSpec((1, DILATIONS[2], ts // DILATIONS[2], W_QKV_OUT), lambda b, i: (b, 0, i, 0)),
    )
    return pl.pallas_call(
        _mix_kernel,
        grid=(B, nt),
        in_specs=in_specs,
        out_specs=out_specs,
        out_shape=out_shape,
        scratch_shapes=[
            pltpu.VMEM((D // 128, ts, 128), F32),
            pltpu.VMEM((ts + CONV_HALO, W_BRANCH), F32),
            pltpu.VMEM((ts + POOL_HALO, W_BRANCH), F32),
        ],
        compiler_params=pltpu.CompilerParams(
            dimension_semantics=("arbitrary", "arbitrary"), vmem_limit_bytes=VMEM_LIMIT),
        name="mix",
    )(x, g, w_mix, lng, lnb, ws, bs_full, conv, wpool_bd, dscale)


def _attn_kernel(q0_ref, q1_ref, q2_ref, bcur_ref, bfull_ref, out_ref, o_scr, l_scr):
    S = q0_ref.shape[1]
    pr = pl.program_id(1)
    refs = (q0_ref, q1_ref, q2_ref)
    lane = lax.broadcasted_iota(jnp.int32, (CHUNK, 128), 1)
    sub0 = lane < HEAD_DIM
    nt_dims = (((1,), (1,)), ((), ()))

    def unit(g, q_start, k_start, has_prev, out_rows):
        ref = refs[g]
        nk = 2 * CHUNK if has_prev else CHUNK
        kw = ref[0, pl.ds(k_start, nk), 256:384]
        vw = ref[0, pl.ds(k_start, nk), 384:512]
        os_, ls_ = [], []
        for sub in range(2):
            hh = 2 * pr + sub
            qe = ref[0, pl.ds(q_start, CHUNK), sub * 128:(sub + 1) * 128]
            s = lax.dot_general(qe, kw, nt_dims, preferred_element_type=F32)
            s = s + (bfull_ref[g, hh] if has_prev else bcur_ref[g, hh])
            m = jnp.max(s, axis=-1, keepdims=True)
            e = jnp.exp(s - m)
            den = jnp.sum(e, axis=-1, keepdims=True)
            pv = _dot(e.astype(BF16), vw)
            os_.append(pv / den)
            ls_.append(jnp.broadcast_to(m + jnp.log(den), (CHUNK, 128)))
        o_scr[g, out_rows, :] = jnp.where(sub0, os_[0], os_[1])
        l_scr[g, out_rows, :] = jnp.where(sub0, ls_[0], ls_[1])

    unit(0, 0, 0, False, pl.ds(0, CHUNK))

    def g0_body(j, carry):
        qs = pl.multiple_of(j * CHUNK, CHUNK)
        unit(0, qs, qs - CHUNK, True, pl.ds(qs, CHUNK))
        return carry

    lax.fori_loop(1, S // CHUNK, g0_body, 0)

    for g in (1, 2):
        dil = DILATIONS[g]
        sub_len = S // dil
        nblk = sub_len // CHUNK
        for r in range(dil):
            base = r * sub_len
            unit(g, base, base, False, pl.ds(r, CHUNK, stride=dil))
            for blk in range(1, nblk):
                qs = base + blk * CHUNK
                unit(g, qs, qs - CHUNK, True, pl.ds(blk * CHUNK * dil + r, CHUNK, stride=dil))

    rows = 256
    for c in range(S // rows):
        sl = pl.ds(c * rows, rows)
        l0, l1, l2 = l_scr[0, sl, :], l_scr[1, sl, :], l_scr[2, sl, :]
        m = jnp.maximum(jnp.maximum(l0, l1), l2)
        e0, e1, e2 = jnp.exp(l0 - m), jnp.exp(l1 - m), jnp.exp(l2 - m)
        num = e0 * o_scr[0, sl, :] + e1 * o_scr[1, sl, :] + e2 * o_scr[2, sl, :]
        out_ref[0, sl, :] = (num / (e0 + e1 + e2)).astype(BF16)


def _attn_call(q0, q1, q2, bcur, bfull):
    B, S, W = q0.shape
    wp = W // 2
    spec = pl.BlockSpec((1, S, wp), lambda b, p: (b, 0, p))
    return pl.pallas_call(
        _attn_kernel,
        grid=(B, 2),
        in_specs=[spec, spec, spec, _const_spec(bcur.shape), _const_spec(bfull.shape)],
        out_specs=pl.BlockSpec((1, S, 128), lambda b, p: (b, 0, p)),
        out_shape=jax.ShapeDtypeStruct((B, S, W_BRANCH), BF16),
        scratch_shapes=[
            pltpu.VMEM((3, S, 128), F32),
            pltpu.VMEM((3, S, 128), F32),
        ],
        compiler_params=pltpu.CompilerParams(
            dimension_semantics=("arbitrary", "arbitrary"), vmem_limit_bytes=VMEM_LIMIT),
        name="attn",
    )(q0, q1, q2, bcur, bfull)


def _merge_kernel(x_ref, pa_ref, pb_ref, pc_ref, pd_ref, g_ref, wg_ref, wout_ref, wo_ref, o_ref):
    x = x_ref[...]
    hb = _rmsnorm(x, g_ref[...]).astype(BF16)
    merged = None
    for br, p_ref in enumerate((pa_ref, pb_ref, pc_ref, pd_ref)):
        gate = jax.nn.sigmoid(_dot(hb, wg_ref[:, br * D_MODEL:(br + 1) * D_MODEL]))
        term = gate * _dot(p_ref[...], wout_ref[br])
        merged = term if merged is None else merged + term
    o_ref[...] = x + _dot(merged.astype(BF16), wo_ref[...])


def _merge_call(x, pa, pb, pc, pd, g, w_gate, w_out, w_o):
    T, D = x.shape
    tt = TOK_TILE
    tok = lambda i: (i, 0)
    pspec = pl.BlockSpec((tt, W_BRANCH), tok)
    return pl.pallas_call(
        _merge_kernel,
        grid=(T // tt,),
        in_specs=[pl.BlockSpec((tt, D), tok), pspec, pspec, pspec, pspec,
                  _const_spec((1, D)), _const_spec(w_gate.shape), _const_spec(w_out.shape),
                  _const_spec(w_o.shape)],
        out_specs=pl.BlockSpec((tt, D), tok),
        out_shape=jax.ShapeDtypeStruct((T, D), F32),
        compiler_params=pltpu.CompilerParams(
            dimension_semantics=("arbitrary",), vmem_limit_bytes=VMEM_LIMIT),
        name="merge",
    )(x, pa, pb, pc, pd, g, w_gate, w_out, w_o)


def _ffn_kernel(x_ref, g_ref, w1_ref, w2_ref, fg_ref, o_ref, *, final_norm):
    x = x_ref[...]
    hb = _rmsnorm(x, g_ref[...]).astype(BF16)
    acts = []
    for c in range(D_FF // D_MODEL):
        a = _dot(hb, w1_ref[:, c * D_MODEL:(c + 1) * D_MODEL])
        acts.append(jnp.square(jnp.maximum(a, 0.0)).astype(BF16))
    y = x + _dot(jnp.concatenate(acts, axis=1), w2_ref[...])
    if final_norm:
        y = _rmsnorm(y, fg_ref[...])
    o_ref[...] = y


def _ffn_call(x, g, w1, w2, final_g, final_norm):
    T, D = x.shape
    tt = TOK_TILE
    tok = lambda i: (i, 0)
    return pl.pallas_call(
        functools.partial(_ffn_kernel, final_norm=final_norm),
        grid=(T // tt,),
        in_specs=[pl.BlockSpec((tt, D), tok), _const_spec((1, D)), _const_spec(w1.shape),
                  _const_spec(w2.shape), _const_spec((1, D))],
        out_specs=pl.BlockSpec((tt, D), tok),
        out_shape=jax.ShapeDtypeStruct((T, D), F32),
        compiler_params=pltpu.CompilerParams(
            dimension_semantics=("arbitrary",), vmem_limit_bytes=VMEM_LIMIT),
        name="ffn",
    )(x, g, w1, w2, final_g)


def _t5_bucket_np(dist):
    max_exact = N_BUCKETS // 2
    d_f = np.maximum(dist, 1).astype(np.float32)
    ratio = np.log(d_f / np.float32(max_exact)) / np.float32(math.log(MAX_DISTANCE / max_exact))
    large = max_exact + (ratio * np.float32(N_BUCKETS - max_exact)).astype(np.int32)
    large = np.minimum(large, N_BUCKETS - 1)
    return np.where(dist < max_exact, dist, large)


def _bias_tables(rel_bias):
    qi = np.arange(CHUNK)[:, None]
    ki = np.arange(CHUNK)[None, :]
    d_cur = qi - ki
    d_prev = qi + CHUNK - ki
    curs, fulls = [], []
    for g, dil in enumerate(DILATIONS):
        tab = rel_bias[:, g * HEADS_PER_GROUP:(g + 1) * HEADS_PER_GROUP]
        idx_cur = _t5_bucket_np(np.clip(d_cur, 0, None) * dil)
        cur = jnp.where((d_cur >= 0)[None], jnp.transpose(tab[idx_cur], (2, 0, 1)), NEG_BIG)
        curs.append(cur)
        if g < 2:
            idx_prev = _t5_bucket_np(d_prev * dil)
            prev = jnp.where((d_prev <= CHUNK)[None], jnp.transpose(tab[idx_prev], (2, 0, 1)), NEG_BIG)
            fulls.append(jnp.concatenate([prev, cur], axis=-1))
    return jnp.stack(curs).astype(F32), jnp.stack(fulls).astype(F32)


def kernel(x, norm_mix_g, w_in, a_ln_g, a_ln_b, a_ws, a_bs, w_a_out, b_conv, w_b_out, rel_bias,
           w_c_out, d_w, d_scale, w_d_out, w_o, norm_ff_g, w_ff1, w_ff2, final_g):
    B, S, D = x.shape
    depth = w_in.shape[0]
    bcur, bfull = _bias_tables(rel_bias)
    c_q0 = 5 * W_BRANCH
    c_k0 = c_q0 + 3 * W_BRANCH
    c_v0 = c_k0 + 3 * W_BRANCH
    c_d0 = c_v0 + 3 * W_BRANCH
    for l in range(depth):
        wl = w_in[l]
        cols = [wl[:, 0:c_q0], wl[:, c_d0:c_d0 + W_BRANCH]]
        for g in range(3):
            sl = slice(g * W_BRANCH, (g + 1) * W_BRANCH)
            cols += [wl[:, c_q0:c_k0][:, sl], wl[:, c_k0:c_v0][:, sl], wl[:, c_v0:c_d0][:, sl]]
        w_mix = jnp.concatenate(cols, axis=1).astype(BF16)
        w_gate = wl[:, GATE_COL0:].astype(BF16)
        ws = jnp.transpose(a_ws[l], (1, 0, 2)).reshape(CHUNK, A_GROUPS * CHUNK)
        bs_full = jnp.repeat(a_bs[l].T, W_BRANCH // A_GROUPS, axis=1)
        wpool_bd = jax.scipy.linalg.block_diag(*[d_w[l, g] for g in range(4)]).astype(BF16)
        w_out = jnp.stack([w_a_out[l], w_b_out[l], w_c_out[l], w_d_out[l]]).astype(BF16)

        pa, pb, pd, q0, q1, q2 = _mix_call(
            x, norm_mix_g[l][None], w_mix, a_ln_g[l][None], a_ln_b[l][None], ws, bs_full,
            b_conv[l], wpool_bd, d_scale[l][None])
        pc = _attn_call(q0, q1.reshape(B, S, W_QKV_OUT), q2.reshape(B, S, W_QKV_OUT), bcur, bfull)

        T = B * S
        x2 = _merge_call(x.reshape(T, D), pa.reshape(T, -1), pb.reshape(T, -1), pc.reshape(T, -1),
                         pd.reshape(T, -1), norm_mix_g[l][None], w_gate, w_out, w_o[l].astype(BF16))
        x2 = _ffn_call(x2, norm_ff_g[l][None], w_ff1[l].astype(BF16), w_ff2[l].astype(BF16),
                       final_g[None], final_norm=(l == depth - 1))
        x = x2.reshape(B, S, D)
    return x
```

```python
import functools
import math

import jax
import jax.numpy as jnp
import numpy as np
from jax import lax
from jax.experimental import pallas as pl
from jax.experimental.pallas import tpu as pltpu

F32 = jnp.float32
BF16 = jnp.bfloat16

D_MODEL = 1024
W_BRANCH = 256
A_GROUPS = 4
CHUNK = 128
CONV_WIDTH = 3
DILATIONS = (1, 4, 16)
WINDOWS = (128, 512, 2048)
HEADS_PER_GROUP = 4
HEAD_DIM = 64
POOL_WINDOWS = (2, 4, 8, 16)
POOL_HALO = 16
CONV_HALO = 8
N_BRANCH = 4
D_FF = 4 * D_MODEL
N_BUCKETS = 32
MAX_DISTANCE = 2048
EPS = 1e-6
NEG_BIG = -1e30

W_LOCAL = 6 * W_BRANCH
W_QKV = 3 * W_BRANCH
W_QKV_OUT = 4 * W_BRANCH
GATE_COL0 = 5 * W_BRANCH + 3 * W_QKV + W_BRANCH

MIX_TILE = 512
TOK_TILE = 512
VMEM_LIMIT = 56 * 1024 * 1024


def _rmsnorm(x, g):
    return x * lax.rsqrt(jnp.mean(x * x, axis=-1, keepdims=True) + EPS) * g


def _gelu_tanh(x):
    c = math.sqrt(2.0 / math.pi)
    return x * (0.5 * (1.0 + jnp.tanh(c * (x + 0.044715 * (x * x * x)))))


def _dot(a, b):
    return jnp.dot(a, b, preferred_element_type=F32)


def _const_spec(shape):
    n = len(shape)
    return pl.BlockSpec(shape, lambda *_: (0,) * n, pipeline_mode=pl.Buffered(1))


def _mix_kernel(x_ref, g_ref, w_ref, lng_ref, lnb_ref, ws_ref, bs_ref, conv_ref, wpool_ref,
                dscale_ref, pa_ref, pb_ref, pd_ref, q0_ref, q1_ref, q2_ref,
                h_scr, bbuf, dbuf):
    ts = MIX_TILE
    i = pl.program_id(1)
    x = x_ref[0]
    h = _rmsnorm(x, g_ref[...])
    for c in range(D_MODEL // 128):
        h_scr[c] = h[:, c * 128:(c + 1) * 128]
    hb = h.astype(BF16)
    zl = _dot(hb, w_ref[:, 0:W_LOCAL])

    u = _gelu_tanh(zl[:, 0:256])
    v = _gelu_tanh(zl[:, 256:512])
    mu = jnp.mean(v, axis=-1, keepdims=True)
    vc = v - mu
    var = jnp.mean(vc * vc, axis=-1, keepdims=True)
    v = vc * lax.rsqrt(var + EPS) * lng_ref[...] + lnb_ref[...]
    row = lax.broadcasted_iota(jnp.int32, (CHUNK, A_GROUPS * CHUNK), 0)
    col = lax.broadcasted_iota(jnp.int32, (CHUNK, A_GROUPS * CHUNK), 1) % CHUNK
    wtril = jnp.where(row >= col, ws_ref[...], 0.0).astype(BF16)
    grp = lax.broadcasted_iota(jnp.int32, (CHUNK, W_BRANCH), 1) // (W_BRANCH // A_GROUPS)
    svs = []
    for c in range(ts // CHUNK):
        vch = v[c * CHUNK:(c + 1) * CHUNK]
        stacked = jnp.concatenate(
            [jnp.where(grp == g, vch, 0.0) for g in range(A_GROUPS)], axis=0).astype(BF16)
        svs.append(_dot(wtril, stacked) + bs_ref[...])
    sv = jnp.concatenate(svs, axis=0)
    pa_ref[0] = (u * sv).astype(BF16)

    prod = zl[:, 768:1024] * zl[:, 1024:1280]

    @pl.when(i == 0)
    def _():
        bbuf[0:CONV_HALO] = jnp.zeros((CONV_HALO, W_BRANCH), F32)
        dbuf[0:POOL_HALO] = jnp.zeros((POOL_HALO, W_BRANCH), F32)

    @pl.when(i > 0)
    def _():
        bbuf[0:CONV_HALO] = bbuf[ts:ts + CONV_HALO]
        dbuf[0:POOL_HALO] = dbuf[ts:ts + POOL_HALO]

    bbuf[CONV_HALO:ts + CONV_HALO] = prod
    cw = conv_ref[...]
    conv = (cw[0:1] * bbuf[pl.ds(CONV_HALO - 2, ts), :]
            + cw[1:2] * bbuf[pl.ds(CONV_HALO - 1, ts), :]
            + cw[2:3] * prod)
    pb_ref[0] = (zl[:, 512:768] * conv).astype(BF16)

    dz = zl[:, 1280:1536]
    dbuf[POOL_HALO:ts + POOL_HALO] = dz
    lane = lax.broadcasted_iota(jnp.int32, (ts, 128), 1)
    first_half = lane < 64
    tpos = (lax.broadcasted_iota(jnp.int32, (ts, 128), 0) + (i * ts + 1)).astype(F32)

    def window_sums(e, levels):
        out = []
        s = e
        for k in range(levels):
            s = s + pltpu.roll(s, 1 << k, 0)
            out.append(s[POOL_HALO:])
        return out

    lo = window_sums(dbuf[:, 0:128], 2)
    hi = window_sums(dbuf[:, 128:256], 4)
    pooled_lo = jnp.where(first_half, lo[0], lo[1]) / jnp.minimum(tpos, jnp.where(first_half, 2.0, 4.0))
    pooled_hi = jnp.where(first_half, hi[2], hi[3]) / jnp.minimum(tpos, jnp.where(first_half, 8.0, 16.0))
    y = jnp.concatenate([pooled_lo, pooled_hi], axis=1) - dz
    pd_ref[0] = (_dot(y.astype(BF16), wpool_ref[...]) * dscale_ref[...]).astype(BF16)

    lane_q = lax.broadcasted_iota(jnp.int32, (ts, 128), 1)
    sub0 = lane_q < HEAD_DIM

    def emit_qkv(z, store):
        q = z[:, 0:256] * (HEAD_DIM ** -0.5)
        pieces = []
        for pr in range(2):
            qp = q[:, pr * 128:(pr + 1) * 128]
            pieces.append(jnp.where(sub0, qp, 0.0))
            pieces.append(jnp.where(sub0, 0.0, qp))
            pieces.append(z[:, 256 + pr * 128:256 + (pr + 1) * 128])
            pieces.append(z[:, 512 + pr * 128:512 + (pr + 1) * 128])
        store(jnp.concatenate(pieces, axis=1).astype(BF16))

    def store0(val):
        q0_ref[0] = val

    emit_qkv(_dot(hb, w_ref[:, W_LOCAL:W_LOCAL + W_QKV]), store0)

    for gi, (q_ref, dil) in enumerate(((q1_ref, DILATIONS[1]), (q2_ref, DILATIONS[2]))):
        n = ts // dil
        hp = jnp.concatenate(
            [jnp.concatenate([h_scr[c, pl.ds(r, n, stride=dil), :] for r in range(dil)], axis=0)
             for c in range(D_MODEL // 128)], axis=1).astype(BF16)
        c0 = W_LOCAL + (gi + 1) * W_QKV

        def store_perm(val, q_ref=q_ref, dil=dil, n=n):
            for r in range(dil):
                q_ref[0, r] = val[r * n:(r + 1) * n]

        emit_qkv(_dot(hp, w_ref[:, c0:c0 + W_QKV]), store_perm)


def _mix_call(x, g, w_mix, lng, lnb, ws, bs_full, conv, wpool_bd, dscale):
    B, S, D = x.shape
    ts = MIX_TILE
    nt = S // ts
    tok = lambda b, i: (b, i, 0)
    out_shape = (
        jax.ShapeDtypeStruct((B, S, W_BRANCH), BF16),
        jax.ShapeDtypeStruct((B, S, W_BRANCH), BF16),
        jax.ShapeDtypeStruct((B, S, W_BRANCH), BF16),
        jax.ShapeDtypeStruct((B, S, W_QKV_OUT), BF16),
        jax.ShapeDtypeStruct((B, DILATIONS[1], S // DILATIONS[1], W_QKV_OUT), BF16),
        jax.ShapeDtypeStruct((B, DILATIONS[2], S // DILATIONS[2], W_QKV_OUT), BF16),
    )
    in_specs = [
        pl.BlockSpec((1, ts, D), tok),
        _const_spec((1, D)),
        _const_spec(w_mix.shape),
        _const_spec((1, W_BRANCH)),
        _const_spec((1, W_BRANCH)),
        _const_spec((CHUNK, A_GROUPS * CHUNK)),
        _const_spec((CHUNK, W_BRANCH)),
        _const_spec((CONV_WIDTH, W_BRANCH)),
        _const_spec((W_BRANCH, W_BRANCH)),
        _const_spec((1, W_BRANCH)),
    ]
    out_specs = (
        pl.BlockSpec((1, ts, W_BRANCH), tok),
        pl.BlockSpec((1, ts, W_BRANCH), tok),
        pl.BlockSpec((1, ts, W_BRANCH), tok),
        pl.BlockSpec((1, ts, W_QKV_OUT), tok),
        pl.BlockSpec((1, DILATIONS[1], ts // DILATIONS[1], W_QKV_OUT), lambda b, i: (b, 0, i, 0)),
        pl.BlockSpec((1, DILATIONS[2], ts // DILATIONS[2], W_QKV_OUT), lambda b, i: (b, 0, i, 0)),
    )
    return pl.pallas_call(
        _mix_kernel,
        grid=(B, nt),
        in_specs=in_specs,
        out_specs=out_specs,
        out_shape=out_shape,
        scratch_shapes=[
            pltpu.VMEM((D // 128, ts, 128), F32),
            pltpu.VMEM((ts + CONV_HALO, W_BRANCH), F32),
            pltpu.VMEM((ts + POOL_HALO, W_BRANCH), F32),
        ],
        compiler_params=pltpu.CompilerParams(
            dimension_semantics=("arbitrary", "arbitrary"), vmem_limit_bytes=VMEM_LIMIT),
        name="mix",
    )(x, g, w_mix, lng, lnb, ws, bs_full, conv, wpool_bd, dscale)


def _attn_kernel(q0_ref, q1_ref, q2_ref, bcur_ref, bprev_ref, out_ref, o_scr, l_scr):
    S = q0_ref.shape[1]
    pr = pl.program_id(1)
    refs = (q0_ref, q1_ref, q2_ref)
    nb = S // CHUNK
    sub0 = lax.broadcasted_iota(jnp.int32, (nb, CHUNK, 128), 2) < HEAD_DIM
    qk = lambda a, b: jnp.einsum("jqd,jkd->jqk", a, b, preferred_element_type=F32)
    pv = lambda a, b: jnp.einsum("jqk,jkd->jqd", a, b, preferred_element_type=F32)

    for g in range(3):
        ref = refs[g]
        dil = DILATIONS[g]
        blocks_per_seq = nb // dil
        k = ref[0, :, 256:384].reshape(nb, CHUNK, 128)
        v = ref[0, :, 384:512].reshape(nb, CHUNK, 128)
        os_, ls_ = [], []
        for sub in range(2):
            hh = 2 * pr + sub
            q = ref[0, :, sub * 128:(sub + 1) * 128].reshape(nb, CHUNK, 128)
            s_cur = qk(q, k) + bcur_ref[g, hh][None]
            row_max = jnp.max(s_cur, axis=-1, keepdims=True)
            if blocks_per_seq > 1:
                s_prev = qk(q[1:], k[:-1]) + bprev_ref[g, hh][None]
                if blocks_per_seq < nb:
                    j = lax.broadcasted_iota(jnp.int32, s_prev.shape, 0) + 1
                    s_prev = jnp.where(j % blocks_per_seq != 0, s_prev, NEG_BIG)
                s_prev = jnp.concatenate([jnp.full((1, CHUNK, CHUNK), NEG_BIG, F32), s_prev], axis=0)
                row_max = jnp.maximum(row_max, jnp.max(s_prev, axis=-1, keepdims=True))
            e_cur = jnp.exp(s_cur - row_max)
            den = jnp.sum(e_cur, axis=-1, keepdims=True)
            acc = pv(e_cur.astype(BF16), v)
            if blocks_per_seq > 1:
                e_prev = jnp.exp(s_prev - row_max)
                den = den + jnp.sum(e_prev, axis=-1, keepdims=True)
                acc_prev = pv(e_prev[1:].astype(BF16), v[:-1])
                acc = acc + jnp.concatenate([jnp.zeros((1, CHUNK, 128), F32), acc_prev], axis=0)
            os_.append(acc / den)
            ls_.append(jnp.broadcast_to(row_max + jnp.log(den), (nb, CHUNK, 128)))
        o = jnp.where(sub0, os_[0], os_[1])
        lse = jnp.where(sub0, ls_[0], ls_[1])
        for j in range(nb):
            r, blk = divmod(j, blocks_per_seq)
            rows = pl.ds(blk * CHUNK * dil + r, CHUNK, stride=dil) if dil > 1 else pl.ds(j * CHUNK, CHUNK)
            o_scr[g, rows, :] = o[j]
            l_scr[g, rows, :] = lse[j]

    rows = 256
    for c in range(S // rows):
        sl = pl.ds(c * rows, rows)
        l0, l1, l2 = l_scr[0, sl, :], l_scr[1, sl, :], l_scr[2, sl, :]
        m = jnp.maximum(jnp.maximum(l0, l1), l2)
        e0, e1, e2 = jnp.exp(l0 - m), jnp.exp(l1 - m), jnp.exp(l2 - m)
        num = e0 * o_scr[0, sl, :] + e1 * o_scr[1, sl, :] + e2 * o_scr[2, sl, :]
        out_ref[0, sl, :] = (num / (e0 + e1 + e2)).astype(BF16)


def _attn_call(q0, q1, q2, bcur, bprev):
    B, S, W = q0.shape
    wp = W // 2
    spec = pl.BlockSpec((1, S, wp), lambda b, p: (b, 0, p))
    return pl.pallas_call(
        _attn_kernel,
        grid=(B, 2),
        in_specs=[spec, spec, spec, _const_spec(bcur.shape), _const_spec(bprev.shape)],
        out_specs=pl.BlockSpec((1, S, 128), lambda b, p: (b, 0, p)),
        out_shape=jax.ShapeDtypeStruct((B, S, W_BRANCH), BF16),
        scratch_shapes=[
            pltpu.VMEM((3, S, 128), F32),
            pltpu.VMEM((3, S, 128), F32),
        ],
        compiler_params=pltpu.CompilerParams(
            dimension_semantics=("arbitrary", "arbitrary"), vmem_limit_bytes=VMEM_LIMIT),
        name="attn",
    )(q0, q1, q2, bcur, bprev)


def _merge_kernel(x_ref, pa_ref, pb_ref, pc_ref, pd_ref, g_ref, wg_ref, wout_ref, wo_ref, o_ref):
    x = x_ref[...]
    hb = _rmsnorm(x, g_ref[...]).astype(BF16)
    merged = None
    for br, p_ref in enumerate((pa_ref, pb_ref, pc_ref, pd_ref)):
        gate = jax.nn.sigmoid(_dot(hb, wg_ref[:, br * D_MODEL:(br + 1) * D_MODEL]))
        term = gate * _dot(p_ref[...], wout_ref[br])
        merged = term if merged is None else merged + term
    o_ref[...] = x + _dot(merged.astype(BF16), wo_ref[...])


def _merge_call(x, pa, pb, pc, pd, g, w_gate, w_out, w_o):
    T, D = x.shape
    tt = TOK_TILE
    tok = lambda i: (i, 0)
    pspec = pl.BlockSpec((tt, W_BRANCH), tok)
    return pl.pallas_call(
        _merge_kernel,
        grid=(T // tt,),
        in_specs=[pl.BlockSpec((tt, D), tok), pspec, pspec, pspec, pspec,
                  _const_spec((1, D)), _const_spec(w_gate.shape), _const_spec(w_out.shape),
                  _const_spec(w_o.shape)],
        out_specs=pl.BlockSpec((tt, D), tok),
        out_shape=jax.ShapeDtypeStruct((T, D), F32),
        compiler_params=pltpu.CompilerParams(
            dimension_semantics=("arbitrary",), vmem_limit_bytes=VMEM_LIMIT),
        name="merge",
    )(x, pa, pb, pc, pd, g, w_gate, w_out, w_o)


def _ffn_kernel(x_ref, g_ref, w1_ref, w2_ref, fg_ref, o_ref, *, final_norm):
    x = x_ref[...]
    hb = _rmsnorm(x, g_ref[...]).astype(BF16)
    acts = []
    for c in range(D_FF // D_MODEL):
        a = _dot(hb, w1_ref[:, c * D_MODEL:(c + 1) * D_MODEL])
        acts.append(jnp.square(jnp.maximum(a, 0.0)).astype(BF16))
    y = x + _dot(jnp.concatenate(acts, axis=1), w2_ref[...])
    if final_norm:
        y = _rmsnorm(y, fg_ref[...])
    o_ref[...] = y


def _ffn_call(x, g, w1, w2, final_g, final_norm):
    T, D = x.shape
    tt = TOK_TILE
    tok = lambda i: (i, 0)
    return pl.pallas_call(
        functools.partial(_ffn_kernel, final_norm=final_norm),
        grid=(T // tt,),
        in_specs=[pl.BlockSpec((tt, D), tok), _const_spec((1, D)), _const_spec(w1.shape),
                  _const_spec(w2.shape), _const_spec((1, D))],
        out_specs=pl.BlockSpec((tt, D), tok),
        out_shape=jax.ShapeDtypeStruct((T, D), F32),
        compiler_params=pltpu.CompilerParams(
            dimension_semantics=("arbitrary",), vmem_limit_bytes=VMEM_LIMIT),
        name="ffn",
    )(x, g, w1, w2, final_g)


def _t5_bucket_np(dist):
    max_exact = N_BUCKETS // 2
    d_f = np.maximum(dist, 1).astype(np.float32)
    ratio = np.log(d_f / np.float32(max_exact)) / np.float32(math.log(MAX_DISTANCE / max_exact))
    large = max_exact + (ratio * np.float32(N_BUCKETS - max_exact)).astype(np.int32)
    large = np.minimum(large, N_BUCKETS - 1)
    return np.where(dist < max_exact, dist, large)


def _bucket_indices():
    qi = np.arange(CHUNK)[:, None]
    ki = np.arange(CHUNK)[None, :]
    d_cur = qi - ki
    d_prev = qi + CHUNK - ki
    cur = [np.where(d_cur >= 0, _t5_bucket_np(np.clip(d_cur, 0, None) * dil), -1) for dil in DILATIONS]
    prev = [np.where(d_prev <= CHUNK, _t5_bucket_np(d_prev * dil), -1) for dil in DILATIONS[:2]]
    return np.stack(cur).astype(np.int32), np.stack(prev).astype(np.int32)


def _bias_kernel(tab_ref, icur_ref, iprev_ref, bcur_ref, bprev_ref):
    for idx_ref, out_ref in ((icur_ref, bcur_ref), (iprev_ref, bprev_ref)):
        for g in range(idx_ref.shape[0]):
            idx = idx_ref[g]
            accs = [jnp.full((CHUNK, CHUNK), NEG_BIG, F32) for _ in range(HEADS_PER_GROUP)]
            for b in range(N_BUCKETS):
                hit = idx == b
                for h in range(HEADS_PER_GROUP):
                    accs[h] = jnp.where(hit, tab_ref[b, g * HEADS_PER_GROUP + h], accs[h])
            for h in range(HEADS_PER_GROUP):
                out_ref[g, h] = accs[h]


def _bias_tables(rel_bias):
    icur, iprev = _bucket_indices()
    vm = pl.BlockSpec(memory_space=pltpu.VMEM)
    return pl.pallas_call(
        _bias_kernel,
        in_specs=[pl.BlockSpec(memory_space=pltpu.SMEM), vm, vm],
        out_specs=(vm, vm),
        out_shape=(jax.ShapeDtypeStruct((3, HEADS_PER_GROUP, CHUNK, CHUNK), F32),
                   jax.ShapeDtypeStruct((2, HEADS_PER_GROUP, CHUNK, CHUNK), F32)),
        name="bias",
    )(rel_bias, jnp.asarray(icur), jnp.asarray(iprev))


def kernel(x, norm_mix_g, w_in, a_ln_g, a_ln_b, a_ws, a_bs, w_a_out, b_conv, w_b_out, rel_bias,
           w_c_out, d_w, d_scale, w_d_out, w_o, norm_ff_g, w_ff1, w_ff2, final_g):
    B, S, D = x.shape
    depth = w_in.shape[0]
    bcur, bprev = _bias_tables(rel_bias)
    c_q0 = 5 * W_BRANCH
    c_k0 = c_q0 + 3 * W_BRANCH
    c_v0 = c_k0 + 3 * W_BRANCH
    c_d0 = c_v0 + 3 * W_BRANCH
    for l in range(depth):
        wl = w_in[l]
        cols = [wl[:, 0:c_q0], wl[:, c_d0:c_d0 + W_BRANCH]]
        for g in range(3):
            sl = slice(g * W_BRANCH, (g + 1) * W_BRANCH)
            cols += [wl[:, c_q0:c_k0][:, sl], wl[:, c_k0:c_v0][:, sl], wl[:, c_v0:c_d0][:, sl]]
        w_mix = jnp.concatenate(cols, axis=1).astype(BF16)
        w_gate = wl[:, GATE_COL0:].astype(BF16)
        ws = jnp.transpose(a_ws[l], (1, 0, 2)).reshape(CHUNK, A_GROUPS * CHUNK)
        bs_full = jnp.repeat(a_bs[l].T, W_BRANCH // A_GROUPS, axis=1)
        wpool_bd = jax.scipy.linalg.block_diag(*[d_w[l, g] for g in range(4)]).astype(BF16)
        w_out = jnp.stack([w_a_out[l], w_b_out[l], w_c_out[l], w_d_out[l]]).astype(BF16)

        pa, pb, pd, q0, q1, q2 = _mix_call(
            x, norm_mix_g[l][None], w_mix, a_ln_g[l][None], a_ln_b[l][None], ws, bs_full,
            b_conv[l], wpool_bd, d_scale[l][None])
        pc = _attn_call(q0, q1.reshape(B, S, W_QKV_OUT), q2.reshape(B, S, W_QKV_OUT), bcur, bprev)

        T = B * S
        x2 = _merge_call(x.reshape(T, D), pa.reshape(T, -1), pb.reshape(T, -1), pc.reshape(T, -1),
                         pd.reshape(T, -1), norm_mix_g[l][None], w_gate, w_out, w_o[l].astype(BF16))
        x2 = _ffn_call(x2, norm_ff_g[l][None], w_ff1[l].astype(BF16), w_ff2[l].astype(BF16),
                       final_g[None], final_norm=(l == depth - 1))
        x = x2.reshape(B, S, D)
    return x
```

```python
import functools
import math

import jax
import jax.numpy as jnp
import numpy as np
from jax import lax
from jax.experimental import pallas as pl
from jax.experimental.pallas import tpu as pltpu

F32 = jnp.float32
BF16 = jnp.bfloat16

D_MODEL = 1024
W_BRANCH = 256
A_GROUPS = 4
CHUNK = 128
CONV_WIDTH = 3
DILATIONS = (1, 4, 16)
WINDOWS = (128, 512, 2048)
HEADS_PER_GROUP = 4
HEAD_DIM = 64
POOL_WINDOWS = (2, 4, 8, 16)
POOL_HALO = 16
CONV_HALO = 8
N_BRANCH = 4
D_FF = 4 * D_MODEL
N_BUCKETS = 32
MAX_DISTANCE = 2048
EPS = 1e-6
NEG_BIG = -1e30

W_LOCAL = 6 * W_BRANCH
W_QKV = 3 * W_BRANCH
W_QKV_OUT = 4 * W_BRANCH
GATE_COL0 = 5 * W_BRANCH + 3 * W_QKV + W_BRANCH

MIX_TILE = 512
TOK_TILE = 1024
SUB_TILE = 512
ATTN_BLOCKS_PER_PIECE = 8
VMEM_LIMIT = 56 * 1024 * 1024


def _rmsnorm(x, g):
    return x * lax.rsqrt(jnp.mean(x * x, axis=-1, keepdims=True) + EPS) * g


def _gelu_tanh(x):
    c = math.sqrt(2.0 / math.pi)
    return x * (0.5 * (1.0 + jnp.tanh(c * (x + 0.044715 * (x * x * x)))))


def _dot(a, b):
    return jnp.dot(a, b, preferred_element_type=F32)


def _const_spec(shape):
    n = len(shape)
    return pl.BlockSpec(shape, lambda *_: (0,) * n, pipeline_mode=pl.Buffered(1))


def _mix_kernel(x_ref, g_ref, w_ref, lng_ref, lnb_ref, ws_ref, bs_ref, conv_ref, wpool_ref,
                dscale_ref, pa_ref, pb_ref, pd_ref, qkv_ref, bbuf, dbuf):
    ts = MIX_TILE
    i = pl.program_id(1)
    x = x_ref[0]
    hb = _rmsnorm(x, g_ref[...]).astype(BF16)
    zl = _dot(hb, w_ref[:, 0:W_LOCAL])

    u = _gelu_tanh(zl[:, 0:256])
    v = _gelu_tanh(zl[:, 256:512])
    mu = jnp.mean(v, axis=-1, keepdims=True)
    vc = v - mu
    var = jnp.mean(vc * vc, axis=-1, keepdims=True)
    v = vc * lax.rsqrt(var + EPS) * lng_ref[...] + lnb_ref[...]
    row = lax.broadcasted_iota(jnp.int32, (CHUNK, A_GROUPS * CHUNK), 0)
    col = lax.broadcasted_iota(jnp.int32, (CHUNK, A_GROUPS * CHUNK), 1) % CHUNK
    wtril = jnp.where(row >= col, ws_ref[...], 0.0).astype(BF16)
    grp = lax.broadcasted_iota(jnp.int32, (CHUNK, W_BRANCH), 1) // (W_BRANCH // A_GROUPS)
    svs = []
    for c in range(ts // CHUNK):
        vch = v[c * CHUNK:(c + 1) * CHUNK]
        stacked = jnp.concatenate(
            [jnp.where(grp == g, vch, 0.0) for g in range(A_GROUPS)], axis=0).astype(BF16)
        svs.append(_dot(wtril, stacked) + bs_ref[...])
    sv = jnp.concatenate(svs, axis=0)
    pa_ref[0] = (u * sv).astype(BF16)

    prod = zl[:, 768:1024] * zl[:, 1024:1280]

    @pl.when(i == 0)
    def _():
        bbuf[0:CONV_HALO] = jnp.zeros((CONV_HALO, W_BRANCH), F32)
        dbuf[0:POOL_HALO] = jnp.zeros((POOL_HALO, W_BRANCH), F32)

    @pl.when(i > 0)
    def _():
        bbuf[0:CONV_HALO] = bbuf[ts:ts + CONV_HALO]
        dbuf[0:POOL_HALO] = dbuf[ts:ts + POOL_HALO]

    bbuf[CONV_HALO:ts + CONV_HALO] = prod
    cw = conv_ref[...]
    conv = (cw[0:1] * bbuf[pl.ds(CONV_HALO - 2, ts), :]
            + cw[1:2] * bbuf[pl.ds(CONV_HALO - 1, ts), :]
            + cw[2:3] * prod)
    pb_ref[0] = (zl[:, 512:768] * conv).astype(BF16)

    dz = zl[:, 1280:1536]
    dbuf[POOL_HALO:ts + POOL_HALO] = dz
    lane = lax.broadcasted_iota(jnp.int32, (ts, 128), 1)
    first_half = lane < 64
    tpos = (lax.broadcasted_iota(jnp.int32, (ts, 128), 0) + (i * ts + 1)).astype(F32)

    def window_sums(e, levels):
        out = []
        s = e
        for k in range(levels):
            s = s + pltpu.roll(s, 1 << k, 0)
            out.append(s[POOL_HALO:])
        return out

    lo = window_sums(dbuf[:, 0:128], 2)
    hi = window_sums(dbuf[:, 128:256], 4)
    pooled_lo = jnp.where(first_half, lo[0], lo[1]) / jnp.minimum(tpos, jnp.where(first_half, 2.0, 4.0))
    pooled_hi = jnp.where(first_half, hi[2], hi[3]) / jnp.minimum(tpos, jnp.where(first_half, 8.0, 16.0))
    y = jnp.concatenate([pooled_lo, pooled_hi], axis=1) - dz
    pd_ref[0] = (_dot(y.astype(BF16), wpool_ref[...]) * dscale_ref[...]).astype(BF16)

    sub0 = lax.broadcasted_iota(jnp.int32, (ts, 128), 1) < HEAD_DIM
    for g in range(len(DILATIONS)):
        c0 = W_LOCAL + g * W_QKV
        z = _dot(hb, w_ref[:, c0:c0 + W_QKV])
        q = z[:, 0:256] * (HEAD_DIM ** -0.5)
        pieces = []
        for pr in range(2):
            qp = q[:, pr * 128:(pr + 1) * 128]
            pieces.append(jnp.where(sub0, qp, 0.0))
            pieces.append(jnp.where(sub0, 0.0, qp))
            pieces.append(z[:, 256 + pr * 128:256 + (pr + 1) * 128])
            pieces.append(z[:, 512 + pr * 128:512 + (pr + 1) * 128])
        qkv_ref[0, :, g * W_QKV_OUT:(g + 1) * W_QKV_OUT] = jnp.concatenate(pieces, axis=1).astype(BF16)


def _mix_call(x, g, w_mix, lng, lnb, ws, bs_full, conv, wpool_bd, dscale):
    B, S, D = x.shape
    ts = MIX_TILE
    nt = S // ts
    tok = lambda b, i: (b, i, 0)
    out_shape = (
        jax.ShapeDtypeStruct((B, S, W_BRANCH), BF16),
        jax.ShapeDtypeStruct((B, S, W_BRANCH), BF16),
        jax.ShapeDtypeStruct((B, S, W_BRANCH), BF16),
        jax.ShapeDtypeStruct((B, S, len(DILATIONS) * W_QKV_OUT), BF16),
    )
    in_specs = [
        pl.BlockSpec((1, ts, D), tok),
        _const_spec((1, D)),
        _const_spec(w_mix.shape),
        _const_spec((1, W_BRANCH)),
        _const_spec((1, W_BRANCH)),
        _const_spec((CHUNK, A_GROUPS * CHUNK)),
        _const_spec((CHUNK, W_BRANCH)),
        _const_spec((CONV_WIDTH, W_BRANCH)),
        _const_spec((W_BRANCH, W_BRANCH)),
        _const_spec((1, W_BRANCH)),
    ]
    out_specs = (
        pl.BlockSpec((1, ts, W_BRANCH), tok),
        pl.BlockSpec((1, ts, W_BRANCH), tok),
        pl.BlockSpec((1, ts, W_BRANCH), tok),
        pl.BlockSpec((1, ts, len(DILATIONS) * W_QKV_OUT), tok),
    )
    return pl.pallas_call(
        _mix_kernel,
        grid=(B, nt),
        in_specs=in_specs,
        out_specs=out_specs,
        out_shape=out_shape,
        scratch_shapes=[
            pltpu.VMEM((ts + CONV_HALO, W_BRANCH), F32),
            pltpu.VMEM((ts + POOL_HALO, W_BRANCH), F32),
        ],
        compiler_params=pltpu.CompilerParams(
            dimension_semantics=("arbitrary", "arbitrary"), vmem_limit_bytes=VMEM_LIMIT),
        name="mix",
    )(x, g, w_mix, lng, lnb, ws, bs_full, conv, wpool_bd, dscale)


def _attn_kernel(*refs):
    n_in = sum(DILATIONS)
    res_refs = [refs[sum(DILATIONS[:g]):sum(DILATIONS[:g + 1])] for g in range(len(DILATIONS))]
    bcur_ref, bprev_ref, out_ref, o_scr, l_scr = refs[n_in:]
    S = out_ref.shape[1]
    pr = pl.program_id(1)
    nb = S // CHUNK

    def cols(g, c0):
        parts = [r_ref[0, :, c0:c0 + 128] for r_ref in res_refs[g]]
        return (parts[0] if len(parts) == 1 else jnp.concatenate(parts, axis=0)).reshape(nb, CHUNK, 128)

    jb = ATTN_BLOCKS_PER_PIECE
    sub0 = lax.broadcasted_iota(jnp.int32, (jb, CHUNK, 128), 2) < HEAD_DIM
    qk = lambda a, b: jnp.einsum("jqd,jkd->jqk", a, b, preferred_element_type=F32)
    pv = lambda a, b: jnp.einsum("jqk,jkd->jqd", a, b, preferred_element_type=F32)
    hh0 = 2 * pr

    for g, dil in enumerate(DILATIONS):
        blocks_per_seq = nb // dil
        has_prev = blocks_per_seq > 1
        q = jnp.concatenate([cols(g, 0), cols(g, 128)], axis=1)
        k = cols(g, 256)
        v = cols(g, 384)
        bias_cur = jnp.concatenate([bcur_ref[g, hh0], bcur_ref[g, hh0 + 1]], axis=0)[None]
        if has_prev:
            k = jnp.concatenate([jnp.concatenate([k[:1], k[:-1]], axis=0), k], axis=1)
            v = jnp.concatenate([jnp.concatenate([v[:1], v[:-1]], axis=0), v], axis=1)
            bias_prev = jnp.concatenate([bprev_ref[g, hh0], bprev_ref[g, hh0 + 1]], axis=0)[None]
        for j0 in range(0, nb, jb):
            s = qk(q[j0:j0 + jb], k[j0:j0 + jb])
            s_cur = s[:, :, -CHUNK:] + bias_cur
            top = s_cur
            if has_prev:
                j = lax.broadcasted_iota(jnp.int32, (jb, 2 * CHUNK, CHUNK), 0) + j0
                s_prev = jnp.where(j % blocks_per_seq != 0, s[:, :, :CHUNK] + bias_prev, NEG_BIG)
                top = jnp.maximum(s_cur, s_prev)
            row_max = jnp.max(top, axis=-1, keepdims=True)
            e = jnp.exp(s_cur - row_max)
            den = e
            if has_prev:
                e_prev = jnp.exp(s_prev - row_max)
                den = e + e_prev
                e = jnp.concatenate([e_prev, e], axis=-1)
            den = jnp.sum(den, axis=-1, keepdims=True)
            acc = pv(e.astype(BF16), v[j0:j0 + jb])
            o2 = acc / den
            l2 = jnp.broadcast_to(row_max + jnp.log(den), o2.shape)
            o = jnp.where(sub0, o2[:, :CHUNK], o2[:, CHUNK:])
            lse = jnp.where(sub0, l2[:, :CHUNK], l2[:, CHUNK:])
            for jj in range(jb):
                r, blk = divmod(j0 + jj, blocks_per_seq)
                rows = (pl.ds(blk * CHUNK * dil + r, CHUNK, stride=dil) if dil > 1
                        else pl.ds((j0 + jj) * CHUNK, CHUNK))
                o_scr[g, rows, :] = o[jj]
                l_scr[g, rows, :] = lse[jj]

    rows = 256
    for c in range(S // rows):
        sl = pl.ds(c * rows, rows)
        l0, l1, l2 = l_scr[0, sl, :], l_scr[1, sl, :], l_scr[2, sl, :]
        m = jnp.maximum(jnp.maximum(l0, l1), l2)
        e0, e1, e2 = jnp.exp(l0 - m), jnp.exp(l1 - m), jnp.exp(l2 - m)
        num = e0 * o_scr[0, sl, :] + e1 * o_scr[1, sl, :] + e2 * o_scr[2, sl, :]
        out_ref[0, sl, :] = (num / (e0 + e1 + e2)).astype(BF16)


def _attn_call(qkv, bcur, bprev):
    B, S, W = qkv.shape
    wp = W_QKV_OUT // 2
    operands, specs = [], []
    for gi, dil in enumerate(DILATIONS):
        view = qkv.reshape(B, S // dil, dil * W)
        for r in range(dil):
            blk = (r * W + gi * W_QKV_OUT) // wp
            operands.append(view)
            specs.append(pl.BlockSpec((1, S // dil, wp), lambda b, p, blk=blk: (b, 0, blk + p)))
    return pl.pallas_call(
        _attn_kernel,
        grid=(B, 2),
        in_specs=specs + [_const_spec(bcur.shape), _const_spec(bprev.shape)],
        out_specs=pl.BlockSpec((1, S, 128), lambda b, p: (b, 0, p)),
        out_shape=jax.ShapeDtypeStruct((B, S, W_BRANCH), BF16),
        scratch_shapes=[
            pltpu.VMEM((3, S, 128), F32),
            pltpu.VMEM((3, S, 128), F32),
        ],
        compiler_params=pltpu.CompilerParams(
            dimension_semantics=("arbitrary", "arbitrary"), vmem_limit_bytes=VMEM_LIMIT),
        name="attn",
    )(*operands, bcur, bprev)


def _merge_kernel(x_ref, pa_ref, pb_ref, pc_ref, pd_ref, g_ref, wg_ref, wout_ref, wo_ref, o_ref):
    for s in range(TOK_TILE // SUB_TILE):
        rows = pl.ds(s * SUB_TILE, SUB_TILE)
        x = x_ref[rows, :]
        hb = _rmsnorm(x, g_ref[...]).astype(BF16)
        merged = None
        for br, p_ref in enumerate((pa_ref, pb_ref, pc_ref, pd_ref)):
            gate = jax.nn.sigmoid(_dot(hb, wg_ref[:, br * D_MODEL:(br + 1) * D_MODEL]))
            term = gate * _dot(p_ref[rows, :], wout_ref[br])
            merged = term if merged is None else merged + term
        o_ref[rows, :] = x + _dot(merged.astype(BF16), wo_ref[...])


def _merge_call(x, pa, pb, pc, pd, g, w_gate, w_out, w_o):
    T, D = x.shape
    tt = TOK_TILE
    tok = lambda i: (i, 0)
    pspec = pl.BlockSpec((tt, W_BRANCH), tok)
    return pl.pallas_call(
        _merge_kernel,
        grid=(T // tt,),
        in_specs=[pl.BlockSpec((tt, D), tok), pspec, pspec, pspec, pspec,
                  _const_spec((1, D)), _const_spec(w_gate.shape), _const_spec(w_out.shape),
                  _const_spec(w_o.shape)],
        out_specs=pl.BlockSpec((tt, D), tok),
        out_shape=jax.ShapeDtypeStruct((T, D), F32),
        compiler_params=pltpu.CompilerParams(
            dimension_semantics=("arbitrary",), vmem_limit_bytes=VMEM_LIMIT),
        name="merge",
    )(x, pa, pb, pc, pd, g, w_gate, w_out, w_o)


def _ffn_kernel(x_ref, g_ref, w1_ref, w2_ref, fg_ref, o_ref, *, final_norm):
    for s in range(TOK_TILE // SUB_TILE):
        rows = pl.ds(s * SUB_TILE, SUB_TILE)
        x = x_ref[rows, :]
        hb = _rmsnorm(x, g_ref[...]).astype(BF16)
        acts = []
        for c in range(D_FF // D_MODEL):
            a = _dot(hb, w1_ref[:, c * D_MODEL:(c + 1) * D_MODEL])
            acts.append(jnp.square(jnp.maximum(a, 0.0)).astype(BF16))
        y = x + _dot(jnp.concatenate(acts, axis=1), w2_ref[...])
        if final_norm:
            y = _rmsnorm(y, fg_ref[...])
        o_ref[rows, :] = y


def _ffn_call(x, g, w1, w2, final_g, final_norm):
    T, D = x.shape
    tt = TOK_TILE
    tok = lambda i: (i, 0)
    return pl.pallas_call(
        functools.partial(_ffn_kernel, final_norm=final_norm),
        grid=(T // tt,),
        in_specs=[pl.BlockSpec((tt, D), tok), _const_spec((1, D)), _const_spec(w1.shape),
                  _const_spec(w2.shape), _const_spec((1, D))],
        out_specs=pl.BlockSpec((tt, D), tok),
        out_shape=jax.ShapeDtypeStruct((T, D), F32),
        compiler_params=pltpu.CompilerParams(
            dimension_semantics=("arbitrary",), vmem_limit_bytes=VMEM_LIMIT),
        name="ffn",
    )(x, g, w1, w2, final_g)


def _t5_bucket_np(dist):
    max_exact = N_BUCKETS // 2
    d_f = np.maximum(dist, 1).astype(np.float32)
    ratio = np.log(d_f / np.float32(max_exact)) / np.float32(math.log(MAX_DISTANCE / max_exact))
    large = max_exact + (ratio * np.float32(N_BUCKETS - max_exact)).astype(np.int32)
    large = np.minimum(large, N_BUCKETS - 1)
    return np.where(dist < max_exact, dist, large)


def _bucket_indices():
    qi = np.arange(CHUNK)[:, None]
    ki = np.arange(CHUNK)[None, :]
    d_cur = qi - ki
    d_prev = qi + CHUNK - ki
    cur = [np.where(d_cur >= 0, _t5_bucket_np(np.clip(d_cur, 0, None) * dil), -1) for dil in DILATIONS]
    prev = [np.where(d_prev <= CHUNK, _t5_bucket_np(d_prev * dil), -1) for dil in DILATIONS[:2]]
    return np.stack(cur).astype(np.int32), np.stack(prev).astype(np.int32)


def _bias_kernel(tab_ref, icur_ref, iprev_ref, bcur_ref, bprev_ref):
    for idx_ref, out_ref in ((icur_ref, bcur_ref), (iprev_ref, bprev_ref)):
        for g in range(idx_ref.shape[0]):
            idx = idx_ref[g]
            accs = [jnp.full((CHUNK, CHUNK), NEG_BIG, F32) for _ in range(HEADS_PER_GROUP)]
            for b in range(N_BUCKETS):
                hit = idx == b
                for h in range(HEADS_PER_GROUP):
                    accs[h] = jnp.where(hit, tab_ref[b, g * HEADS_PER_GROUP + h], accs[h])
            for h in range(HEADS_PER_GROUP):
                out_ref[g, h] = accs[h]


def _bias_tables(rel_bias):
    icur, iprev = _bucket_indices()
    vm = pl.BlockSpec(memory_space=pltpu.VMEM)
    return pl.pallas_call(
        _bias_kernel,
        in_specs=[pl.BlockSpec(memory_space=pltpu.SMEM), vm, vm],
        out_specs=(vm, vm),
        out_shape=(jax.ShapeDtypeStruct((3, HEADS_PER_GROUP, CHUNK, CHUNK), F32),
                   jax.ShapeDtypeStruct((2, HEADS_PER_GROUP, CHUNK, CHUNK), F32)),
        name="bias",
    )(rel_bias, jnp.asarray(icur), jnp.asarray(iprev))


def kernel(x, norm_mix_g, w_in, a_ln_g, a_ln_b, a_ws, a_bs, w_a_out, b_conv, w_b_out, rel_bias,
           w_c_out, d_w, d_scale, w_d_out, w_o, norm_ff_g, w_ff1, w_ff2, final_g):
    B, S, D = x.shape
    depth = w_in.shape[0]
    bcur, bprev = _bias_tables(rel_bias)
    c_q0 = 5 * W_BRANCH
    c_k0 = c_q0 + 3 * W_BRANCH
    c_v0 = c_k0 + 3 * W_BRANCH
    c_d0 = c_v0 + 3 * W_BRANCH
    for l in range(depth):
        wl = w_in[l]
        cols = [wl[:, 0:c_q0], wl[:, c_d0:c_d0 + W_BRANCH]]
        for g in range(3):
            sl = slice(g * W_BRANCH, (g + 1) * W_BRANCH)
            cols += [wl[:, c_q0:c_k0][:, sl], wl[:, c_k0:c_v0][:, sl], wl[:, c_v0:c_d0][:, sl]]
        w_mix = jnp.concatenate(cols, axis=1).astype(BF16)
        w_gate = wl[:, GATE_COL0:].astype(BF16)
        ws = jnp.transpose(a_ws[l], (1, 0, 2)).reshape(CHUNK, A_GROUPS * CHUNK)
        bs_full = jnp.repeat(a_bs[l].T, W_BRANCH // A_GROUPS, axis=1)
        wpool_bd = jax.scipy.linalg.block_diag(*[d_w[l, g] for g in range(4)]).astype(BF16)
        w_out = jnp.stack([w_a_out[l], w_b_out[l], w_c_out[l], w_d_out[l]]).astype(BF16)

        pa, pb, pd, qkv = _mix_call(
            x, norm_mix_g[l][None], w_mix, a_ln_g[l][None], a_ln_b[l][None], ws, bs_full,
            b_conv[l], wpool_bd, d_scale[l][None])
        pc = _attn_call(qkv, bcur, bprev)

        T = B * S
        x2 = _merge_call(x.reshape(T, D), pa.reshape(T, -1), pb.reshape(T, -1), pc.reshape(T, -1),
                         pd.reshape(T, -1), norm_mix_g[l][None], w_gate, w_out, w_o[l].astype(BF16))
        x2 = _ffn_call(x2, norm_ff_g[l][None], w_ff1[l].astype(BF16), w_ff2[l].astype(BF16),
                       final_g[None], final_norm=(l == depth - 1))
        x = x2.reshape(B, S, D)
    return x
```

```python
import functools
import math

import jax
import jax.numpy as jnp
import numpy as np
from jax import lax
from jax.experimental import pallas as pl
from jax.experimental.pallas import tpu as pltpu

F32 = jnp.float32
BF16 = jnp.bfloat16

D_MODEL = 1024
W_BRANCH = 256
A_GROUPS = 4
CHUNK = 128
CONV_WIDTH = 3
DILATIONS = (1, 4, 16)
WINDOWS = (128, 512, 2048)
HEADS_PER_GROUP = 4
HEAD_DIM = 64
POOL_WINDOWS = (2, 4, 8, 16)
POOL_HALO = 16
CONV_HALO = 8
N_BRANCH = 4
D_FF = 4 * D_MODEL
N_BUCKETS = 32
MAX_DISTANCE = 2048
EPS = 1e-6
NEG_BIG = -1e30

COL_Q = 5 * W_BRANCH
COL_K = COL_Q + 3 * W_BRANCH
COL_V = COL_K + 3 * W_BRANCH
COL_D = COL_V + 3 * W_BRANCH
COL_GATE = COL_D + W_BRANCH
W_QKV_OUT = 4 * W_BRANCH

MIX_TILE = 512
TOK_TILE = 1024
SUB_TILE = 512
ATTN_BLOCKS_PER_PIECE = 8
VMEM_LIMIT = 56 * 1024 * 1024


def _rmsnorm(x, g):
    return x * lax.rsqrt(jnp.mean(x * x, axis=-1, keepdims=True) + EPS) * g


def _gelu_tanh(x):
    c = math.sqrt(2.0 / math.pi)
    return x * (0.5 * (1.0 + jnp.tanh(c * (x + 0.044715 * (x * x * x)))))


def _dot(a, b):
    return jnp.dot(a, b, preferred_element_type=F32)


def _const_spec(shape):
    n = len(shape)
    return pl.BlockSpec(shape, lambda *_: (0,) * n, pipeline_mode=pl.Buffered(1))


def _layer_spec(block_shape, layer):
    n = len(block_shape)
    return pl.BlockSpec(block_shape, lambda *_: (layer,) + (0,) * (n - 1), pipeline_mode=pl.Buffered(1))


def _mix_kernel(x_ref, g_ref, w_ref, lng_ref, lnb_ref, ws_ref, bs_ref, conv_ref, wpool_ref,
                dscale_ref, pa_ref, pb_ref, pd_ref, q0_ref, q1_ref, q2_ref, h_scr, bbuf, dbuf):
    ts = MIX_TILE
    i = pl.program_id(1)
    q_refs = (q0_ref, q1_ref, q2_ref)
    x = x_ref[0]
    h = _rmsnorm(x, g_ref[...])
    for c in range(D_MODEL // 128):
        h_scr[c] = h[:, c * 128:(c + 1) * 128]
    hb = h.astype(BF16)
    zl = _dot(hb, w_ref[0, :, 0:COL_Q])

    u = _gelu_tanh(zl[:, 0:256])
    v = _gelu_tanh(zl[:, 256:512])
    mu = jnp.mean(v, axis=-1, keepdims=True)
    vc = v - mu
    var = jnp.mean(vc * vc, axis=-1, keepdims=True)
    v = vc * lax.rsqrt(var + EPS) * lng_ref[...] + lnb_ref[...]
    row = lax.broadcasted_iota(jnp.int32, (CHUNK, A_GROUPS * CHUNK), 0)
    col = lax.broadcasted_iota(jnp.int32, (CHUNK, A_GROUPS * CHUNK), 1) % CHUNK
    wtril = jnp.where(row >= col, ws_ref[...], 0.0).astype(BF16)
    grp = lax.broadcasted_iota(jnp.int32, (CHUNK, W_BRANCH), 1) // (W_BRANCH // A_GROUPS)
    svs = []
    for c in range(ts // CHUNK):
        vch = v[c * CHUNK:(c + 1) * CHUNK]
        stacked = jnp.concatenate(
            [jnp.where(grp == g, vch, 0.0) for g in range(A_GROUPS)], axis=0).astype(BF16)
        svs.append(_dot(wtril, stacked) + bs_ref[...])
    sv = jnp.concatenate(svs, axis=0)
    pa_ref[0] = (u * sv).astype(BF16)

    prod = zl[:, 768:1024] * zl[:, 1024:1280]

    @pl.when(i == 0)
    def _():
        bbuf[0:CONV_HALO] = jnp.zeros((CONV_HALO, W_BRANCH), F32)
        dbuf[0:POOL_HALO] = jnp.zeros((POOL_HALO, W_BRANCH), F32)

    @pl.when(i > 0)
    def _():
        bbuf[0:CONV_HALO] = bbuf[ts:ts + CONV_HALO]
        dbuf[0:POOL_HALO] = dbuf[ts:ts + POOL_HALO]

    bbuf[CONV_HALO:ts + CONV_HALO] = prod
    cw = conv_ref[...]
    conv = (cw[0:1] * bbuf[pl.ds(CONV_HALO - 2, ts), :]
            + cw[1:2] * bbuf[pl.ds(CONV_HALO - 1, ts), :]
            + cw[2:3] * prod)
    pb_ref[0] = (zl[:, 512:768] * conv).astype(BF16)

    dz = _dot(hb, w_ref[0, :, COL_D:COL_GATE])
    dbuf[POOL_HALO:ts + POOL_HALO] = dz
    lane = lax.broadcasted_iota(jnp.int32, (ts, 128), 1)
    first_half = lane < 64
    tpos = (lax.broadcasted_iota(jnp.int32, (ts, 128), 0) + (i * ts + 1)).astype(F32)

    def window_sums(e, levels):
        out = []
        s = e
        for k in range(levels):
            s = s + pltpu.roll(s, 1 << k, 0)
            out.append(s[POOL_HALO:])
        return out

    lo = window_sums(dbuf[:, 0:128], 2)
    hi = window_sums(dbuf[:, 128:256], 4)
    pooled_lo = jnp.where(first_half, lo[0], lo[1]) / jnp.minimum(tpos, jnp.where(first_half, 2.0, 4.0))
    pooled_hi = jnp.where(first_half, hi[2], hi[3]) / jnp.minimum(tpos, jnp.where(first_half, 8.0, 16.0))
    y = jnp.concatenate([pooled_lo, pooled_hi], axis=1) - dz
    pd_ref[0] = (_dot(y.astype(BF16), wpool_ref[...]) * dscale_ref[...]).astype(BF16)

    sub0 = lax.broadcasted_iota(jnp.int32, (ts, 128), 1) < HEAD_DIM
    for g, (q_ref, dil) in enumerate(zip(q_refs, DILATIONS)):
        n = ts // dil
        if dil == 1:
            hp = hb
        else:
            hp = jnp.concatenate(
                [jnp.concatenate([h_scr[c, pl.ds(r, n, stride=dil), :] for r in range(dil)], axis=0)
                 for c in range(D_MODEL // 128)], axis=1).astype(BF16)
        c0 = g * W_BRANCH
        q, k, v = (_dot(hp, w_ref[0, :, col + c0:col + c0 + W_BRANCH]) for col in (COL_Q, COL_K, COL_V))
        q = q * (HEAD_DIM ** -0.5)
        pieces = []
        for pr in range(2):
            qp = q[:, pr * 128:(pr + 1) * 128]
            pieces.append(jnp.where(sub0, qp, 0.0))
            pieces.append(jnp.where(sub0, 0.0, qp))
            pieces.append(k[:, pr * 128:(pr + 1) * 128])
            pieces.append(v[:, pr * 128:(pr + 1) * 128])
        val = jnp.concatenate(pieces, axis=1).astype(BF16)
        for r in range(dil):
            q_ref[0, r] = val[r * n:(r + 1) * n]


def _mix_call(layer, x, g, w_in, lng, lnb, ws, bs_full, conv, wpool_bd, dscale):
    B, S, D = x.shape
    ts = MIX_TILE
    nt = S // ts
    tok = lambda b, i: (b, i, 0)
    out_shape = (
        jax.ShapeDtypeStruct((B, S, W_BRANCH), BF16),
        jax.ShapeDtypeStruct((B, S, W_BRANCH), BF16),
        jax.ShapeDtypeStruct((B, S, W_BRANCH), BF16),
    ) + tuple(jax.ShapeDtypeStruct((B, dil, S // dil, W_QKV_OUT), BF16) for dil in DILATIONS)
    in_specs = [
        pl.BlockSpec((1, ts, D), tok),
        _const_spec((1, D)),
        _layer_spec((1, D, COL_GATE), layer),
        _const_spec((1, W_BRANCH)),
        _const_spec((1, W_BRANCH)),
        _const_spec((CHUNK, A_GROUPS * CHUNK)),
        _const_spec((CHUNK, W_BRANCH)),
        _const_spec((CONV_WIDTH, W_BRANCH)),
        _const_spec((W_BRANCH, W_BRANCH)),
        _const_spec((1, W_BRANCH)),
    ]
    out_specs = (
        pl.BlockSpec((1, ts, W_BRANCH), tok),
        pl.BlockSpec((1, ts, W_BRANCH), tok),
        pl.BlockSpec((1, ts, W_BRANCH), tok),
    ) + tuple(pl.BlockSpec((1, dil, ts // dil, W_QKV_OUT), lambda b, i: (b, 0, i, 0))
              for dil in DILATIONS)
    return pl.pallas_call(
        _mix_kernel,
        grid=(B, nt),
        in_specs=in_specs,
        out_specs=out_specs,
        out_shape=out_shape,
        scratch_shapes=[
            pltpu.VMEM((D // 128, ts, 128), F32),
            pltpu.VMEM((ts + CONV_HALO, W_BRANCH), F32),
            pltpu.VMEM((ts + POOL_HALO, W_BRANCH), F32),
        ],
        compiler_params=pltpu.CompilerParams(
            dimension_semantics=("arbitrary", "arbitrary"), vmem_limit_bytes=VMEM_LIMIT),
        name="mix",
    )(x, g, w_in, lng, lnb, ws, bs_full, conv, wpool_bd, dscale)


def _attn_kernel(q0_ref, q1_ref, q2_ref, bcur_ref, bprev_ref, out_ref, o_scr, l_scr):
    q_refs = (q0_ref, q1_ref, q2_ref)
    S = out_ref.shape[1]
    pr = pl.program_id(1)
    nb = S // CHUNK

    def cols(g, c0):
        return q_refs[g][0, :, c0:c0 + 128].reshape(nb, CHUNK, 128)

    jb = ATTN_BLOCKS_PER_PIECE
    sub0 = lax.broadcasted_iota(jnp.int32, (jb, CHUNK, 128), 2) < HEAD_DIM
    qk = lambda a, b: jnp.einsum("jqd,jkd->jqk", a, b, preferred_element_type=F32)
    pv = lambda a, b: jnp.einsum("jqk,jkd->jqd", a, b, preferred_element_type=F32)
    hh0 = 2 * pr

    for g, dil in enumerate(DILATIONS):
        blocks_per_seq = nb // dil
        has_prev = blocks_per_seq > 1
        q = jnp.concatenate([cols(g, 0), cols(g, 128)], axis=1)
        k = cols(g, 256)
        v = cols(g, 384)
        bias_cur = jnp.concatenate([bcur_ref[g, hh0], bcur_ref[g, hh0 + 1]], axis=0)[None]
        if has_prev:
            k = jnp.concatenate([jnp.concatenate([k[:1], k[:-1]], axis=0), k], axis=1)
            v = jnp.concatenate([jnp.concatenate([v[:1], v[:-1]], axis=0), v], axis=1)
            bias_prev = jnp.concatenate([bprev_ref[g, hh0], bprev_ref[g, hh0 + 1]], axis=0)[None]
        for j0 in range(0, nb, jb):
            s = qk(q[j0:j0 + jb], k[j0:j0 + jb])
            s_cur = s[:, :, -CHUNK:] + bias_cur
            top = s_cur
            if has_prev:
                j = lax.broadcasted_iota(jnp.int32, (jb, 2 * CHUNK, CHUNK), 0) + j0
                s_prev = jnp.where(j % blocks_per_seq != 0, s[:, :, :CHUNK] + bias_prev, NEG_BIG)
                top = jnp.maximum(s_cur, s_prev)
            row_max = jnp.max(top, axis=-1, keepdims=True)
            e = jnp.exp(s_cur - row_max)
            den = e
            if has_prev:
                e_prev = jnp.exp(s_prev - row_max)
                den = e + e_prev
                e = jnp.concatenate([e_prev, e], axis=-1)
            den = jnp.sum(den, axis=-1, keepdims=True)
            acc = pv(e.astype(BF16), v[j0:j0 + jb])
            o2 = acc / den
            l2 = jnp.broadcast_to(row_max + jnp.log(den), o2.shape)
            o = jnp.where(sub0, o2[:, :CHUNK], o2[:, CHUNK:])
            lse = jnp.where(sub0, l2[:, :CHUNK], l2[:, CHUNK:])
            for jj in range(jb):
                r, blk = divmod(j0 + jj, blocks_per_seq)
                rows = (pl.ds(blk * CHUNK * dil + r, CHUNK, stride=dil) if dil > 1
                        else pl.ds((j0 + jj) * CHUNK, CHUNK))
                o_scr[g, rows, :] = o[jj]
                l_scr[g, rows, :] = lse[jj]

    rows = 256
    for c in range(S // rows):
        sl = pl.ds(c * rows, rows)
        l0, l1, l2 = l_scr[0, sl, :], l_scr[1, sl, :], l_scr[2, sl, :]
        m = jnp.maximum(jnp.maximum(l0, l1), l2)
        e0, e1, e2 = jnp.exp(l0 - m), jnp.exp(l1 - m), jnp.exp(l2 - m)
        num = e0 * o_scr[0, sl, :] + e1 * o_scr[1, sl, :] + e2 * o_scr[2, sl, :]
        out_ref[0, sl, :] = (num / (e0 + e1 + e2)).astype(BF16)


def _attn_call(qs, bcur, bprev):
    B = qs[0].shape[0]
    S = qs[0].shape[1] * qs[0].shape[2]
    operands = [q.reshape(B, S, W_QKV_OUT) for q in qs]
    spec = pl.BlockSpec((1, S, W_QKV_OUT // 2), lambda b, p: (b, 0, p))
    return pl.pallas_call(
        _attn_kernel,
        grid=(B, 2),
        in_specs=[spec, spec, spec, _const_spec(bcur.shape), _const_spec(bprev.shape)],
        out_specs=pl.BlockSpec((1, S, 128), lambda b, p: (b, 0, p)),
        out_shape=jax.ShapeDtypeStruct((B, S, W_BRANCH), BF16),
        scratch_shapes=[
            pltpu.VMEM((3, S, 128), F32),
            pltpu.VMEM((3, S, 128), F32),
        ],
        compiler_params=pltpu.CompilerParams(
            dimension_semantics=("arbitrary", "arbitrary"), vmem_limit_bytes=VMEM_LIMIT),
        name="attn",
    )(*operands, bcur, bprev)


def _merge_kernel(x_ref, pa_ref, pb_ref, pc_ref, pd_ref, g_ref, wg_ref, wout_ref, wo_ref, o_ref):
    for s in range(TOK_TILE // SUB_TILE):
        rows = pl.ds(s * SUB_TILE, SUB_TILE)
        x = x_ref[rows, :]
        hb = _rmsnorm(x, g_ref[...]).astype(BF16)
        merged = None
        for br, p_ref in enumerate((pa_ref, pb_ref, pc_ref, pd_ref)):
            gate = jax.nn.sigmoid(_dot(hb, wg_ref[:, br * D_MODEL:(br + 1) * D_MODEL]))
            term = gate * _dot(p_ref[rows, :], wout_ref[0, br])
            merged = term if merged is None else merged + term
        o_ref[rows, :] = x + _dot(merged.astype(BF16), wo_ref[0])


def _merge_call(layer, x, pa, pb, pc, pd, g, w_gate, w_out, w_o):
    T, D = x.shape
    tt = TOK_TILE
    tok = lambda i: (i, 0)
    pspec = pl.BlockSpec((tt, W_BRANCH), tok)
    return pl.pallas_call(
        _merge_kernel,
        grid=(T // tt,),
        in_specs=[pl.BlockSpec((tt, D), tok), pspec, pspec, pspec, pspec,
                  _const_spec((1, D)), _const_spec(w_gate.shape),
                  _layer_spec((1,) + w_out.shape[1:], layer), _layer_spec((1, D, D), layer)],
        out_specs=pl.BlockSpec((tt, D), tok),
        out_shape=jax.ShapeDtypeStruct((T, D), F32),
        compiler_params=pltpu.CompilerParams(
            dimension_semantics=("arbitrary",), vmem_limit_bytes=VMEM_LIMIT),
        name="merge",
    )(x, pa, pb, pc, pd, g, w_gate, w_out, w_o)


def _ffn_kernel(x_ref, g_ref, w1_ref, w2_ref, fg_ref, o_ref, *, final_norm):
    for s in range(TOK_TILE // SUB_TILE):
        rows = pl.ds(s * SUB_TILE, SUB_TILE)
        x = x_ref[rows, :]
        hb = _rmsnorm(x, g_ref[...]).astype(BF16)
        acts = []
        for c in range(D_FF // D_MODEL):
            a = _dot(hb, w1_ref[0, :, c * D_MODEL:(c + 1) * D_MODEL])
            acts.append(jnp.square(jnp.maximum(a, 0.0)).astype(BF16))
        y = x + _dot(jnp.concatenate(acts, axis=1), w2_ref[0])
        if final_norm:
            y = _rmsnorm(y, fg_ref[...])
        o_ref[rows, :] = y


def _ffn_call(layer, x, g, w1, w2, final_g, final_norm):
    T, D = x.shape
    tt = TOK_TILE
    tok = lambda i: (i, 0)
    return pl.pallas_call(
        functools.partial(_ffn_kernel, final_norm=final_norm),
        grid=(T // tt,),
        in_specs=[pl.BlockSpec((tt, D), tok), _const_spec((1, D)), _layer_spec((1, D, D_FF), layer),
                  _layer_spec((1, D_FF, D), layer), _const_spec((1, D))],
        out_specs=pl.BlockSpec((tt, D), tok),
        out_shape=jax.ShapeDtypeStruct((T, D), F32),
        compiler_params=pltpu.CompilerParams(
            dimension_semantics=("arbitrary",), vmem_limit_bytes=VMEM_LIMIT),
        name="ffn",
    )(x, g, w1, w2, final_g)


def _t5_bucket_np(dist):
    max_exact = N_BUCKETS // 2
    d_f = np.maximum(dist, 1).astype(np.float32)
    ratio = np.log(d_f / np.float32(max_exact)) / np.float32(math.log(MAX_DISTANCE / max_exact))
    large = max_exact + (ratio * np.float32(N_BUCKETS - max_exact)).astype(np.int32)
    large = np.minimum(large, N_BUCKETS - 1)
    return np.where(dist < max_exact, dist, large)


def _bucket_indices():
    qi = np.arange(CHUNK)[:, None]
    ki = np.arange(CHUNK)[None, :]
    d_cur = qi - ki
    d_prev = qi + CHUNK - ki
    cur = [np.where(d_cur >= 0, _t5_bucket_np(np.clip(d_cur, 0, None) * dil), -1) for dil in DILATIONS]
    prev = [np.where(d_prev <= CHUNK, _t5_bucket_np(d_prev * dil), -1) for dil in DILATIONS[:2]]
    return np.stack(cur).astype(np.int32), np.stack(prev).astype(np.int32)


def _bias_kernel(tab_ref, icur_ref, iprev_ref, bcur_ref, bprev_ref):
    for idx_ref, out_ref in ((icur_ref, bcur_ref), (iprev_ref, bprev_ref)):
        for g in range(idx_ref.shape[0]):
            idx = idx_ref[g]
            accs = [jnp.full((CHUNK, CHUNK), NEG_BIG, F32) for _ in range(HEADS_PER_GROUP)]
            for b in range(N_BUCKETS):
                hit = idx == b
                for h in range(HEADS_PER_GROUP):
                    accs[h] = jnp.where(hit, tab_ref[b, g * HEADS_PER_GROUP + h], accs[h])
            for h in range(HEADS_PER_GROUP):
                out_ref[g, h] = accs[h]


def _bias_tables(rel_bias):
    icur, iprev = _bucket_indices()
    vm = pl.BlockSpec(memory_space=pltpu.VMEM)
    return pl.pallas_call(
        _bias_kernel,
        in_specs=[pl.BlockSpec(memory_space=pltpu.SMEM), vm, vm],
        out_specs=(vm, vm),
        out_shape=(jax.ShapeDtypeStruct((3, HEADS_PER_GROUP, CHUNK, CHUNK), F32),
                   jax.ShapeDtypeStruct((2, HEADS_PER_GROUP, CHUNK, CHUNK), F32)),
        name="bias",
    )(rel_bias, jnp.asarray(icur), jnp.asarray(iprev))


def kernel(x, norm_mix_g, w_in, a_ln_g, a_ln_b, a_ws, a_bs, w_a_out, b_conv, w_b_out, rel_bias,
           w_c_out, d_w, d_scale, w_d_out, w_o, norm_ff_g, w_ff1, w_ff2, final_g):
    B, S, D = x.shape
    depth = w_in.shape[0]
    bcur, bprev = _bias_tables(rel_bias)
    w_in_b = w_in.astype(BF16)
    w_out_b = jnp.stack([w_a_out, w_b_out, w_c_out, w_d_out], axis=1).astype(BF16)
    w_o_b = w_o.astype(BF16)
    w_ff1_b = w_ff1.astype(BF16)
    w_ff2_b = w_ff2.astype(BF16)
    T = B * S
    for l in range(depth):
        w_gate = w_in_b[l, :, COL_GATE:]
        ws = jnp.transpose(a_ws[l], (1, 0, 2)).reshape(CHUNK, A_GROUPS * CHUNK)
        bs_full = jnp.repeat(a_bs[l].T, W_BRANCH // A_GROUPS, axis=1)
        wpool_bd = jax.scipy.linalg.block_diag(*[d_w[l, g] for g in range(4)]).astype(BF16)

        pa, pb, pd, *qs = _mix_call(
            l, x, norm_mix_g[l][None], w_in_b, a_ln_g[l][None], a_ln_b[l][None], ws, bs_full,
            b_conv[l], wpool_bd, d_scale[l][None])
        pc = _attn_call(qs, bcur, bprev)
        x2 = _merge_call(l, x.reshape(T, D), pa.reshape(T, -1), pb.reshape(T, -1), pc.reshape(T, -1),
                         pd.reshape(T, -1), norm_mix_g[l][None], w_gate, w_out_b, w_o_b)
        x2 = _ffn_call(l, x2, norm_ff_g[l][None], w_ff1_b, w_ff2_b, final_g[None],
                       final_norm=(l == depth - 1))
        x = x2.reshape(B, S, D)
    return x
```

```python
import functools
import math

import jax
import jax.numpy as jnp
import numpy as np
from jax import lax
from jax.experimental import pallas as pl
from jax.experimental.pallas import tpu as pltpu

F32 = jnp.float32
BF16 = jnp.bfloat16

D_MODEL = 1024
W_BRANCH = 256
A_GROUPS = 4
CHUNK = 128
CONV_WIDTH = 3
DILATIONS = (1, 4, 16)
WINDOWS = (128, 512, 2048)
HEADS_PER_GROUP = 4
HEAD_DIM = 64
POOL_WINDOWS = (2, 4, 8, 16)
POOL_HALO = 16
CONV_HALO = 8
N_BRANCH = 4
D_FF = 4 * D_MODEL
N_BUCKETS = 32
MAX_DISTANCE = 2048
EPS = 1e-6
NEG_BIG = -1e30

COL_Q = 5 * W_BRANCH
COL_K = COL_Q + 3 * W_BRANCH
COL_V = COL_K + 3 * W_BRANCH
COL_D = COL_V + 3 * W_BRANCH
COL_GATE = COL_D + W_BRANCH
W_LOCAL = 6 * W_BRANCH
W_QKV = 3 * W_BRANCH
W_QKV_OUT = 4 * W_BRANCH

MIX_TILE = 512
TOK_TILE = 1024
SUB_TILE = 512
ATTN_BLOCKS_PER_PIECE = 8
VMEM_LIMIT = 56 * 1024 * 1024


def _rmsnorm(x, g):
    return x * lax.rsqrt(jnp.mean(x * x, axis=-1, keepdims=True) + EPS) * g


def _gelu_tanh(x):
    c = math.sqrt(2.0 / math.pi)
    return x * (0.5 * (1.0 + jnp.tanh(c * (x + 0.044715 * (x * x * x)))))


def _dot(a, b):
    return jnp.dot(a, b, preferred_element_type=F32)


def _const_spec(shape):
    n = len(shape)
    return pl.BlockSpec(shape, lambda *_: (0,) * n, pipeline_mode=pl.Buffered(1))


def _layer_spec(block_shape, layer):
    n = len(block_shape)
    return pl.BlockSpec(block_shape, lambda *_: (layer,) + (0,) * (n - 1), pipeline_mode=pl.Buffered(1))


def _mix_kernel(x_ref, g_ref, w_ref, lng_ref, lnb_ref, ws_ref, bs_ref, conv_ref, wpool_ref,
                dscale_ref, pa_ref, pb_ref, pd_ref, q0_ref, q1_ref, q2_ref, h_scr, bbuf, dbuf):
    ts = MIX_TILE
    i = pl.program_id(1)
    q_refs = (q0_ref, q1_ref, q2_ref)

    @pl.when((pl.program_id(0) == 0) & (i == 0))
    def _():
        bbuf[...] = jnp.zeros(bbuf.shape, F32)
        dbuf[...] = jnp.zeros(dbuf.shape, F32)

    x = x_ref[0]
    h = _rmsnorm(x, g_ref[...])
    for c in range(D_MODEL // 128):
        h_scr[c] = h[:, c * 128:(c + 1) * 128]
    hb = h.astype(BF16)
    zl = _dot(hb, w_ref[0, :, 0:W_LOCAL + W_QKV])

    u = _gelu_tanh(zl[:, 0:256])
    v = _gelu_tanh(zl[:, 256:512])
    mu = jnp.mean(v, axis=-1, keepdims=True)
    vc = v - mu
    var = jnp.mean(vc * vc, axis=-1, keepdims=True)
    v = vc * lax.rsqrt(var + EPS) * lng_ref[...] + lnb_ref[...]
    row = lax.broadcasted_iota(jnp.int32, (CHUNK, A_GROUPS * CHUNK), 0)
    col = lax.broadcasted_iota(jnp.int32, (CHUNK, A_GROUPS * CHUNK), 1) % CHUNK
    wtril = jnp.where(row >= col, ws_ref[...], 0.0).astype(BF16)
    grp = lax.broadcasted_iota(jnp.int32, (CHUNK, W_BRANCH), 1) // (W_BRANCH // A_GROUPS)
    svs = []
    for c in range(ts // CHUNK):
        vch = v[c * CHUNK:(c + 1) * CHUNK]
        stacked = jnp.concatenate(
            [jnp.where(grp == g, vch, 0.0) for g in range(A_GROUPS)], axis=0).astype(BF16)
        svs.append(_dot(wtril, stacked) + bs_ref[...])
    sv = jnp.concatenate(svs, axis=0)
    pa_ref[0] = (u * sv).astype(BF16)

    prod = zl[:, 768:1024] * zl[:, 1024:1280]

    bbuf[0:CONV_HALO] = jnp.where(i > 0, bbuf[ts:ts + CONV_HALO], 0.0)
    dbuf[0:POOL_HALO] = jnp.where(i > 0, dbuf[ts:ts + POOL_HALO], 0.0)
    bbuf[CONV_HALO:ts + CONV_HALO] = prod
    cw = conv_ref[...]
    conv = (cw[0:1] * bbuf[pl.ds(CONV_HALO - 2, ts), :]
            + cw[1:2] * bbuf[pl.ds(CONV_HALO - 1, ts), :]
            + cw[2:3] * prod)
    pb_ref[0] = (zl[:, 512:768] * conv).astype(BF16)

    dz = zl[:, 1280:1536]
    dbuf[POOL_HALO:ts + POOL_HALO] = dz
    lane = lax.broadcasted_iota(jnp.int32, (ts, 128), 1)
    first_half = lane < 64
    tpos = (lax.broadcasted_iota(jnp.int32, (ts, 128), 0) + (i * ts + 1)).astype(F32)

    def window_sums(e, levels):
        out = []
        s = e
        for k in range(levels):
            s = s + pltpu.roll(s, 1 << k, 0)
            out.append(s[POOL_HALO:])
        return out

    lo = window_sums(dbuf[:, 0:128], 2)
    hi = window_sums(dbuf[:, 128:256], 4)
    pooled_lo = jnp.where(first_half, lo[0], lo[1]) / jnp.minimum(tpos, jnp.where(first_half, 2.0, 4.0))
    pooled_hi = jnp.where(first_half, hi[2], hi[3]) / jnp.minimum(tpos, jnp.where(first_half, 8.0, 16.0))
    y = jnp.concatenate([pooled_lo, pooled_hi], axis=1) - dz
    pd_ref[0] = (_dot(y.astype(BF16), wpool_ref[...]) * dscale_ref[...]).astype(BF16)

    sub0 = lax.broadcasted_iota(jnp.int32, (ts, 128), 1) < HEAD_DIM
    for g, (q_ref, dil) in enumerate(zip(q_refs, DILATIONS)):
        n = ts // dil
        if dil == 1:
            z = zl[:, W_LOCAL:W_LOCAL + W_QKV]
        else:
            hp = jnp.concatenate(
                [jnp.concatenate([h_scr[c, pl.ds(r, n, stride=dil), :] for r in range(dil)], axis=0)
                 for c in range(D_MODEL // 128)], axis=1).astype(BF16)
            c0 = W_LOCAL + g * W_QKV
            z = _dot(hp, w_ref[0, :, c0:c0 + W_QKV])
        q = z[:, 0:256] * (HEAD_DIM ** -0.5)
        pieces = []
        for pr in range(2):
            qp = q[:, pr * 128:(pr + 1) * 128]
            pieces.append(jnp.where(sub0, qp, 0.0))
            pieces.append(jnp.where(sub0, 0.0, qp))
            pieces.append(z[:, 256 + pr * 128:256 + (pr + 1) * 128])
            pieces.append(z[:, 512 + pr * 128:512 + (pr + 1) * 128])
        val = jnp.concatenate(pieces, axis=1).astype(BF16)
        for r in range(dil):
            q_ref[0, r] = val[r * n:(r + 1) * n]


def _mix_call(layer, x, g, w_mix, lng, lnb, ws, bs_full, conv, wpool_bd, dscale):
    B, S, D = x.shape
    ts = MIX_TILE
    nt = S // ts
    tok = lambda b, i: (b, i, 0)
    out_shape = (
        jax.ShapeDtypeStruct((B, S, W_BRANCH), BF16),
        jax.ShapeDtypeStruct((B, S, W_BRANCH), BF16),
        jax.ShapeDtypeStruct((B, S, W_BRANCH), BF16),
    ) + tuple(jax.ShapeDtypeStruct((B, dil, S // dil, W_QKV_OUT), BF16) for dil in DILATIONS)
    in_specs = [
        pl.BlockSpec((1, ts, D), tok),
        _const_spec((1, D)),
        _layer_spec((1,) + w_mix.shape[1:], layer),
        _const_spec((1, W_BRANCH)),
        _const_spec((1, W_BRANCH)),
        _const_spec((CHUNK, A_GROUPS * CHUNK)),
        _const_spec((CHUNK, W_BRANCH)),
        _const_spec((CONV_WIDTH, W_BRANCH)),
        _const_spec((W_BRANCH, W_BRANCH)),
        _const_spec((1, W_BRANCH)),
    ]
    out_specs = (
        pl.BlockSpec((1, ts, W_BRANCH), tok),
        pl.BlockSpec((1, ts, W_BRANCH), tok),
        pl.BlockSpec((1, ts, W_BRANCH), tok),
    ) + tuple(pl.BlockSpec((1, dil, ts // dil, W_QKV_OUT), lambda b, i: (b, 0, i, 0))
              for dil in DILATIONS)
    return pl.pallas_call(
        _mix_kernel,
        grid=(B, nt),
        in_specs=in_specs,
        out_specs=out_specs,
        out_shape=out_shape,
        scratch_shapes=[
            pltpu.VMEM((D // 128, ts, 128), F32),
            pltpu.VMEM((ts + CONV_HALO, W_BRANCH), F32),
            pltpu.VMEM((ts + POOL_HALO, W_BRANCH), F32),
        ],
        compiler_params=pltpu.CompilerParams(
            dimension_semantics=("arbitrary", "arbitrary"), vmem_limit_bytes=VMEM_LIMIT),
        name="mix",
    )(x, g, w_mix, lng, lnb, ws, bs_full, conv, wpool_bd, dscale)


def _attn_kernel(q0_ref, q1_ref, q2_ref, bcur_ref, bprev_ref, out_ref, o_scr, l_scr):
    q_refs = (q0_ref, q1_ref, q2_ref)
    S = out_ref.shape[1]
    pr = pl.program_id(1)
    nb = S // CHUNK

    def cols(g, c0):
        return q_refs[g][0, :, c0:c0 + 128].reshape(nb, CHUNK, 128)

    jb = ATTN_BLOCKS_PER_PIECE
    sub0 = lax.broadcasted_iota(jnp.int32, (jb, CHUNK, 128), 2) < HEAD_DIM
    qk = lambda a, b: jnp.einsum("jqd,jkd->jqk", a, b, preferred_element_type=F32)
    pv = lambda a, b: jnp.einsum("jqk,jkd->jqd", a, b, preferred_element_type=F32)
    hh0 = 2 * pr

    for g, dil in enumerate(DILATIONS):
        blocks_per_seq = nb // dil
        has_prev = blocks_per_seq > 1
        q = jnp.concatenate([cols(g, 0), cols(g, 128)], axis=1)
        k = cols(g, 256)
        v = cols(g, 384)
        bias_cur = jnp.concatenate([bcur_ref[g, hh0], bcur_ref[g, hh0 + 1]], axis=0)[None]
        if has_prev:
            k = jnp.concatenate([jnp.concatenate([k[:1], k[:-1]], axis=0), k], axis=1)
            v = jnp.concatenate([jnp.concatenate([v[:1], v[:-1]], axis=0), v], axis=1)
            bias_prev = jnp.concatenate([bprev_ref[g, hh0], bprev_ref[g, hh0 + 1]], axis=0)[None]
        for j0 in range(0, nb, jb):
            s = qk(q[j0:j0 + jb], k[j0:j0 + jb])
            s_cur = s[:, :, -CHUNK:] + bias_cur
            top = s_cur
            if has_prev:
                j = lax.broadcasted_iota(jnp.int32, (jb, 2 * CHUNK, CHUNK), 0) + j0
                s_prev = jnp.where(j % blocks_per_seq != 0, s[:, :, :CHUNK] + bias_prev, NEG_BIG)
                top = jnp.maximum(s_cur, s_prev)
            row_max = jnp.max(top, axis=-1, keepdims=True)
            e = jnp.exp(s_cur - row_max)
            den = e
            if has_prev:
                e_prev = jnp.exp(s_prev - row_max)
                den = e + e_prev
                e = jnp.concatenate([e_prev, e], axis=-1)
            den = jnp.sum(den, axis=-1, keepdims=True)
            acc = pv(e.astype(BF16), v[j0:j0 + jb])
            o2 = acc / den
            l2 = jnp.broadcast_to(row_max + jnp.log(den), o2.shape)
            o = jnp.where(sub0, o2[:, :CHUNK], o2[:, CHUNK:])
            lse = jnp.where(sub0, l2[:, :CHUNK], l2[:, CHUNK:])
            for jj in range(jb):
                r, blk = divmod(j0 + jj, blocks_per_seq)
                rows = (pl.ds(blk * CHUNK * dil + r, CHUNK, stride=dil) if dil > 1
                        else pl.ds((j0 + jj) * CHUNK, CHUNK))
                o_scr[g, rows, :] = o[jj]
                l_scr[g, rows, :] = lse[jj]

    rows = 256
    for c in range(S // rows):
        sl = pl.ds(c * rows, rows)
        l0, l1, l2 = l_scr[0, sl, :], l_scr[1, sl, :], l_scr[2, sl, :]
        m = jnp.maximum(jnp.maximum(l0, l1), l2)
        e0, e1, e2 = jnp.exp(l0 - m), jnp.exp(l1 - m), jnp.exp(l2 - m)
        num = e0 * o_scr[0, sl, :] + e1 * o_scr[1, sl, :] + e2 * o_scr[2, sl, :]
        out_ref[0, sl, :] = (num / (e0 + e1 + e2)).astype(BF16)


def _attn_call(qs, bcur, bprev):
    B = qs[0].shape[0]
    S = qs[0].shape[1] * qs[0].shape[2]
    operands = [q.reshape(B, S, W_QKV_OUT) for q in qs]
    spec = pl.BlockSpec((1, S, W_QKV_OUT // 2), lambda b, p: (b, 0, p))
    return pl.pallas_call(
        _attn_kernel,
        grid=(B, 2),
        in_specs=[spec, spec, spec, _const_spec(bcur.shape), _const_spec(bprev.shape)],
        out_specs=pl.BlockSpec((1, S, 128), lambda b, p: (b, 0, p)),
        out_shape=jax.ShapeDtypeStruct((B, S, W_BRANCH), BF16),
        scratch_shapes=[
            pltpu.VMEM((3, S, 128), F32),
            pltpu.VMEM((3, S, 128), F32),
        ],
        compiler_params=pltpu.CompilerParams(
            dimension_semantics=("arbitrary", "arbitrary"), vmem_limit_bytes=VMEM_LIMIT),
        name="attn",
    )(*operands, bcur, bprev)


def _merge_kernel(x_ref, pa_ref, pb_ref, pc_ref, pd_ref, g_ref, wg_ref, wout_ref, wo_ref, o_ref):
    for s in range(TOK_TILE // SUB_TILE):
        rows = pl.ds(s * SUB_TILE, SUB_TILE)
        x = x_ref[rows, :]
        hb = _rmsnorm(x, g_ref[...]).astype(BF16)
        merged = None
        for br, p_ref in enumerate((pa_ref, pb_ref, pc_ref, pd_ref)):
            gate = jax.nn.sigmoid(_dot(hb, wg_ref[:, br * D_MODEL:(br + 1) * D_MODEL]))
            term = gate * _dot(p_ref[rows, :], wout_ref[0, br])
            merged = term if merged is None else merged + term
        o_ref[rows, :] = x + _dot(merged.astype(BF16), wo_ref[0])


def _merge_call(layer, x, pa, pb, pc, pd, g, w_gate, w_out, w_o):
    T, D = x.shape
    tt = TOK_TILE
    tok = lambda i: (i, 0)
    pspec = pl.BlockSpec((tt, W_BRANCH), tok)
    return pl.pallas_call(
        _merge_kernel,
        grid=(T // tt,),
        in_specs=[pl.BlockSpec((tt, D), tok), pspec, pspec, pspec, pspec,
                  _const_spec((1, D)), _const_spec(w_gate.shape),
                  _layer_spec((1,) + w_out.shape[1:], layer), _layer_spec((1, D, D), layer)],
        out_specs=pl.BlockSpec((tt, D), tok),
        out_shape=jax.ShapeDtypeStruct((T, D), F32),
        compiler_params=pltpu.CompilerParams(
            dimension_semantics=("arbitrary",), vmem_limit_bytes=VMEM_LIMIT),
        name="merge",
    )(x, pa, pb, pc, pd, g, w_gate, w_out, w_o)


def _ffn_kernel(x_ref, g_ref, w1_ref, w2_ref, fg_ref, o_ref, *, final_norm):
    for s in range(TOK_TILE // SUB_TILE):
        rows = pl.ds(s * SUB_TILE, SUB_TILE)
        x = x_ref[rows, :]
        hb = _rmsnorm(x, g_ref[...]).astype(BF16)
        acts = []
        for c in range(D_FF // D_MODEL):
            a = _dot(hb, w1_ref[0, :, c * D_MODEL:(c + 1) * D_MODEL])
            acts.append(jnp.square(jnp.maximum(a, 0.0)).astype(BF16))
        y = x + _dot(jnp.concatenate(acts, axis=1), w2_ref[0])
        if final_norm:
            y = _rmsnorm(y, fg_ref[...])
        o_ref[rows, :] = y


def _ffn_call(layer, x, g, w1, w2, final_g, final_norm):
    T, D = x.shape
    tt = TOK_TILE
    tok = lambda i: (i, 0)
    return pl.pallas_call(
        functools.partial(_ffn_kernel, final_norm=final_norm),
        grid=(T // tt,),
        in_specs=[pl.BlockSpec((tt, D), tok), _const_spec((1, D)), _layer_spec((1, D, D_FF), layer),
                  _layer_spec((1, D_FF, D), layer), _const_spec((1, D))],
        out_specs=pl.BlockSpec((tt, D), tok),
        out_shape=jax.ShapeDtypeStruct((T, D), F32),
        compiler_params=pltpu.CompilerParams(
            dimension_semantics=("arbitrary",), vmem_limit_bytes=VMEM_LIMIT),
        name="ffn",
    )(x, g, w1, w2, final_g)


def _t5_bucket_np(dist):
    max_exact = N_BUCKETS // 2
    d_f = np.maximum(dist, 1).astype(np.float32)
    ratio = np.log(d_f / np.float32(max_exact)) / np.float32(math.log(MAX_DISTANCE / max_exact))
    large = max_exact + (ratio * np.float32(N_BUCKETS - max_exact)).astype(np.int32)
    large = np.minimum(large, N_BUCKETS - 1)
    return np.where(dist < max_exact, dist, large)


def _bucket_indices():
    qi = np.arange(CHUNK)[:, None]
    ki = np.arange(CHUNK)[None, :]
    d_cur = qi - ki
    d_prev = qi + CHUNK - ki
    cur = [np.where(d_cur >= 0, _t5_bucket_np(np.clip(d_cur, 0, None) * dil), -1) for dil in DILATIONS]
    prev = [np.where(d_prev <= CHUNK, _t5_bucket_np(d_prev * dil), -1) for dil in DILATIONS[:2]]
    return np.stack(cur).astype(np.int32), np.stack(prev).astype(np.int32)


def _bias_kernel(tab_ref, icur_ref, iprev_ref, bcur_ref, bprev_ref):
    for idx_ref, out_ref in ((icur_ref, bcur_ref), (iprev_ref, bprev_ref)):
        for g in range(idx_ref.shape[0]):
            idx = idx_ref[g]
            accs = [jnp.full((CHUNK, CHUNK), NEG_BIG, F32) for _ in range(HEADS_PER_GROUP)]
            for b in range(N_BUCKETS):
                hit = idx == b
                for h in range(HEADS_PER_GROUP):
                    accs[h] = jnp.where(hit, tab_ref[b, g * HEADS_PER_GROUP + h], accs[h])
            for h in range(HEADS_PER_GROUP):
                out_ref[g, h] = accs[h]


def _bias_tables(rel_bias):
    icur, iprev = _bucket_indices()
    vm = pl.BlockSpec(memory_space=pltpu.VMEM)
    return pl.pallas_call(
        _bias_kernel,
        in_specs=[pl.BlockSpec(memory_space=pltpu.SMEM), vm, vm],
        out_specs=(vm, vm),
        out_shape=(jax.ShapeDtypeStruct((3, HEADS_PER_GROUP, CHUNK, CHUNK), F32),
                   jax.ShapeDtypeStruct((2, HEADS_PER_GROUP, CHUNK, CHUNK), F32)),
        name="bias",
    )(rel_bias, jnp.asarray(icur), jnp.asarray(iprev))


def kernel(x, norm_mix_g, w_in, a_ln_g, a_ln_b, a_ws, a_bs, w_a_out, b_conv, w_b_out, rel_bias,
           w_c_out, d_w, d_scale, w_d_out, w_o, norm_ff_g, w_ff1, w_ff2, final_g):
    B, S, D = x.shape
    depth = w_in.shape[0]
    bcur, bprev = _bias_tables(rel_bias)
    w_in_b = w_in.astype(BF16)
    qkv_cols = [w_in_b[:, :, col + g * W_BRANCH:col + (g + 1) * W_BRANCH]
                for g in range(len(DILATIONS)) for col in (COL_Q, COL_K, COL_V)]
    w_mix_b = jnp.concatenate([w_in_b[:, :, 0:COL_Q], w_in_b[:, :, COL_D:COL_GATE]] + qkv_cols, axis=2)
    w_out_b = jnp.stack([w_a_out, w_b_out, w_c_out, w_d_out], axis=1).astype(BF16)
    w_o_b = w_o.astype(BF16)
    w_ff1_b = w_ff1.astype(BF16)
    w_ff2_b = w_ff2.astype(BF16)
    T = B * S
    for l in range(depth):
        w_gate = w_in_b[l, :, COL_GATE:]
        ws = jnp.transpose(a_ws[l], (1, 0, 2)).reshape(CHUNK, A_GROUPS * CHUNK)
        bs_full = jnp.repeat(a_bs[l].T, W_BRANCH // A_GROUPS, axis=1)
        wpool_bd = jax.scipy.linalg.block_diag(*[d_w[l, g] for g in range(4)]).astype(BF16)

        pa, pb, pd, *qs = _mix_call(
            l, x, norm_mix_g[l][None], w_mix_b, a_ln_g[l][None], a_ln_b[l][None], ws, bs_full,
            b_conv[l], wpool_bd, d_scale[l][None])
        pc = _attn_call(qs, bcur, bprev)
        x2 = _merge_call(l, x.reshape(T, D), pa.reshape(T, -1), pb.reshape(T, -1), pc.reshape(T, -1),
                         pd.reshape(T, -1), norm_mix_g[l][None], w_gate, w_out_b, w_o_b)
        x2 = _ffn_call(l, x2, norm_ff_g[l][None], w_ff1_b, w_ff2_b, final_g[None],
                       final_norm=(l == depth - 1))
        x = x2.reshape(B, S, D)
    return x
```

```python
import functools
import math

import jax
import jax.numpy as jnp
import numpy as np
from jax import lax
from jax.experimental import pallas as pl
from jax.experimental.pallas import tpu as pltpu

F32 = jnp.float32
BF16 = jnp.bfloat16

D_MODEL = 1024
W_BRANCH = 256
A_GROUPS = 4
CHUNK = 128
CONV_WIDTH = 3
DILATIONS = (1, 4, 16)
WINDOWS = (128, 512, 2048)
HEADS_PER_GROUP = 4
HEAD_DIM = 64
POOL_WINDOWS = (2, 4, 8, 16)
POOL_HALO = 16
CONV_HALO = 8
N_BRANCH = 4
D_FF = 4 * D_MODEL
N_BUCKETS = 32
MAX_DISTANCE = 2048
EPS = 1e-6
NEG_BIG = -1e30

COL_Q = 5 * W_BRANCH
COL_K = COL_Q + 3 * W_BRANCH
COL_V = COL_K + 3 * W_BRANCH
COL_D = COL_V + 3 * W_BRANCH
COL_GATE = COL_D + W_BRANCH
W_LOCAL = 6 * W_BRANCH
W_QKV = 3 * W_BRANCH
W_QKV_OUT = 4 * W_BRANCH

MIX_TILE = 1024
MIX_SUB = 512
TOK_TILE = 1024
SUB_TILE = 512
ATTN_BLOCKS_PER_PIECE = 8
VMEM_LIMIT = 56 * 1024 * 1024


def _rmsnorm(x, g):
    return x * lax.rsqrt(jnp.mean(x * x, axis=-1, keepdims=True) + EPS) * g


def _gelu_tanh(x):
    c = math.sqrt(2.0 / math.pi)
    return x * (0.5 * (1.0 + jnp.tanh(c * (x + 0.044715 * (x * x * x)))))


def _dot(a, b):
    return jnp.dot(a, b, preferred_element_type=F32)


def _const_spec(shape):
    n = len(shape)
    return pl.BlockSpec(shape, lambda *_: (0,) * n, pipeline_mode=pl.Buffered(1))


def _layer_spec(block_shape, layer):
    n = len(block_shape)
    return pl.BlockSpec(block_shape, lambda *_: (layer,) + (0,) * (n - 1), pipeline_mode=pl.Buffered(1))


def _mix_kernel(x_ref, g_ref, w_ref, lng_ref, lnb_ref, ws_ref, bs_ref, conv_ref, wpool_ref,
                dscale_ref, pa_ref, pb_ref, pd_ref, q0_ref, q1_ref, q2_ref, h_scr, bbuf, dbuf):
    n_sub = MIX_TILE // MIX_SUB
    first = (pl.program_id(0) == 0) & (pl.program_id(1) == 0)

    @pl.when(first)
    def _():
        bbuf[...] = jnp.zeros(bbuf.shape, F32)
        dbuf[...] = jnp.zeros(dbuf.shape, F32)

    for s in range(n_sub):
        _mix_subtile(s, pl.program_id(1) * n_sub + s, x_ref, g_ref, w_ref, lng_ref, lnb_ref, ws_ref,
                     bs_ref, conv_ref, wpool_ref, dscale_ref, pa_ref, pb_ref, pd_ref,
                     (q0_ref, q1_ref, q2_ref), h_scr.at[s], bbuf.at[s], bbuf.at[(s - 1) % n_sub],
                     dbuf.at[s], dbuf.at[(s - 1) % n_sub])


def _mix_subtile(s, i, x_ref, g_ref, w_ref, lng_ref, lnb_ref, ws_ref, bs_ref, conv_ref, wpool_ref,
                 dscale_ref, pa_ref, pb_ref, pd_ref, q_refs, h_scr, bbuf, bbuf_prev, dbuf, dbuf_prev):
    ts = MIX_SUB
    rows = pl.ds(s * ts, ts)
    x = x_ref[0, rows, :]
    h = _rmsnorm(x, g_ref[...])
    for c in range(D_MODEL // 128):
        h_scr[c] = h[:, c * 128:(c + 1) * 128]
    hb = h.astype(BF16)
    zl = _dot(hb, w_ref[0, :, 0:W_LOCAL + W_QKV])

    u = _gelu_tanh(zl[:, 0:256])
    v = _gelu_tanh(zl[:, 256:512])
    mu = jnp.mean(v, axis=-1, keepdims=True)
    vc = v - mu
    var = jnp.mean(vc * vc, axis=-1, keepdims=True)
    v = vc * lax.rsqrt(var + EPS) * lng_ref[...] + lnb_ref[...]
    row = lax.broadcasted_iota(jnp.int32, (CHUNK, A_GROUPS * CHUNK), 0)
    col = lax.broadcasted_iota(jnp.int32, (CHUNK, A_GROUPS * CHUNK), 1) % CHUNK
    wtril = jnp.where(row >= col, ws_ref[...], 0.0).astype(BF16)
    grp = lax.broadcasted_iota(jnp.int32, (CHUNK, W_BRANCH), 1) // (W_BRANCH // A_GROUPS)
    svs = []
    for c in range(ts // CHUNK):
        vch = v[c * CHUNK:(c + 1) * CHUNK]
        stacked = jnp.concatenate(
            [jnp.where(grp == g, vch, 0.0) for g in range(A_GROUPS)], axis=0).astype(BF16)
        svs.append(_dot(wtril, stacked) + bs_ref[...])
    sv = jnp.concatenate(svs, axis=0)
    pa_ref[0, rows, :] = (u * sv).astype(BF16)

    prod = zl[:, 768:1024] * zl[:, 1024:1280]

    bbuf[0:CONV_HALO] = jnp.where(i > 0, bbuf_prev[ts:ts + CONV_HALO], 0.0)
    dbuf[0:POOL_HALO] = jnp.where(i > 0, dbuf_prev[ts:ts + POOL_HALO], 0.0)
    bbuf[CONV_HALO:ts + CONV_HALO] = prod
    cw = conv_ref[...]
    conv = (cw[0:1] * bbuf[pl.ds(CONV_HALO - 2, ts), :]
            + cw[1:2] * bbuf[pl.ds(CONV_HALO - 1, ts), :]
            + cw[2:3] * prod)
    pb_ref[0, rows, :] = (zl[:, 512:768] * conv).astype(BF16)

    dz = zl[:, 1280:1536]
    dbuf[POOL_HALO:ts + POOL_HALO] = dz
    lane = lax.broadcasted_iota(jnp.int32, (ts, 128), 1)
    first_half = lane < 64
    tpos = (lax.broadcasted_iota(jnp.int32, (ts, 128), 0) + (i * ts + 1)).astype(F32)

    def window_sums(e, levels):
        out = []
        s = e
        for k in range(levels):
            s = s + pltpu.roll(s, 1 << k, 0)
            out.append(s[POOL_HALO:])
        return out

    lo = window_sums(dbuf[:, 0:128], 2)
    hi = window_sums(dbuf[:, 128:256], 4)
    pooled_lo = jnp.where(first_half, lo[0], lo[1]) / jnp.minimum(tpos, jnp.where(first_half, 2.0, 4.0))
    pooled_hi = jnp.where(first_half, hi[2], hi[3]) / jnp.minimum(tpos, jnp.where(first_half, 8.0, 16.0))
    y = jnp.concatenate([pooled_lo, pooled_hi], axis=1) - dz
    pd_ref[0, rows, :] = (_dot(y.astype(BF16), wpool_ref[...]) * dscale_ref[...]).astype(BF16)

    sub0 = lax.broadcasted_iota(jnp.int32, (ts, 128), 1) < HEAD_DIM
    for g, (q_ref, dil) in enumerate(zip(q_refs, DILATIONS)):
        n = ts // dil
        if dil == 1:
            z = zl[:, W_LOCAL:W_LOCAL + W_QKV]
        else:
            hp = jnp.concatenate(
                [jnp.concatenate([h_scr[c, pl.ds(r, n, stride=dil), :] for r in range(dil)], axis=0)
                 for c in range(D_MODEL // 128)], axis=1).astype(BF16)
            c0 = W_LOCAL + g * W_QKV
            z = _dot(hp, w_ref[0, :, c0:c0 + W_QKV])
        q = z[:, 0:256] * (HEAD_DIM ** -0.5)
        pieces = []
        for pr in range(2):
            qp = q[:, pr * 128:(pr + 1) * 128]
            pieces.append(jnp.where(sub0, qp, 0.0))
            pieces.append(jnp.where(sub0, 0.0, qp))
            pieces.append(z[:, 256 + pr * 128:256 + (pr + 1) * 128])
            pieces.append(z[:, 512 + pr * 128:512 + (pr + 1) * 128])
        val = jnp.concatenate(pieces, axis=1).astype(BF16)
        for r in range(dil):
            q_ref[0, r, pl.ds(s * n, n), :] = val[r * n:(r + 1) * n]


def _mix_call(layer, x, g, w_mix, lng, lnb, ws, bs_full, conv, wpool_bd, dscale):
    B, S, D = x.shape
    ts = MIX_TILE
    nt = S // ts
    tok = lambda b, i: (b, i, 0)
    out_shape = (
        jax.ShapeDtypeStruct((B, S, W_BRANCH), BF16),
        jax.ShapeDtypeStruct((B, S, W_BRANCH), BF16),
        jax.ShapeDtypeStruct((B, S, W_BRANCH), BF16),
    ) + tuple(jax.ShapeDtypeStruct((B, dil, S // dil, W_QKV_OUT), BF16) for dil in DILATIONS)
    in_specs = [
        pl.BlockSpec((1, ts, D), tok),
        _const_spec((1, D)),
        _layer_spec((1,) + w_mix.shape[1:], layer),
        _const_spec((1, W_BRANCH)),
        _const_spec((1, W_BRANCH)),
        _const_spec((CHUNK, A_GROUPS * CHUNK)),
        _const_spec((CHUNK, W_BRANCH)),
        _const_spec((CONV_WIDTH, W_BRANCH)),
        _const_spec((W_BRANCH, W_BRANCH)),
        _const_spec((1, W_BRANCH)),
    ]
    out_specs = (
        pl.BlockSpec((1, ts, W_BRANCH), tok),
        pl.BlockSpec((1, ts, W_BRANCH), tok),
        pl.BlockSpec((1, ts, W_BRANCH), tok),
    ) + tuple(pl.BlockSpec((1, dil, ts // dil, W_QKV_OUT), lambda b, i: (b, 0, i, 0))
              for dil in DILATIONS)
    return pl.pallas_call(
        _mix_kernel,
        grid=(B, nt),
        in_specs=in_specs,
        out_specs=out_specs,
        out_shape=out_shape,
        scratch_shapes=[
            pltpu.VMEM((ts // MIX_SUB, D // 128, MIX_SUB, 128), F32),
            pltpu.VMEM((ts // MIX_SUB, MIX_SUB + CONV_HALO, W_BRANCH), F32),
            pltpu.VMEM((ts // MIX_SUB, MIX_SUB + POOL_HALO, W_BRANCH), F32),
        ],
        compiler_params=pltpu.CompilerParams(
            dimension_semantics=("arbitrary", "arbitrary"), vmem_limit_bytes=VMEM_LIMIT),
        name="mix",
    )(x, g, w_mix, lng, lnb, ws, bs_full, conv, wpool_bd, dscale)


def _attn_kernel(q0_ref, q1_ref, q2_ref, bcur_ref, bprev_ref, out_ref, o_scr, l_scr):
    q_refs = (q0_ref, q1_ref, q2_ref)
    S = out_ref.shape[1]
    pr = pl.program_id(1)
    nb = S // CHUNK

    def cols(g, c0):
        return q_refs[g][0, :, c0:c0 + 128].reshape(nb, CHUNK, 128)

    jb = ATTN_BLOCKS_PER_PIECE
    sub0 = lax.broadcasted_iota(jnp.int32, (jb, CHUNK, 128), 2) < HEAD_DIM
    qk = lambda a, b: jnp.einsum("jqd,jkd->jqk", a, b, preferred_element_type=F32)
    pv = lambda a, b: jnp.einsum("jqk,jkd->jqd", a, b, preferred_element_type=F32)
    hh0 = 2 * pr

    for g, dil in enumerate(DILATIONS):
        blocks_per_seq = nb // dil
        has_prev = blocks_per_seq > 1
        q = jnp.concatenate([cols(g, 0), cols(g, 128)], axis=1)
        k = cols(g, 256)
        v = cols(g, 384)
        bias_cur = jnp.concatenate([bcur_ref[g, hh0], bcur_ref[g, hh0 + 1]], axis=0)[None]
        if has_prev:
            k = jnp.concatenate([jnp.concatenate([k[:1], k[:-1]], axis=0), k], axis=1)
            v = jnp.concatenate([jnp.concatenate([v[:1], v[:-1]], axis=0), v], axis=1)
            bias_prev = jnp.concatenate([bprev_ref[g, hh0], bprev_ref[g, hh0 + 1]], axis=0)[None]
        for j0 in range(0, nb, jb):
            s = qk(q[j0:j0 + jb], k[j0:j0 + jb])
            s_cur = s[:, :, -CHUNK:] + bias_cur
            top = s_cur
            if has_prev:
                j = lax.broadcasted_iota(jnp.int32, (jb, 2 * CHUNK, CHUNK), 0) + j0
                s_prev = jnp.where(j % blocks_per_seq != 0, s[:, :, :CHUNK] + bias_prev, NEG_BIG)
                top = jnp.maximum(s_cur, s_prev)
            row_max = jnp.max(top, axis=-1, keepdims=True)
            e = jnp.exp(s_cur - row_max)
            den = e
            if has_prev:
                e_prev = jnp.exp(s_prev - row_max)
                den = e + e_prev
                e = jnp.concatenate([e_prev, e], axis=-1)
            den = jnp.sum(den, axis=-1, keepdims=True)
            acc = pv(e.astype(BF16), v[j0:j0 + jb])
            o2 = acc / den
            l2 = jnp.broadcast_to(row_max + jnp.log(den), o2.shape)
            o = jnp.where(sub0, o2[:, :CHUNK], o2[:, CHUNK:])
            lse = jnp.where(sub0, l2[:, :CHUNK], l2[:, CHUNK:])
            for jj in range(jb):
                r, blk = divmod(j0 + jj, blocks_per_seq)
                rows = (pl.ds(blk * CHUNK * dil + r, CHUNK, stride=dil) if dil > 1
                        else pl.ds((j0 + jj) * CHUNK, CHUNK))
                o_scr[g, rows, :] = o[jj]
                l_scr[g, rows, :] = lse[jj]

    rows = 256
    for c in range(S // rows):
        sl = pl.ds(c * rows, rows)
        l0, l1, l2 = l_scr[0, sl, :], l_scr[1, sl, :], l_scr[2, sl, :]
        m = jnp.maximum(jnp.maximum(l0, l1), l2)
        e0, e1, e2 = jnp.exp(l0 - m), jnp.exp(l1 - m), jnp.exp(l2 - m)
        num = e0 * o_scr[0, sl, :] + e1 * o_scr[1, sl, :] + e2 * o_scr[2, sl, :]
        out_ref[0, sl, :] = (num / (e0 + e1 + e2)).astype(BF16)


def _attn_call(qs, bcur, bprev):
    B = qs[0].shape[0]
    S = qs[0].shape[1] * qs[0].shape[2]
    operands = [q.reshape(B, S, W_QKV_OUT) for q in qs]
    spec = pl.BlockSpec((1, S, W_QKV_OUT // 2), lambda b, p: (b, 0, p))
    return pl.pallas_call(
        _attn_kernel,
        grid=(B, 2),
        in_specs=[spec, spec, spec, _const_spec(bcur.shape), _const_spec(bprev.shape)],
        out_specs=pl.BlockSpec((1, S, 128), lambda b, p: (b, 0, p)),
        out_shape=jax.ShapeDtypeStruct((B, S, W_BRANCH), BF16),
        scratch_shapes=[
            pltpu.VMEM((3, S, 128), F32),
            pltpu.VMEM((3, S, 128), F32),
        ],
        compiler_params=pltpu.CompilerParams(
            dimension_semantics=("arbitrary", "arbitrary"), vmem_limit_bytes=VMEM_LIMIT),
        name="attn",
    )(*operands, bcur, bprev)


def _merge_kernel(x_ref, pa_ref, pb_ref, pc_ref, pd_ref, g_ref, wg_ref, wout_ref, wo_ref, o_ref):
    for s in range(TOK_TILE // SUB_TILE):
        rows = pl.ds(s * SUB_TILE, SUB_TILE)
        x = x_ref[rows, :]
        hb = _rmsnorm(x, g_ref[...]).astype(BF16)
        merged = None
        for br, p_ref in enumerate((pa_ref, pb_ref, pc_ref, pd_ref)):
            gate = jax.nn.sigmoid(_dot(hb, wg_ref[0, :, br * D_MODEL:(br + 1) * D_MODEL]))
            term = gate * _dot(p_ref[rows, :], wout_ref[0, br])
            merged = term if merged is None else merged + term
        o_ref[rows, :] = x + _dot(merged.astype(BF16), wo_ref[0])


def _merge_call(layer, x, pa, pb, pc, pd, g, w_gate, w_out, w_o):
    T, D = x.shape
    tt = TOK_TILE
    tok = lambda i: (i, 0)
    pspec = pl.BlockSpec((tt, W_BRANCH), tok)
    return pl.pallas_call(
        _merge_kernel,
        grid=(T // tt,),
        in_specs=[pl.BlockSpec((tt, D), tok), pspec, pspec, pspec, pspec,
                  _const_spec((1, D)), _layer_spec((1,) + w_gate.shape[1:], layer),
                  _layer_spec((1,) + w_out.shape[1:], layer), _layer_spec((1, D, D), layer)],
        out_specs=pl.BlockSpec((tt, D), tok),
        out_shape=jax.ShapeDtypeStruct((T, D), F32),
        compiler_params=pltpu.CompilerParams(
            dimension_semantics=("arbitrary",), vmem_limit_bytes=VMEM_LIMIT),
        name="merge",
    )(x, pa, pb, pc, pd, g, w_gate, w_out, w_o)


def _ffn_kernel(x_ref, g_ref, w1_ref, w2_ref, fg_ref, o_ref, *, final_norm):
    for s in range(TOK_TILE // SUB_TILE):
        rows = pl.ds(s * SUB_TILE, SUB_TILE)
        x = x_ref[rows, :]
        hb = _rmsnorm(x, g_ref[...]).astype(BF16)
        acts = []
        for c in range(D_FF // D_MODEL):
            a = _dot(hb, w1_ref[0, :, c * D_MODEL:(c + 1) * D_MODEL])
            acts.append(jnp.square(jnp.maximum(a, 0.0)).astype(BF16))
        y = x + _dot(jnp.concatenate(acts, axis=1), w2_ref[0])
        if final_norm:
            y = _rmsnorm(y, fg_ref[...])
        o_ref[rows, :] = y


def _ffn_call(layer, x, g, w1, w2, final_g, final_norm):
    T, D = x.shape
    tt = TOK_TILE
    tok = lambda i: (i, 0)
    return pl.pallas_call(
        functools.partial(_ffn_kernel, final_norm=final_norm),
        grid=(T // tt,),
        in_specs=[pl.BlockSpec((tt, D), tok), _const_spec((1, D)), _layer_spec((1, D, D_FF), layer),
                  _layer_spec((1, D_FF, D), layer), _const_spec((1, D))],
        out_specs=pl.BlockSpec((tt, D), tok),
        out_shape=jax.ShapeDtypeStruct((T, D), F32),
        compiler_params=pltpu.CompilerParams(
            dimension_semantics=("arbitrary",), vmem_limit_bytes=VMEM_LIMIT),
        name="ffn",
    )(x, g, w1, w2, final_g)


def _t5_bucket_np(dist):
    max_exact = N_BUCKETS // 2
    d_f = np.maximum(dist, 1).astype(np.float32)
    ratio = np.log(d_f / np.float32(max_exact)) / np.float32(math.log(MAX_DISTANCE / max_exact))
    large = max_exact + (ratio * np.float32(N_BUCKETS - max_exact)).astype(np.int32)
    large = np.minimum(large, N_BUCKETS - 1)
    return np.where(dist < max_exact, dist, large)


def _bucket_indices():
    qi = np.arange(CHUNK)[:, None]
    ki = np.arange(CHUNK)[None, :]
    d_cur = qi - ki
    d_prev = qi + CHUNK - ki
    cur = [np.where(d_cur >= 0, _t5_bucket_np(np.clip(d_cur, 0, None) * dil), -1) for dil in DILATIONS]
    prev = [np.where(d_prev <= CHUNK, _t5_bucket_np(d_prev * dil), -1) for dil in DILATIONS[:2]]
    return np.stack(cur).astype(np.int32), np.stack(prev).astype(np.int32)


def _bias_kernel(tab_ref, icur_ref, iprev_ref, bcur_ref, bprev_ref):
    for idx_ref, out_ref in ((icur_ref, bcur_ref), (iprev_ref, bprev_ref)):
        for g in range(idx_ref.shape[0]):
            idx = idx_ref[g]
            accs = [jnp.full((CHUNK, CHUNK), NEG_BIG, F32) for _ in range(HEADS_PER_GROUP)]
            for b in range(N_BUCKETS):
                hit = idx == b
                for h in range(HEADS_PER_GROUP):
                    accs[h] = jnp.where(hit, tab_ref[b, g * HEADS_PER_GROUP + h], accs[h])
            for h in range(HEADS_PER_GROUP):
                out_ref[g, h] = accs[h]


def _bias_tables(rel_bias):
    icur, iprev = _bucket_indices()
    vm = pl.BlockSpec(memory_space=pltpu.VMEM)
    return pl.pallas_call(
        _bias_kernel,
        in_specs=[pl.BlockSpec(memory_space=pltpu.SMEM), vm, vm],
        out_specs=(vm, vm),
        out_shape=(jax.ShapeDtypeStruct((3, HEADS_PER_GROUP, CHUNK, CHUNK), F32),
                   jax.ShapeDtypeStruct((2, HEADS_PER_GROUP, CHUNK, CHUNK), F32)),
        name="bias",
    )(rel_bias, jnp.asarray(icur), jnp.asarray(iprev))


def kernel(x, norm_mix_g, w_in, a_ln_g, a_ln_b, a_ws, a_bs, w_a_out, b_conv, w_b_out, rel_bias,
           w_c_out, d_w, d_scale, w_d_out, w_o, norm_ff_g, w_ff1, w_ff2, final_g):
    B, S, D = x.shape
    depth = w_in.shape[0]
    bcur, bprev = _bias_tables(rel_bias)
    qkv_cols = [w_in[:, :, col + g * W_BRANCH:col + (g + 1) * W_BRANCH]
                for g in range(len(DILATIONS)) for col in (COL_Q, COL_K, COL_V)]
    w_mix_b = jnp.concatenate([w_in[:, :, 0:COL_Q], w_in[:, :, COL_D:COL_GATE]] + qkv_cols,
                              axis=2).astype(BF16)
    w_gate_b = w_in[:, :, COL_GATE:].astype(BF16)
    w_out_b = jnp.stack([w_a_out, w_b_out, w_c_out, w_d_out], axis=1).astype(BF16)
    w_o_b = w_o.astype(BF16)
    w_ff1_b = w_ff1.astype(BF16)
    w_ff2_b = w_ff2.astype(BF16)
    T = B * S
    for l in range(depth):
        ws = jnp.transpose(a_ws[l], (1, 0, 2)).reshape(CHUNK, A_GROUPS * CHUNK)
        bs_full = jnp.repeat(a_bs[l].T, W_BRANCH // A_GROUPS, axis=1)
        wpool_bd = jax.scipy.linalg.block_diag(*[d_w[l, g] for g in range(4)]).astype(BF16)

        pa, pb, pd, *qs = _mix_call(
            l, x, norm_mix_g[l][None], w_mix_b, a_ln_g[l][None], a_ln_b[l][None], ws, bs_full,
            b_conv[l], wpool_bd, d_scale[l][None])
        pc = _attn_call(qs, bcur, bprev)
        x2 = _merge_call(l, x.reshape(T, D), pa.reshape(T, -1), pb.reshape(T, -1), pc.reshape(T, -1),
                         pd.reshape(T, -1), norm_mix_g[l][None], w_gate_b, w_out_b, w_o_b)
        x2 = _ffn_call(l, x2, norm_ff_g[l][None], w_ff1_b, w_ff2_b, final_g[None],
                       final_norm=(l == depth - 1))
        x = x2.reshape(B, S, D)
    return x
```

```python
import functools
import math

import jax
import jax.numpy as jnp
import numpy as np
from jax import lax
from jax.experimental import pallas as pl
from jax.experimental.pallas import tpu as pltpu

F32 = jnp.float32
BF16 = jnp.bfloat16

D_MODEL = 1024
W_BRANCH = 256
A_GROUPS = 4
CHUNK = 128
CONV_WIDTH = 3
DILATIONS = (1, 4, 16)
WINDOWS = (128, 512, 2048)
HEADS_PER_GROUP = 4
HEAD_DIM = 64
POOL_WINDOWS = (2, 4, 8, 16)
POOL_HALO = 16
CONV_HALO = 8
N_BRANCH = 4
D_FF = 4 * D_MODEL
N_BUCKETS = 32
MAX_DISTANCE = 2048
EPS = 1e-6
NEG_BIG = -1e30

COL_Q = 5 * W_BRANCH
COL_K = COL_Q + 3 * W_BRANCH
COL_V = COL_K + 3 * W_BRANCH
COL_D = COL_V + 3 * W_BRANCH
COL_GATE = COL_D + W_BRANCH
W_LOCAL = 6 * W_BRANCH
W_QKV = 3 * W_BRANCH
W_QKV_OUT = 4 * W_BRANCH

MIX_TILE = 1024
MIX_SUB = 512
TOK_TILE = 1024
SUB_TILE = 512
ATTN_BLOCKS_PER_PIECE = 8
VMEM_LIMIT = 56 * 1024 * 1024


def _rmsnorm(x, g):
    return x * lax.rsqrt(jnp.mean(x * x, axis=-1, keepdims=True) + EPS) * g


def _gelu_tanh(x):
    c = math.sqrt(2.0 / math.pi)
    return x * (0.5 * (1.0 + jnp.tanh(c * (x + 0.044715 * (x * x * x)))))


def _dot(a, b):
    return jnp.dot(a, b, preferred_element_type=F32)


def _const_spec(shape):
    n = len(shape)
    return pl.BlockSpec(shape, lambda *_: (0,) * n, pipeline_mode=pl.Buffered(1))


_CAST_ARITY = {"plain": (1, 1), "w_in": (1, 2), "w_out": (N_BRANCH, 1)}
_MIX_COLS = ((0, COL_Q), (COL_D, COL_GATE)) + tuple(
    (col + g * W_BRANCH, col + (g + 1) * W_BRANCH)
    for g in range(len(DILATIONS)) for col in (COL_Q, COL_K, COL_V))


def _cast_plumbing(jobs, n_steps, step_of):
    kinds, operands, in_specs, out_shapes, out_specs = [], [], [], [], []
    for kind, layer, srcs in jobs:
        kinds.append(kind)
        R, C = srcs[0].shape[1:]
        rows = R // n_steps
        assert rows * n_steps == R and rows % 16 == 0, (kind, R, n_steps)
        for src in srcs:
            operands.append(src)
            in_specs.append(pl.BlockSpec((1, rows, C), lambda *g, layer=layer: (layer, step_of(*g), 0)))
        row_blk = lambda *g: (step_of(*g), 0)
        if kind == "w_out":
            out_shapes.append(jax.ShapeDtypeStruct((len(srcs), R, C), BF16))
            out_specs.append(pl.BlockSpec((len(srcs), rows, C), lambda *g: (0, step_of(*g), 0)))
        elif kind == "w_in":
            for width in (sum(b - a for a, b in _MIX_COLS), C - COL_GATE):
                out_shapes.append(jax.ShapeDtypeStruct((R, width), BF16))
                out_specs.append(pl.BlockSpec((rows, width), row_blk))
        else:
            out_shapes.append(jax.ShapeDtypeStruct((R, C), BF16))
            out_specs.append(pl.BlockSpec((rows, C), row_blk))
    return tuple(kinds), operands, in_specs, out_shapes, out_specs


def _split_refs(refs, n_in, n_out, kinds):
    n_ci = sum(_CAST_ARITY[k][0] for k in kinds)
    n_co = sum(_CAST_ARITY[k][1] for k in kinds)
    a, b, c = n_in, n_in + n_ci, n_in + n_ci + n_out
    return refs[:a], refs[a:b], refs[b:c], refs[c:c + n_co], refs[c + n_co:]


def _run_casts(kinds, src_refs, dst_refs):
    i = o = 0
    for kind in kinds:
        n_i, n_o = _CAST_ARITY[kind]
        srcs, dsts = src_refs[i:i + n_i], dst_refs[o:o + n_o]
        i, o = i + n_i, o + n_o
        if kind == "plain":
            dsts[0][...] = srcs[0][0].astype(BF16)
        elif kind == "w_out":
            for k, src in enumerate(srcs):
                dsts[0][k] = src[0].astype(BF16)
        else:
            c = 0
            for a, b in _MIX_COLS:
                dsts[0][:, c:c + b - a] = srcs[0][0, :, a:b].astype(BF16)
                c += b - a
            dsts[1][...] = srcs[0][0, :, COL_GATE:].astype(BF16)


def _mix_kernel(*refs, casts):
    ins, cast_src, outs, cast_dst, (h_scr, bbuf, dbuf) = _split_refs(refs, 10, 6, casts)
    x_ref, g_ref, w_ref, lng_ref, lnb_ref, ws_ref, bs_ref, conv_ref, wpool_ref, dscale_ref = ins
    pa_ref, pb_ref, pd_ref, q0_ref, q1_ref, q2_ref = outs
    _run_casts(casts, cast_src, cast_dst)
    n_sub = MIX_TILE // MIX_SUB
    first = (pl.program_id(0) == 0) & (pl.program_id(1) == 0)

    @pl.when(first)
    def _():
        bbuf[...] = jnp.zeros(bbuf.shape, F32)
        dbuf[...] = jnp.zeros(dbuf.shape, F32)

    for s in range(n_sub):
        _mix_subtile(s, pl.program_id(1) * n_sub + s, x_ref, g_ref, w_ref, lng_ref, lnb_ref, ws_ref,
                     bs_ref, conv_ref, wpool_ref, dscale_ref, pa_ref, pb_ref, pd_ref,
                     (q0_ref, q1_ref, q2_ref), h_scr.at[s], bbuf.at[s], bbuf.at[(s - 1) % n_sub],
                     dbuf.at[s], dbuf.at[(s - 1) % n_sub])


def _mix_subtile(s, i, x_ref, g_ref, w_ref, lng_ref, lnb_ref, ws_ref, bs_ref, conv_ref, wpool_ref,
                 dscale_ref, pa_ref, pb_ref, pd_ref, q_refs, h_scr, bbuf, bbuf_prev, dbuf, dbuf_prev):
    ts = MIX_SUB
    rows = pl.ds(s * ts, ts)
    x = x_ref[0, rows, :]
    h = _rmsnorm(x, g_ref[...])
    for c in range(D_MODEL // 128):
        h_scr[c] = h[:, c * 128:(c + 1) * 128]
    hb = h.astype(BF16)
    zl = _dot(hb, w_ref[:, 0:W_LOCAL + W_QKV])

    u = _gelu_tanh(zl[:, 0:256])
    v = _gelu_tanh(zl[:, 256:512])
    mu = jnp.mean(v, axis=-1, keepdims=True)
    vc = v - mu
    var = jnp.mean(vc * vc, axis=-1, keepdims=True)
    v = vc * lax.rsqrt(var + EPS) * lng_ref[...] + lnb_ref[...]
    row = lax.broadcasted_iota(jnp.int32, (CHUNK, A_GROUPS * CHUNK), 0)
    col = lax.broadcasted_iota(jnp.int32, (CHUNK, A_GROUPS * CHUNK), 1) % CHUNK
    wtril = jnp.where(row >= col, ws_ref[...], 0.0).astype(BF16)
    grp = lax.broadcasted_iota(jnp.int32, (CHUNK, W_BRANCH), 1) // (W_BRANCH // A_GROUPS)
    svs = []
    for c in range(ts // CHUNK):
        vch = v[c * CHUNK:(c + 1) * CHUNK]
        stacked = jnp.concatenate(
            [jnp.where(grp == g, vch, 0.0) for g in range(A_GROUPS)], axis=0).astype(BF16)
        svs.append(_dot(wtril, stacked) + bs_ref[...])
    sv = jnp.concatenate(svs, axis=0)
    pa_ref[0, rows, :] = (u * sv).astype(BF16)

    prod = zl[:, 768:1024] * zl[:, 1024:1280]

    bbuf[0:CONV_HALO] = jnp.where(i > 0, bbuf_prev[ts:ts + CONV_HALO], 0.0)
    dbuf[0:POOL_HALO] = jnp.where(i > 0, dbuf_prev[ts:ts + POOL_HALO], 0.0)
    bbuf[CONV_HALO:ts + CONV_HALO] = prod
    cw = conv_ref[...]
    conv = (cw[0:1] * bbuf[pl.ds(CONV_HALO - 2, ts), :]
            + cw[1:2] * bbuf[pl.ds(CONV_HALO - 1, ts), :]
            + cw[2:3] * prod)
    pb_ref[0, rows, :] = (zl[:, 512:768] * conv).astype(BF16)

    dz = zl[:, 1280:1536]
    dbuf[POOL_HALO:ts + POOL_HALO] = dz
    lane = lax.broadcasted_iota(jnp.int32, (ts, 128), 1)
    first_half = lane < 64
    tpos = (lax.broadcasted_iota(jnp.int32, (ts, 128), 0) + (i * ts + 1)).astype(F32)

    def window_sums(e, levels):
        out = []
        s = e
        for k in range(levels):
            s = s + pltpu.roll(s, 1 << k, 0)
            out.append(s[POOL_HALO:])
        return out

    lo = window_sums(dbuf[:, 0:128], 2)
    hi = window_sums(dbuf[:, 128:256], 4)
    pooled_lo = jnp.where(first_half, lo[0], lo[1]) / jnp.minimum(tpos, jnp.where(first_half, 2.0, 4.0))
    pooled_hi = jnp.where(first_half, hi[2], hi[3]) / jnp.minimum(tpos, jnp.where(first_half, 8.0, 16.0))
    y = jnp.concatenate([pooled_lo, pooled_hi], axis=1) - dz
    pd_ref[0, rows, :] = (_dot(y.astype(BF16), wpool_ref[...]) * dscale_ref[...]).astype(BF16)

    sub0 = lax.broadcasted_iota(jnp.int32, (ts, 128), 1) < HEAD_DIM
    for g, (q_ref, dil) in enumerate(zip(q_refs, DILATIONS)):
        n = ts // dil
        if dil == 1:
            z = zl[:, W_LOCAL:W_LOCAL + W_QKV]
        else:
            hp = jnp.concatenate(
                [jnp.concatenate([h_scr[c, pl.ds(r, n, stride=dil), :] for r in range(dil)], axis=0)
                 for c in range(D_MODEL // 128)], axis=1).astype(BF16)
            c0 = W_LOCAL + g * W_QKV
            z = _dot(hp, w_ref[:, c0:c0 + W_QKV])
        q = z[:, 0:256] * (HEAD_DIM ** -0.5)
        pieces = []
        for pr in range(2):
            qp = q[:, pr * 128:(pr + 1) * 128]
            pieces.append(jnp.where(sub0, qp, 0.0))
            pieces.append(jnp.where(sub0, 0.0, qp))
            pieces.append(z[:, 256 + pr * 128:256 + (pr + 1) * 128])
            pieces.append(z[:, 512 + pr * 128:512 + (pr + 1) * 128])
        val = jnp.concatenate(pieces, axis=1).astype(BF16)
        for r in range(dil):
            q_ref[0, r, pl.ds(s * n, n), :] = val[r * n:(r + 1) * n]


def _mix_call(x, g, w_mix, lng, lnb, ws, bs_full, conv, wpool_bd, dscale, cast_jobs=()):
    B, S, D = x.shape
    ts = MIX_TILE
    nt = S // ts
    tok = lambda b, i: (b, i, 0)
    casts, c_ops, c_in, c_shapes, c_out = _cast_plumbing(cast_jobs, B * nt, lambda b, i: b * nt + i)
    out_shape = (
        jax.ShapeDtypeStruct((B, S, W_BRANCH), BF16),
        jax.ShapeDtypeStruct((B, S, W_BRANCH), BF16),
        jax.ShapeDtypeStruct((B, S, W_BRANCH), BF16),
    ) + tuple(jax.ShapeDtypeStruct((B, dil, S // dil, W_QKV_OUT), BF16) for dil in DILATIONS)
    in_specs = [
        pl.BlockSpec((1, ts, D), tok),
        _const_spec((1, D)),
        _const_spec(w_mix.shape),
        _const_spec((1, W_BRANCH)),
        _const_spec((1, W_BRANCH)),
        _const_spec((CHUNK, A_GROUPS * CHUNK)),
        _const_spec((CHUNK, W_BRANCH)),
        _const_spec((CONV_WIDTH, W_BRANCH)),
        _const_spec((W_BRANCH, W_BRANCH)),
        _const_spec((1, W_BRANCH)),
    ]
    out_specs = (
        pl.BlockSpec((1, ts, W_BRANCH), tok),
        pl.BlockSpec((1, ts, W_BRANCH), tok),
        pl.BlockSpec((1, ts, W_BRANCH), tok),
    ) + tuple(pl.BlockSpec((1, dil, ts // dil, W_QKV_OUT), lambda b, i: (b, 0, i, 0))
              for dil in DILATIONS)
    return pl.pallas_call(
        functools.partial(_mix_kernel, casts=casts),
        grid=(B, nt),
        in_specs=in_specs + c_in,
        out_specs=out_specs + tuple(c_out),
        out_shape=out_shape + tuple(c_shapes),
        scratch_shapes=[
            pltpu.VMEM((ts // MIX_SUB, D // 128, MIX_SUB, 128), F32),
            pltpu.VMEM((ts // MIX_SUB, MIX_SUB + CONV_HALO, W_BRANCH), F32),
            pltpu.VMEM((ts // MIX_SUB, MIX_SUB + POOL_HALO, W_BRANCH), F32),
        ],
        compiler_params=pltpu.CompilerParams(
            dimension_semantics=("arbitrary", "arbitrary"), vmem_limit_bytes=VMEM_LIMIT),
        name="mix",
    )(x, g, w_mix, lng, lnb, ws, bs_full, conv, wpool_bd, dscale, *c_ops)


def _attn_kernel(*refs, casts):
    ins, cast_src, (out_ref,), cast_dst, (o_scr, l_scr) = _split_refs(refs, 5, 1, casts)
    q0_ref, q1_ref, q2_ref, bcur_ref, bprev_ref = ins
    _run_casts(casts, cast_src, cast_dst)
    q_refs = (q0_ref, q1_ref, q2_ref)
    S = out_ref.shape[1]
    pr = pl.program_id(1)
    nb = S // CHUNK

    def cols(g, c0):
        return q_refs[g][0, :, c0:c0 + 128].reshape(nb, CHUNK, 128)

    jb = ATTN_BLOCKS_PER_PIECE
    sub0 = lax.broadcasted_iota(jnp.int32, (jb, CHUNK, 128), 2) < HEAD_DIM
    qk = lambda a, b: jnp.einsum("jqd,jkd->jqk", a, b, preferred_element_type=F32)
    pv = lambda a, b: jnp.einsum("jqk,jkd->jqd", a, b, preferred_element_type=F32)
    hh0 = 2 * pr

    for g, dil in enumerate(DILATIONS):
        blocks_per_seq = nb // dil
        has_prev = blocks_per_seq > 1
        q = jnp.concatenate([cols(g, 0), cols(g, 128)], axis=1)
        k = cols(g, 256)
        v = cols(g, 384)
        bias_cur = jnp.concatenate([bcur_ref[g, hh0], bcur_ref[g, hh0 + 1]], axis=0)[None]
        if has_prev:
            k = jnp.concatenate([jnp.concatenate([k[:1], k[:-1]], axis=0), k], axis=1)
            v = jnp.concatenate([jnp.concatenate([v[:1], v[:-1]], axis=0), v], axis=1)
            bias_prev = jnp.concatenate([bprev_ref[g, hh0], bprev_ref[g, hh0 + 1]], axis=0)[None]
        for j0 in range(0, nb, jb):
            s = qk(q[j0:j0 + jb], k[j0:j0 + jb])
            s_cur = s[:, :, -CHUNK:] + bias_cur
            top = s_cur
            if has_prev:
                j = lax.broadcasted_iota(jnp.int32, (jb, 2 * CHUNK, CHUNK), 0) + j0
                s_prev = jnp.where(j % blocks_per_seq != 0, s[:, :, :CHUNK] + bias_prev, NEG_BIG)
                top = jnp.maximum(s_cur, s_prev)
            row_max = jnp.max(top, axis=-1, keepdims=True)
            e = jnp.exp(s_cur - row_max)
            den = e
            if has_prev:
                e_prev = jnp.exp(s_prev - row_max)
                den = e + e_prev
                e = jnp.concatenate([e_prev, e], axis=-1)
            den = jnp.sum(den, axis=-1, keepdims=True)
            acc = pv(e.astype(BF16), v[j0:j0 + jb])
            o2 = acc / den
            l2 = jnp.broadcast_to(row_max + jnp.log(den), o2.shape)
            o = jnp.where(sub0, o2[:, :CHUNK], o2[:, CHUNK:])
            lse = jnp.where(sub0, l2[:, :CHUNK], l2[:, CHUNK:])
            for jj in range(jb):
                r, blk = divmod(j0 + jj, blocks_per_seq)
                rows = (pl.ds(blk * CHUNK * dil + r, CHUNK, stride=dil) if dil > 1
                        else pl.ds((j0 + jj) * CHUNK, CHUNK))
                o_scr[g, rows, :] = o[jj]
                l_scr[g, rows, :] = lse[jj]

    rows = 256
    for c in range(S // rows):
        sl = pl.ds(c * rows, rows)
        l0, l1, l2 = l_scr[0, sl, :], l_scr[1, sl, :], l_scr[2, sl, :]
        m = jnp.maximum(jnp.maximum(l0, l1), l2)
        e0, e1, e2 = jnp.exp(l0 - m), jnp.exp(l1 - m), jnp.exp(l2 - m)
        num = e0 * o_scr[0, sl, :] + e1 * o_scr[1, sl, :] + e2 * o_scr[2, sl, :]
        out_ref[0, sl, :] = (num / (e0 + e1 + e2)).astype(BF16)


def _attn_call(qs, bcur, bprev, cast_jobs=()):
    B = qs[0].shape[0]
    S = qs[0].shape[1] * qs[0].shape[2]
    operands = [q.reshape(B, S, W_QKV_OUT) for q in qs]
    spec = pl.BlockSpec((1, S, W_QKV_OUT // 2), lambda b, p: (b, 0, p))
    casts, c_ops, c_in, c_shapes, c_out = _cast_plumbing(cast_jobs, B * 2, lambda b, p: b * 2 + p)
    return pl.pallas_call(
        functools.partial(_attn_kernel, casts=casts),
        grid=(B, 2),
        in_specs=[spec, spec, spec, _const_spec(bcur.shape), _const_spec(bprev.shape)] + c_in,
        out_specs=(pl.BlockSpec((1, S, 128), lambda b, p: (b, 0, p)),) + tuple(c_out),
        out_shape=(jax.ShapeDtypeStruct((B, S, W_BRANCH), BF16),) + tuple(c_shapes),
        scratch_shapes=[
            pltpu.VMEM((3, S, 128), F32),
            pltpu.VMEM((3, S, 128), F32),
        ],
        compiler_params=pltpu.CompilerParams(
            dimension_semantics=("arbitrary", "arbitrary"), vmem_limit_bytes=VMEM_LIMIT),
        name="attn",
    )(*operands, bcur, bprev, *c_ops)


def _merge_kernel(*refs, casts):
    ins, cast_src, (o_ref,), cast_dst, _ = _split_refs(refs, 9, 1, casts)
    x_ref, pa_ref, pb_ref, pc_ref, pd_ref, g_ref, wg_ref, wout_ref, wo_ref = ins
    _run_casts(casts, cast_src, cast_dst)
    for s in range(TOK_TILE // SUB_TILE):
        rows = pl.ds(s * SUB_TILE, SUB_TILE)
        x = x_ref[rows, :]
        hb = _rmsnorm(x, g_ref[...]).astype(BF16)
        merged = None
        for br, p_ref in enumerate((pa_ref, pb_ref, pc_ref, pd_ref)):
            gate = jax.nn.sigmoid(_dot(hb, wg_ref[:, br * D_MODEL:(br + 1) * D_MODEL]))
            term = gate * _dot(p_ref[rows, :], wout_ref[br])
            merged = term if merged is None else merged + term
        o_ref[rows, :] = x + _dot(merged.astype(BF16), wo_ref[...])


def _merge_call(x, pa, pb, pc, pd, g, w_gate, w_out, w_o, cast_jobs=()):
    T, D = x.shape
    tt = TOK_TILE
    tok = lambda i: (i, 0)
    pspec = pl.BlockSpec((tt, W_BRANCH), tok)
    casts, c_ops, c_in, c_shapes, c_out = _cast_plumbing(cast_jobs, T // tt, lambda i: i)
    return pl.pallas_call(
        functools.partial(_merge_kernel, casts=casts),
        grid=(T // tt,),
        in_specs=[pl.BlockSpec((tt, D), tok), pspec, pspec, pspec, pspec,
                  _const_spec((1, D)), _const_spec(w_gate.shape), _const_spec(w_out.shape),
                  _const_spec(w_o.shape)] + c_in,
        out_specs=(pl.BlockSpec((tt, D), tok),) + tuple(c_out),
        out_shape=(jax.ShapeDtypeStruct((T, D), F32),) + tuple(c_shapes),
        compiler_params=pltpu.CompilerParams(
            dimension_semantics=("arbitrary",), vmem_limit_bytes=VMEM_LIMIT),
        name="merge",
    )(x, pa, pb, pc, pd, g, w_gate, w_out, w_o, *c_ops)


def _ffn_kernel(*refs, final_norm, casts):
    (x_ref, g_ref, w1_ref, w2_ref, fg_ref), cast_src, (o_ref,), cast_dst, _ = _split_refs(refs, 5, 1, casts)
    _run_casts(casts, cast_src, cast_dst)
    for s in range(TOK_TILE // SUB_TILE):
        rows = pl.ds(s * SUB_TILE, SUB_TILE)
        x = x_ref[rows, :]
        hb = _rmsnorm(x, g_ref[...]).astype(BF16)
        acts = []
        for c in range(D_FF // D_MODEL):
            a = _dot(hb, w1_ref[:, c * D_MODEL:(c + 1) * D_MODEL])
            acts.append(jnp.square(jnp.maximum(a, 0.0)).astype(BF16))
        y = x + _dot(jnp.concatenate(acts, axis=1), w2_ref[...])
        if final_norm:
            y = _rmsnorm(y, fg_ref[...])
        o_ref[rows, :] = y


def _ffn_call(x, g, w1, w2, final_g, final_norm, cast_jobs=()):
    T, D = x.shape
    tt = TOK_TILE
    tok = lambda i: (i, 0)
    casts, c_ops, c_in, c_shapes, c_out = _cast_plumbing(cast_jobs, T // tt, lambda i: i)
    return pl.pallas_call(
        functools.partial(_ffn_kernel, final_norm=final_norm, casts=casts),
        grid=(T // tt,),
        in_specs=[pl.BlockSpec((tt, D), tok), _const_spec((1, D)), _const_spec(w1.shape),
                  _const_spec(w2.shape), _const_spec((1, D))] + c_in,
        out_specs=(pl.BlockSpec((tt, D), tok),) + tuple(c_out),
        out_shape=(jax.ShapeDtypeStruct((T, D), F32),) + tuple(c_shapes),
        compiler_params=pltpu.CompilerParams(
            dimension_semantics=("arbitrary",), vmem_limit_bytes=VMEM_LIMIT),
        name="ffn",
    )(x, g, w1, w2, final_g, *c_ops)


def _t5_bucket_np(dist):
    max_exact = N_BUCKETS // 2
    d_f = np.maximum(dist, 1).astype(np.float32)
    ratio = np.log(d_f / np.float32(max_exact)) / np.float32(math.log(MAX_DISTANCE / max_exact))
    large = max_exact + (ratio * np.float32(N_BUCKETS - max_exact)).astype(np.int32)
    large = np.minimum(large, N_BUCKETS - 1)
    return np.where(dist < max_exact, dist, large)


def _bucket_indices():
    qi = np.arange(CHUNK)[:, None]
    ki = np.arange(CHUNK)[None, :]
    d_cur = qi - ki
    d_prev = qi + CHUNK - ki
    cur = [np.where(d_cur >= 0, _t5_bucket_np(np.clip(d_cur, 0, None) * dil), -1) for dil in DILATIONS]
    prev = [np.where(d_prev <= CHUNK, _t5_bucket_np(d_prev * dil), -1) for dil in DILATIONS[:2]]
    return np.stack(cur).astype(np.int32), np.stack(prev).astype(np.int32)


def _bias_kernel(tab_ref, icur_ref, iprev_ref, bcur_ref, bprev_ref):
    for idx_ref, out_ref in ((icur_ref, bcur_ref), (iprev_ref, bprev_ref)):
        for g in range(idx_ref.shape[0]):
            idx = idx_ref[g]
            accs = [jnp.full((CHUNK, CHUNK), NEG_BIG, F32) for _ in range(HEADS_PER_GROUP)]
            for b in range(N_BUCKETS):
                hit = idx == b
                for h in range(HEADS_PER_GROUP):
                    accs[h] = jnp.where(hit, tab_ref[b, g * HEADS_PER_GROUP + h], accs[h])
            for h in range(HEADS_PER_GROUP):
                out_ref[g, h] = accs[h]


def _bias_tables(rel_bias):
    icur, iprev = _bucket_indices()
    vm = pl.BlockSpec(memory_space=pltpu.VMEM)
    return pl.pallas_call(
        _bias_kernel,
        in_specs=[pl.BlockSpec(memory_space=pltpu.SMEM), vm, vm],
        out_specs=(vm, vm),
        out_shape=(jax.ShapeDtypeStruct((3, HEADS_PER_GROUP, CHUNK, CHUNK), F32),
                   jax.ShapeDtypeStruct((2, HEADS_PER_GROUP, CHUNK, CHUNK), F32)),
        name="bias",
    )(rel_bias, jnp.asarray(icur), jnp.asarray(iprev))


def kernel(x, norm_mix_g, w_in, a_ln_g, a_ln_b, a_ws, a_bs, w_a_out, b_conv, w_b_out, rel_bias,
           w_c_out, d_w, d_scale, w_d_out, w_o, norm_ff_g, w_ff1, w_ff2, final_g):
    B, S, D = x.shape
    depth = w_in.shape[0]
    bcur, bprev = _bias_tables(rel_bias)
    w_outs = (w_a_out, w_b_out, w_c_out, w_d_out)
    w_mix_b = jnp.concatenate([w_in[0][:, a:b] for a, b in _MIX_COLS], axis=1).astype(BF16)
    w_gate_b = w_in[0][:, COL_GATE:].astype(BF16)
    T = B * S
    for l in range(depth):
        ws = jnp.transpose(a_ws[l], (1, 0, 2)).reshape(CHUNK, A_GROUPS * CHUNK)
        bs_full = jnp.repeat(a_bs[l].T, W_BRANCH // A_GROUPS, axis=1)
        wpool_bd = jax.scipy.linalg.block_diag(*[d_w[l, g] for g in range(4)]).astype(BF16)
        more = l + 1 < depth

        mix_jobs = (("plain", 0, (w_o,)), ("w_out", 0, w_outs)) if l == 0 else ()
        pa, pb, pd, q0, q1, q2, *cast = _mix_call(
            x, norm_mix_g[l][None], w_mix_b, a_ln_g[l][None], a_ln_b[l][None], ws, bs_full,
            b_conv[l], wpool_bd, d_scale[l][None], cast_jobs=mix_jobs)
        if l == 0:
            w_o_b, w_out_b = cast

        attn_jobs = (("plain", 0, (w_ff1,)), ("plain", 0, (w_ff2,))) if l == 0 else ()
        pc, *cast = _attn_call((q0, q1, q2), bcur, bprev, cast_jobs=attn_jobs)
        if l == 0:
            w_ff1_b, w_ff2_b = cast

        merge_jobs = (("w_in", l + 1, (w_in,)),) if more else ()
        x2, *cast = _merge_call(x.reshape(T, D), pa.reshape(T, -1), pb.reshape(T, -1),
                                pc.reshape(T, -1), pd.reshape(T, -1), norm_mix_g[l][None],
                                w_gate_b, w_out_b, w_o_b, cast_jobs=merge_jobs)
        if more:
            w_mix_b, w_gate_b = cast

        ffn_jobs = (("plain", l + 1, (w_ff1,)), ("plain", l + 1, (w_ff2,)), ("plain", l + 1, (w_o,)),
                    ("w_out", l + 1, w_outs)) if more else ()
        x2, *cast = _ffn_call(x2, norm_ff_g[l][None], w_ff1_b, w_ff2_b, final_g[None],
                              final_norm=not more, cast_jobs=ffn_jobs)
        if more:
            w_ff1_b, w_ff2_b, w_o_b, w_out_b = cast
        x = x2.reshape(B, S, D)
    return x
```

```python
import functools
import math

import jax
import jax.numpy as jnp
import numpy as np
from jax import lax
from jax.experimental import pallas as pl
from jax.experimental.pallas import tpu as pltpu

F32 = jnp.float32
BF16 = jnp.bfloat16

D_MODEL = 1024
W_BRANCH = 256
A_GROUPS = 4
CHUNK = 128
CONV_WIDTH = 3
DILATIONS = (1, 4, 16)
WINDOWS = (128, 512, 2048)
HEADS_PER_GROUP = 4
HEAD_DIM = 64
POOL_WINDOWS = (2, 4, 8, 16)
POOL_GROUPS = len(POOL_WINDOWS)
POOL_HALO = 16
CONV_HALO = 8
N_BRANCH = 4
D_FF = 4 * D_MODEL
N_BUCKETS = 32
MAX_DISTANCE = 2048
EPS = 1e-6
NEG_BIG = -1e30

COL_Q = 5 * W_BRANCH
COL_K = COL_Q + 3 * W_BRANCH
COL_V = COL_K + 3 * W_BRANCH
COL_D = COL_V + 3 * W_BRANCH
COL_GATE = COL_D + W_BRANCH
W_LOCAL = 6 * W_BRANCH
W_QKV = 3 * W_BRANCH
W_QKV_OUT = 4 * W_BRANCH

MIX_TILE = 1024
MIX_SUB = 512
TOK_TILE = 1024
SUB_TILE = 512
ATTN_BLOCKS_PER_PIECE = 8
VMEM_LIMIT = 56 * 1024 * 1024


def _rmsnorm(x, g):
    return x * lax.rsqrt(jnp.mean(x * x, axis=-1, keepdims=True) + EPS) * g


def _gelu_tanh(x):
    c = math.sqrt(2.0 / math.pi)
    return x * (0.5 * (1.0 + jnp.tanh(c * (x + 0.044715 * (x * x * x)))))


def _dot(a, b):
    return jnp.dot(a, b, preferred_element_type=F32)


def _const_spec(shape):
    n = len(shape)
    return pl.BlockSpec(shape, lambda *_: (0,) * n, pipeline_mode=pl.Buffered(1))


_CAST_ARITY = {"plain": (1, 1), "w_in": (1, 2), "gate": (1, 1), "w_out": (N_BRANCH, 1)}
_MIX_COLS = ((0, COL_Q), (COL_D, COL_GATE)) + tuple(
    (col + g * W_BRANCH, col + (g + 1) * W_BRANCH)
    for g in range(len(DILATIONS)) for col in (COL_Q, COL_K, COL_V))


def _cast_plumbing(jobs, n_steps, step_of):
    kinds, operands, in_specs, out_shapes, out_specs = [], [], [], [], []
    for kind, layer, srcs in jobs:
        kinds.append(kind)
        R, C = srcs[0].shape[1:]
        rows = R // n_steps
        assert rows * n_steps == R and rows % 16 == 0, (kind, R, n_steps)
        for src in srcs:
            operands.append(src)
            in_specs.append(pl.BlockSpec((1, rows, C), lambda *g, layer=layer: (layer, step_of(*g), 0)))
        row_blk = lambda *g: (step_of(*g), 0)
        if kind == "w_out":
            out_shapes.append(jax.ShapeDtypeStruct((len(srcs), R, C), BF16))
            out_specs.append(pl.BlockSpec((len(srcs), rows, C), lambda *g: (0, step_of(*g), 0)))
        elif kind in ("w_in", "gate"):
            widths = (C - COL_GATE,) if kind == "gate" else (sum(b - a for a, b in _MIX_COLS), C - COL_GATE)
            for width in widths:
                out_shapes.append(jax.ShapeDtypeStruct((R, width), BF16))
                out_specs.append(pl.BlockSpec((rows, width), row_blk))
        else:
            out_shapes.append(jax.ShapeDtypeStruct((R, C), BF16))
            out_specs.append(pl.BlockSpec((rows, C), row_blk))
    return tuple(kinds), operands, in_specs, out_shapes, out_specs


def _split_refs(refs, n_in, n_out, kinds):
    n_ci = sum(_CAST_ARITY[k][0] for k in kinds)
    n_co = sum(_CAST_ARITY[k][1] for k in kinds)
    a, b, c = n_in, n_in + n_ci, n_in + n_ci + n_out
    return refs[:a], refs[a:b], refs[b:c], refs[c:c + n_co], refs[c + n_co:]


def _run_casts(kinds, src_refs, dst_refs):
    i = o = 0
    for kind in kinds:
        n_i, n_o = _CAST_ARITY[kind]
        srcs, dsts = src_refs[i:i + n_i], dst_refs[o:o + n_o]
        i, o = i + n_i, o + n_o
        if kind == "plain":
            dsts[0][...] = srcs[0][0].astype(BF16)
        elif kind == "w_out":
            for k, src in enumerate(srcs):
                dsts[0][k] = src[0].astype(BF16)
        elif kind == "gate":
            dsts[0][...] = srcs[0][0, :, COL_GATE:].astype(BF16)
        else:
            c = 0
            for a, b in _MIX_COLS:
                dsts[0][:, c:c + b - a] = srcs[0][0, :, a:b].astype(BF16)
                c += b - a
            dsts[1][...] = srcs[0][0, :, COL_GATE:].astype(BF16)


def _layer_row(ref, layer):
    return ref.at[pl.ds(layer, 1)]


def _mix_kernel(*refs, layer, casts):
    ins, cast_src, outs, cast_dst, (h_scr, bbuf, dbuf) = _split_refs(refs, 10, 6, casts)
    x_ref, g_ref, w_ref, lng_ref, lnb_ref, ws_ref, bs_ref, conv_ref, wpool_ref, dscale_ref = ins
    g_ref, lng_ref, lnb_ref, dscale_ref = (_layer_row(r, layer) for r in (g_ref, lng_ref, lnb_ref, dscale_ref))
    ws_ref, bs_ref, conv_ref, wpool_ref = (r.at[layer] for r in (ws_ref, bs_ref, conv_ref, wpool_ref))
    pa_ref, pb_ref, pd_ref, q0_ref, q1_ref, q2_ref = outs
    _run_casts(casts, cast_src, cast_dst)
    n_sub = MIX_TILE // MIX_SUB
    first = (pl.program_id(0) == 0) & (pl.program_id(1) == 0)

    @pl.when(first)
    def _():
        bbuf[...] = jnp.zeros(bbuf.shape, F32)
        dbuf[...] = jnp.zeros(dbuf.shape, F32)

    for s in range(n_sub):
        _mix_subtile(s, pl.program_id(1) * n_sub + s, x_ref, g_ref, w_ref, lng_ref, lnb_ref, ws_ref,
                     bs_ref, conv_ref, wpool_ref, dscale_ref, pa_ref, pb_ref, pd_ref,
                     (q0_ref, q1_ref, q2_ref), h_scr.at[s], bbuf.at[s], bbuf.at[(s - 1) % n_sub],
                     dbuf.at[s], dbuf.at[(s - 1) % n_sub])


def _mix_subtile(s, i, x_ref, g_ref, w_ref, lng_ref, lnb_ref, ws_ref, bs_ref, conv_ref, wpool_ref,
                 dscale_ref, pa_ref, pb_ref, pd_ref, q_refs, h_scr, bbuf, bbuf_prev, dbuf, dbuf_prev):
    ts = MIX_SUB
    rows = pl.ds(s * ts, ts)
    x = x_ref[0, rows, :]
    h = _rmsnorm(x, g_ref[...])
    for c in range(D_MODEL // 128):
        h_scr[c] = h[:, c * 128:(c + 1) * 128]
    hb = h.astype(BF16)
    zl = _dot(hb, w_ref[:, 0:W_LOCAL + W_QKV])

    u = _gelu_tanh(zl[:, 0:256])
    v = _gelu_tanh(zl[:, 256:512])
    mu = jnp.mean(v, axis=-1, keepdims=True)
    vc = v - mu
    var = jnp.mean(vc * vc, axis=-1, keepdims=True)
    v = vc * lax.rsqrt(var + EPS) * lng_ref[...] + lnb_ref[...]
    row = lax.broadcasted_iota(jnp.int32, (CHUNK, A_GROUPS * CHUNK), 0)
    col = lax.broadcasted_iota(jnp.int32, (CHUNK, A_GROUPS * CHUNK), 1) % CHUNK
    w_all = jnp.concatenate([ws_ref[g] for g in range(A_GROUPS)], axis=1)
    wtril = jnp.where(row >= col, w_all, 0.0).astype(BF16)
    grp = lax.broadcasted_iota(jnp.int32, (CHUNK, W_BRANCH), 1) // (W_BRANCH // A_GROUPS)
    svs = []
    for c in range(ts // CHUNK):
        vch = v[c * CHUNK:(c + 1) * CHUNK]
        stacked = jnp.concatenate(
            [jnp.where(grp == g, vch, 0.0) for g in range(A_GROUPS)], axis=0).astype(BF16)
        svs.append(_dot(wtril, stacked) + bs_ref[...])
    sv = jnp.concatenate(svs, axis=0)
    pa_ref[0, rows, :] = (u * sv).astype(BF16)

    prod = zl[:, 768:1024] * zl[:, 1024:1280]

    bbuf[0:CONV_HALO] = jnp.where(i > 0, bbuf_prev[ts:ts + CONV_HALO], 0.0)
    dbuf[0:POOL_HALO] = jnp.where(i > 0, dbuf_prev[ts:ts + POOL_HALO], 0.0)
    bbuf[CONV_HALO:ts + CONV_HALO] = prod
    cw = conv_ref[...]
    conv = (cw[0:1] * bbuf[pl.ds(CONV_HALO - 2, ts), :]
            + cw[1:2] * bbuf[pl.ds(CONV_HALO - 1, ts), :]
            + cw[2:3] * prod)
    pb_ref[0, rows, :] = (zl[:, 512:768] * conv).astype(BF16)

    dz = zl[:, 1280:1536]
    dbuf[POOL_HALO:ts + POOL_HALO] = dz
    lane = lax.broadcasted_iota(jnp.int32, (ts, 128), 1)
    first_half = lane < 64
    tpos = (lax.broadcasted_iota(jnp.int32, (ts, 128), 0) + (i * ts + 1)).astype(F32)

    def window_sums(e, levels):
        out = []
        s = e
        for k in range(levels):
            s = s + pltpu.roll(s, 1 << k, 0)
            out.append(s[POOL_HALO:])
        return out

    lo = window_sums(dbuf[:, 0:128], 2)
    hi = window_sums(dbuf[:, 128:256], 4)
    pooled_lo = jnp.where(first_half, lo[0], lo[1]) / jnp.minimum(tpos, jnp.where(first_half, 2.0, 4.0))
    pooled_hi = jnp.where(first_half, hi[2], hi[3]) / jnp.minimum(tpos, jnp.where(first_half, 8.0, 16.0))
    y = jnp.concatenate([pooled_lo, pooled_hi], axis=1) - dz
    pd_ref[0, rows, :] = (_dot(y.astype(BF16), wpool_ref[...]) * dscale_ref[...]).astype(BF16)

    sub0 = lax.broadcasted_iota(jnp.int32, (ts, 128), 1) < HEAD_DIM
    for g, (q_ref, dil) in enumerate(zip(q_refs, DILATIONS)):
        n = ts // dil
        if dil == 1:
            z = zl[:, W_LOCAL:W_LOCAL + W_QKV]
        else:
            hp = jnp.concatenate(
                [jnp.concatenate([h_scr[c, pl.ds(r, n, stride=dil), :] for r in range(dil)], axis=0)
                 for c in range(D_MODEL // 128)], axis=1).astype(BF16)
            c0 = W_LOCAL + g * W_QKV
            z = _dot(hp, w_ref[:, c0:c0 + W_QKV])
        q = z[:, 0:256] * (HEAD_DIM ** -0.5)
        pieces = []
        for pr in range(2):
            qp = q[:, pr * 128:(pr + 1) * 128]
            pieces.append(jnp.where(sub0, qp, 0.0))
            pieces.append(jnp.where(sub0, 0.0, qp))
            pieces.append(z[:, 256 + pr * 128:256 + (pr + 1) * 128])
            pieces.append(z[:, 512 + pr * 128:512 + (pr + 1) * 128])
        val = jnp.concatenate(pieces, axis=1).astype(BF16)
        for r in range(dil):
            q_ref[0, r, pl.ds(s * n, n), :] = val[r * n:(r + 1) * n]


def _mix_call(layer, x, g, w_mix, lng, lnb, ws, bs_full, conv, wpool_bd, dscale, cast_jobs=()):
    B, S, D = x.shape
    ts = MIX_TILE
    nt = S // ts
    tok = lambda b, i: (b, i, 0)
    casts, c_ops, c_in, c_shapes, c_out = _cast_plumbing(cast_jobs, B * nt, lambda b, i: b * nt + i)
    out_shape = (
        jax.ShapeDtypeStruct((B, S, W_BRANCH), BF16),
        jax.ShapeDtypeStruct((B, S, W_BRANCH), BF16),
        jax.ShapeDtypeStruct((B, S, W_BRANCH), BF16),
    ) + tuple(jax.ShapeDtypeStruct((B, dil, S // dil, W_QKV_OUT), BF16) for dil in DILATIONS)
    in_specs = [
        pl.BlockSpec((1, ts, D), tok),
    ] + [_const_spec(a.shape) for a in (g, w_mix, lng, lnb, ws, bs_full, conv, wpool_bd, dscale)]
    out_specs = (
        pl.BlockSpec((1, ts, W_BRANCH), tok),
        pl.BlockSpec((1, ts, W_BRANCH), tok),
        pl.BlockSpec((1, ts, W_BRANCH), tok),
    ) + tuple(pl.BlockSpec((1, dil, ts // dil, W_QKV_OUT), lambda b, i: (b, 0, i, 0))
              for dil in DILATIONS)
    return pl.pallas_call(
        functools.partial(_mix_kernel, layer=layer, casts=casts),
        grid=(B, nt),
        in_specs=in_specs + c_in,
        out_specs=out_specs + tuple(c_out),
        out_shape=out_shape + tuple(c_shapes),
        scratch_shapes=[
            pltpu.VMEM((ts // MIX_SUB, D // 128, MIX_SUB, 128), F32),
            pltpu.VMEM((ts // MIX_SUB, MIX_SUB + CONV_HALO, W_BRANCH), F32),
            pltpu.VMEM((ts // MIX_SUB, MIX_SUB + POOL_HALO, W_BRANCH), F32),
        ],
        compiler_params=pltpu.CompilerParams(
            dimension_semantics=("arbitrary", "arbitrary"), vmem_limit_bytes=VMEM_LIMIT),
        name="mix",
    )(x, g, w_mix, lng, lnb, ws, bs_full, conv, wpool_bd, dscale, *c_ops)


def _attn_kernel(*refs, casts):
    ins, cast_src, (out_ref,), cast_dst, (o_scr, l_scr) = _split_refs(refs, 5, 1, casts)
    q0_ref, q1_ref, q2_ref, bcur_ref, bprev_ref = ins
    _run_casts(casts, cast_src, cast_dst)
    q_refs = (q0_ref, q1_ref, q2_ref)
    S = out_ref.shape[1]
    pr = pl.program_id(1)
    nb = S // CHUNK

    def cols(g, c0):
        return q_refs[g][0, :, c0:c0 + 128].reshape(nb, CHUNK, 128)

    jb = ATTN_BLOCKS_PER_PIECE
    sub0 = lax.broadcasted_iota(jnp.int32, (jb, CHUNK, 128), 2) < HEAD_DIM
    qk = lambda a, b: jnp.einsum("jqd,jkd->jqk", a, b, preferred_element_type=F32)
    pv = lambda a, b: jnp.einsum("jqk,jkd->jqd", a, b, preferred_element_type=F32)
    hh0 = 2 * pr

    for g, dil in enumerate(DILATIONS):
        blocks_per_seq = nb // dil
        has_prev = blocks_per_seq > 1
        q = jnp.concatenate([cols(g, 0), cols(g, 128)], axis=1)
        k = cols(g, 256)
        v = cols(g, 384)
        bias_cur = jnp.concatenate([bcur_ref[g, hh0], bcur_ref[g, hh0 + 1]], axis=0)[None]
        if has_prev:
            k = jnp.concatenate([jnp.concatenate([k[:1], k[:-1]], axis=0), k], axis=1)
            v = jnp.concatenate([jnp.concatenate([v[:1], v[:-1]], axis=0), v], axis=1)
            bias_prev = jnp.concatenate([bprev_ref[g, hh0], bprev_ref[g, hh0 + 1]], axis=0)[None]
        for j0 in range(0, nb, jb):
            s = qk(q[j0:j0 + jb], k[j0:j0 + jb])
            s_cur = s[:, :, -CHUNK:] + bias_cur
            top = s_cur
            if has_prev:
                j = lax.broadcasted_iota(jnp.int32, (jb, 2 * CHUNK, CHUNK), 0) + j0
                s_prev = jnp.where(j % blocks_per_seq != 0, s[:, :, :CHUNK] + bias_prev, NEG_BIG)
                top = jnp.maximum(s_cur, s_prev)
            row_max = jnp.max(top, axis=-1, keepdims=True)
            e = jnp.exp(s_cur - row_max)
            den = e
            if has_prev:
                e_prev = jnp.exp(s_prev - row_max)
                den = e + e_prev
                e = jnp.concatenate([e_prev, e], axis=-1)
            den = jnp.sum(den, axis=-1, keepdims=True)
            acc = pv(e.astype(BF16), v[j0:j0 + jb])
            o2 = acc / den
            l2 = jnp.broadcast_to(row_max + jnp.log(den), o2.shape)
            o = jnp.where(sub0, o2[:, :CHUNK], o2[:, CHUNK:])
            lse = jnp.where(sub0, l2[:, :CHUNK], l2[:, CHUNK:])
            for jj in range(jb):
                r, blk = divmod(j0 + jj, blocks_per_seq)
                rows = (pl.ds(blk * CHUNK * dil + r, CHUNK, stride=dil) if dil > 1
                        else pl.ds((j0 + jj) * CHUNK, CHUNK))
                o_scr[g, rows, :] = o[jj]
                l_scr[g, rows, :] = lse[jj]

    rows = 256
    for c in range(S // rows):
        sl = pl.ds(c * rows, rows)
        l0, l1, l2 = l_scr[0, sl, :], l_scr[1, sl, :], l_scr[2, sl, :]
        m = jnp.maximum(jnp.maximum(l0, l1), l2)
        e0, e1, e2 = jnp.exp(l0 - m), jnp.exp(l1 - m), jnp.exp(l2 - m)
        num = e0 * o_scr[0, sl, :] + e1 * o_scr[1, sl, :] + e2 * o_scr[2, sl, :]
        out_ref[0, sl, :] = (num / (e0 + e1 + e2)).astype(BF16)


def _attn_call(qs, bcur, bprev, cast_jobs=()):
    B = qs[0].shape[0]
    S = qs[0].shape[1] * qs[0].shape[2]
    operands = [q.reshape(B, S, W_QKV_OUT) for q in qs]
    spec = pl.BlockSpec((1, S, W_QKV_OUT // 2), lambda b, p: (b, 0, p))
    casts, c_ops, c_in, c_shapes, c_out = _cast_plumbing(cast_jobs, B * 2, lambda b, p: b * 2 + p)
    return pl.pallas_call(
        functools.partial(_attn_kernel, casts=casts),
        grid=(B, 2),
        in_specs=[spec, spec, spec, _const_spec(bcur.shape), _const_spec(bprev.shape)] + c_in,
        out_specs=(pl.BlockSpec((1, S, 128), lambda b, p: (b, 0, p)),) + tuple(c_out),
        out_shape=(jax.ShapeDtypeStruct((B, S, W_BRANCH), BF16),) + tuple(c_shapes),
        scratch_shapes=[
            pltpu.VMEM((3, S, 128), F32),
            pltpu.VMEM((3, S, 128), F32),
        ],
        compiler_params=pltpu.CompilerParams(
            dimension_semantics=("arbitrary", "arbitrary"), vmem_limit_bytes=VMEM_LIMIT),
        name="attn",
    )(*operands, bcur, bprev, *c_ops)


def _merge_kernel(*refs, layer, casts):
    ins, cast_src, (o_ref,), cast_dst, _ = _split_refs(refs, 9, 1, casts)
    x_ref, pa_ref, pb_ref, pc_ref, pd_ref, g_ref, wg_ref, wout_ref, wo_ref = ins
    g_ref = _layer_row(g_ref, layer)
    _run_casts(casts, cast_src, cast_dst)
    for s in range(TOK_TILE // SUB_TILE):
        rows = pl.ds(s * SUB_TILE, SUB_TILE)
        x = x_ref[rows, :]
        hb = _rmsnorm(x, g_ref[...]).astype(BF16)
        merged = None
        for br, p_ref in enumerate((pa_ref, pb_ref, pc_ref, pd_ref)):
            gate = jax.nn.sigmoid(_dot(hb, wg_ref[:, br * D_MODEL:(br + 1) * D_MODEL]))
            term = gate * _dot(p_ref[rows, :], wout_ref[br])
            merged = term if merged is None else merged + term
        o_ref[rows, :] = x + _dot(merged.astype(BF16), wo_ref[...])


def _merge_call(layer, x, pa, pb, pc, pd, g, w_gate, w_out, w_o, cast_jobs=()):
    T, D = x.shape
    tt = TOK_TILE
    tok = lambda i: (i, 0)
    pspec = pl.BlockSpec((tt, W_BRANCH), tok)
    casts, c_ops, c_in, c_shapes, c_out = _cast_plumbing(cast_jobs, T // tt, lambda i: i)
    return pl.pallas_call(
        functools.partial(_merge_kernel, layer=layer, casts=casts),
        grid=(T // tt,),
        in_specs=[pl.BlockSpec((tt, D), tok), pspec, pspec, pspec, pspec,
                  _const_spec(g.shape), _const_spec(w_gate.shape), _const_spec(w_out.shape),
                  _const_spec(w_o.shape)] + c_in,
        out_specs=(pl.BlockSpec((tt, D), tok),) + tuple(c_out),
        out_shape=(jax.ShapeDtypeStruct((T, D), F32),) + tuple(c_shapes),
        compiler_params=pltpu.CompilerParams(
            dimension_semantics=("arbitrary",), vmem_limit_bytes=VMEM_LIMIT),
        name="merge",
    )(x, pa, pb, pc, pd, g, w_gate, w_out, w_o, *c_ops)


def _ffn_kernel(*refs, layer, final_norm, casts):
    (x_ref, g_ref, w1_ref, w2_ref, fg_ref), cast_src, (o_ref,), cast_dst, _ = _split_refs(refs, 5, 1, casts)
    g_ref = _layer_row(g_ref, layer)
    _run_casts(casts, cast_src, cast_dst)
    for s in range(TOK_TILE // SUB_TILE):
        rows = pl.ds(s * SUB_TILE, SUB_TILE)
        x = x_ref[rows, :]
        hb = _rmsnorm(x, g_ref[...]).astype(BF16)
        acts = []
        for c in range(D_FF // D_MODEL):
            a = _dot(hb, w1_ref[:, c * D_MODEL:(c + 1) * D_MODEL])
            acts.append(jnp.square(jnp.maximum(a, 0.0)).astype(BF16))
        y = x + _dot(jnp.concatenate(acts, axis=1), w2_ref[...])
        if final_norm:
            y = _rmsnorm(y, fg_ref[...])
        o_ref[rows, :] = y


def _ffn_call(layer, x, g, w1, w2, final_g, final_norm, cast_jobs=()):
    T, D = x.shape
    tt = TOK_TILE
    tok = lambda i: (i, 0)
    casts, c_ops, c_in, c_shapes, c_out = _cast_plumbing(cast_jobs, T // tt, lambda i: i)
    return pl.pallas_call(
        functools.partial(_ffn_kernel, layer=layer, final_norm=final_norm, casts=casts),
        grid=(T // tt,),
        in_specs=[pl.BlockSpec((tt, D), tok), _const_spec(g.shape), _const_spec(w1.shape),
                  _const_spec(w2.shape), _const_spec((1, D))] + c_in,
        out_specs=(pl.BlockSpec((tt, D), tok),) + tuple(c_out),
        out_shape=(jax.ShapeDtypeStruct((T, D), F32),) + tuple(c_shapes),
        compiler_params=pltpu.CompilerParams(
            dimension_semantics=("arbitrary",), vmem_limit_bytes=VMEM_LIMIT),
        name="ffn",
    )(x, g, w1, w2, final_g, *c_ops)


def _t5_bucket_np(dist):
    max_exact = N_BUCKETS // 2
    d_f = np.maximum(dist, 1).astype(np.float32)
    ratio = np.log(d_f / np.float32(max_exact)) / np.float32(math.log(MAX_DISTANCE / max_exact))
    large = max_exact + (ratio * np.float32(N_BUCKETS - max_exact)).astype(np.int32)
    large = np.minimum(large, N_BUCKETS - 1)
    return np.where(dist < max_exact, dist, large)


def _bucket_indices():
    qi = np.arange(CHUNK)[:, None]
    ki = np.arange(CHUNK)[None, :]
    d_cur = qi - ki
    d_prev = qi + CHUNK - ki
    cur = [np.where(d_cur >= 0, _t5_bucket_np(np.clip(d_cur, 0, None) * dil), -1) for dil in DILATIONS]
    prev = [np.where(d_prev <= CHUNK, _t5_bucket_np(d_prev * dil), -1) for dil in DILATIONS[:2]]
    return np.stack(cur).astype(np.int32), np.stack(prev).astype(np.int32)


def _bias_kernel(tab_ref, icur_ref, iprev_ref, bcur_ref, bprev_ref):
    for idx_ref, out_ref in ((icur_ref, bcur_ref), (iprev_ref, bprev_ref)):
        for g in range(idx_ref.shape[0]):
            idx = idx_ref[g]
            accs = [jnp.full((CHUNK, CHUNK), NEG_BIG, F32) for _ in range(HEADS_PER_GROUP)]
            for b in range(N_BUCKETS):
                hit = idx == b
                for h in range(HEADS_PER_GROUP):
                    accs[h] = jnp.where(hit, tab_ref[b, g * HEADS_PER_GROUP + h], accs[h])
            for h in range(HEADS_PER_GROUP):
                out_ref[g, h] = accs[h]


def _bias_tables(rel_bias):
    icur, iprev = _bucket_indices()
    vm = pl.BlockSpec(memory_space=pltpu.VMEM)
    return pl.pallas_call(
        _bias_kernel,
        in_specs=[pl.BlockSpec(memory_space=pltpu.SMEM), vm, vm],
        out_specs=(vm, vm),
        out_shape=(jax.ShapeDtypeStruct((3, HEADS_PER_GROUP, CHUNK, CHUNK), F32),
                   jax.ShapeDtypeStruct((2, HEADS_PER_GROUP, CHUNK, CHUNK), F32)),
        name="bias",
    )(rel_bias, jnp.asarray(icur), jnp.asarray(iprev))


def kernel(x, norm_mix_g, w_in, a_ln_g, a_ln_b, a_ws, a_bs, w_a_out, b_conv, w_b_out, rel_bias,
           w_c_out, d_w, d_scale, w_d_out, w_o, norm_ff_g, w_ff1, w_ff2, final_g):
    B, S, D = x.shape
    depth = w_in.shape[0]
    bcur, bprev = _bias_tables(rel_bias)
    w_outs = (w_a_out, w_b_out, w_c_out, w_d_out)
    w_mix_b = jnp.concatenate([w_in[0][:, a:b] for a, b in _MIX_COLS], axis=1).astype(BF16)
    bs_full = jnp.repeat(jnp.swapaxes(a_bs, 1, 2), W_BRANCH // A_GROUPS, axis=2)
    eye = jnp.eye(POOL_GROUPS, dtype=d_w.dtype)
    wpool_bd = jnp.einsum("lgde,gh->lgdhe", d_w, eye).reshape(depth, W_BRANCH, W_BRANCH).astype(BF16)
    final_row = final_g[None]
    T = B * S
    for l in range(depth):
        more = l + 1 < depth

        mix_jobs = (("plain", 0, (w_o,)), ("w_out", 0, w_outs)) if l == 0 else ()
        pa, pb, pd, q0, q1, q2, *cast = _mix_call(
            l, x, norm_mix_g, w_mix_b, a_ln_g, a_ln_b, a_ws, bs_full, b_conv, wpool_bd, d_scale,
            cast_jobs=mix_jobs)
        if l == 0:
            w_o_b, w_out_b = cast

        attn_jobs = (("plain", 0, (w_ff1,)), ("plain", 0, (w_ff2,)), ("gate", 0, (w_in,))) if l == 0 else ()
        pc, *cast = _attn_call((q0, q1, q2), bcur, bprev, cast_jobs=attn_jobs)
        if l == 0:
            w_ff1_b, w_ff2_b, w_gate_b = cast

        merge_jobs = (("w_in", l + 1, (w_in,)),) if more else ()
        x2, *cast = _merge_call(l, x.reshape(T, D), pa.reshape(T, -1), pb.reshape(T, -1),
                                pc.reshape(T, -1), pd.reshape(T, -1), norm_mix_g,
                                w_gate_b, w_out_b, w_o_b, cast_jobs=merge_jobs)
        if more:
            w_mix_b, w_gate_b = cast

        ffn_jobs = (("plain", l + 1, (w_ff1,)), ("plain", l + 1, (w_ff2,)), ("plain", l + 1, (w_o,)),
                    ("w_out", l + 1, w_outs)) if more else ()
        x2, *cast = _ffn_call(l, x2, norm_ff_g, w_ff1_b, w_ff2_b, final_row,
                              final_norm=not more, cast_jobs=ffn_jobs)
        if more:
            w_ff1_b, w_ff2_b, w_o_b, w_out_b = cast
        x = x2.reshape(B, S, D)
    return x
```

```python
import functools
import math

import jax
import jax.numpy as jnp
import numpy as np
from jax import lax
from jax.experimental import pallas as pl
from jax.experimental.pallas import tpu as pltpu

F32 = jnp.float32
BF16 = jnp.bfloat16

D_MODEL = 1024
W_BRANCH = 256
A_GROUPS = 4
CHUNK = 128
CONV_WIDTH = 3
DILATIONS = (1, 4, 16)
WINDOWS = (128, 512, 2048)
HEADS_PER_GROUP = 4
HEAD_DIM = 64
POOL_WINDOWS = (2, 4, 8, 16)
POOL_GROUPS = len(POOL_WINDOWS)
POOL_HALO = 16
CONV_HALO = 8
N_BRANCH = 4
D_FF = 4 * D_MODEL
N_BUCKETS = 32
MAX_DISTANCE = 2048
EPS = 1e-6
NEG_BIG = -1e30
COL_Q = 5 * W_BRANCH
COL_K = COL_Q + 3 * W_BRANCH
COL_V = COL_K + 3 * W_BRANCH
COL_D = COL_V + 3 * W_BRANCH
COL_GATE = COL_D + W_BRANCH
W_LOCAL = 6 * W_BRANCH
W_QKV = 3 * W_BRANCH
W_QKV_OUT = 4 * W_BRANCH

MIX_TILE = 1024
MIX_SUB = 512
TOK_TILE = 1024
SUB_TILE = 512
ATTN_BLOCKS_PER_PIECE = 8
VMEM_LIMIT = 56 * 1024 * 1024


def _rmsnorm(x, g):
    return x * lax.rsqrt(jnp.mean(x * x, axis=-1, keepdims=True) + EPS) * g


def _gelu_tanh(x):
    c = math.sqrt(2.0 / math.pi)
    return x * (0.5 * (1.0 + jnp.tanh(c * (x + 0.044715 * (x * x * x)))))


def _dot(a, b):
    return jnp.dot(a, b, preferred_element_type=F32)


def _const_spec(shape):
    n = len(shape)
    return pl.BlockSpec(shape, lambda *_: (0,) * n, pipeline_mode=pl.Buffered(1))


_CAST_ARITY = {"plain": (1, 1), "w_in": (1, 2), "gate": (1, 1), "w_out": (N_BRANCH, 1)}
_MIX_COLS = ((0, COL_Q), (COL_D, COL_GATE)) + tuple(
    (col + g * W_BRANCH, col + (g + 1) * W_BRANCH)
    for g in range(len(DILATIONS)) for col in (COL_Q, COL_K, COL_V))


def _cast_plumbing(jobs, n_steps, step_of):
    kinds, operands, in_specs, out_shapes, out_specs = [], [], [], [], []
    for kind, layer, srcs in jobs:
        kinds.append(kind)
        R, C = srcs[0].shape[1:]
        rows = R // n_steps
        assert rows * n_steps == R and rows % 16 == 0, (kind, R, n_steps)
        for src in srcs:
            operands.append(src)
            in_specs.append(pl.BlockSpec((1, rows, C), lambda *g, layer=layer: (layer, step_of(*g), 0)))
        row_blk = lambda *g: (step_of(*g), 0)
        if kind == "w_out":
            out_shapes.append(jax.ShapeDtypeStruct((len(srcs), R, C), BF16))
            out_specs.append(pl.BlockSpec((len(srcs), rows, C), lambda *g: (0, step_of(*g), 0)))
        elif kind in ("w_in", "gate"):
            widths = (C - COL_GATE,) if kind == "gate" else (sum(b - a for a, b in _MIX_COLS), C - COL_GATE)
            for width in widths:
                out_shapes.append(jax.ShapeDtypeStruct((R, width), BF16))
                out_specs.append(pl.BlockSpec((rows, width), row_blk))
        else:
            out_shapes.append(jax.ShapeDtypeStruct((R, C), BF16))
            out_specs.append(pl.BlockSpec((rows, C), row_blk))
    return tuple(kinds), operands, in_specs, out_shapes, out_specs


def _split_refs(refs, n_in, n_out, kinds):
    n_ci = sum(_CAST_ARITY[k][0] for k in kinds)
    n_co = sum(_CAST_ARITY[k][1] for k in kinds)
    a, b, c = n_in, n_in + n_ci, n_in + n_ci + n_out
    return refs[:a], refs[a:b], refs[b:c], refs[c:c + n_co], refs[c + n_co:]


def _run_casts(kinds, src_refs, dst_refs):
    i = o = 0
    for kind in kinds:
        n_i, n_o = _CAST_ARITY[kind]
        srcs, dsts = src_refs[i:i + n_i], dst_refs[o:o + n_o]
        i, o = i + n_i, o + n_o
        if kind == "plain":
            dsts[0][...] = srcs[0][0].astype(BF16)
        elif kind == "w_out":
            for k, src in enumerate(srcs):
                dsts[0][k] = src[0].astype(BF16)
        elif kind == "gate":
            dsts[0][...] = srcs[0][0, :, COL_GATE:].astype(BF16)
        else:
            c = 0
            for a, b in _MIX_COLS:
                dsts[0][:, c:c + b - a] = srcs[0][0, :, a:b].astype(BF16)
                c += b - a
            dsts[1][...] = srcs[0][0, :, COL_GATE:].astype(BF16)


def _layer_row(ref, layer):
    return ref.at[pl.ds(layer, 1)]


def _mix_kernel(*refs, layer, casts):
    ins, cast_src, outs, cast_dst, (h_scr, bbuf, dbuf) = _split_refs(refs, 10, 6, casts)
    x_ref, g_ref, w_ref, lng_ref, lnb_ref, ws_ref, bs_ref, conv_ref, wpool_ref, dscale_ref = ins
    g_ref, lng_ref, lnb_ref, dscale_ref = (_layer_row(r, layer) for r in (g_ref, lng_ref, lnb_ref, dscale_ref))
    ws_ref, bs_ref, conv_ref, wpool_ref = (r.at[layer] for r in (ws_ref, bs_ref, conv_ref, wpool_ref))
    pa_ref, pb_ref, pd_ref, q0_ref, q1_ref, q2_ref = outs
    _run_casts(casts, cast_src, cast_dst)
    n_sub = MIX_TILE // MIX_SUB
    first = (pl.program_id(0) == 0) & (pl.program_id(1) == 0)

    @pl.when(first)
    def _():
        bbuf[...] = jnp.zeros(bbuf.shape, F32)
        dbuf[...] = jnp.zeros(dbuf.shape, F32)

    for s in range(n_sub):
        _mix_subtile(s, pl.program_id(1) * n_sub + s, x_ref, g_ref, w_ref, lng_ref, lnb_ref, ws_ref,
                     bs_ref, conv_ref, wpool_ref, dscale_ref, pa_ref, pb_ref, pd_ref,
                     (q0_ref, q1_ref, q2_ref), h_scr.at[s], bbuf.at[s], bbuf.at[(s - 1) % n_sub],
                     dbuf.at[s], dbuf.at[(s - 1) % n_sub])


def _mix_subtile(s, i, x_ref, g_ref, w_ref, lng_ref, lnb_ref, ws_ref, bs_ref, conv_ref, wpool_ref,
                 dscale_ref, pa_ref, pb_ref, pd_ref, q_refs, h_scr, bbuf, bbuf_prev, dbuf, dbuf_prev):
    ts = MIX_SUB
    rows = pl.ds(s * ts, ts)
    x = x_ref[0, rows, :]
    h = _rmsnorm(x, g_ref[...])
    for c in range(D_MODEL // 128):
        h_scr[c] = h[:, c * 128:(c + 1) * 128]
    hb = h.astype(BF16)
    zl = _dot(hb, w_ref[:, 0:W_LOCAL + W_QKV])

    u = _gelu_tanh(zl[:, 0:256])
    v = _gelu_tanh(zl[:, 256:512])
    mu = jnp.mean(v, axis=-1, keepdims=True)
    vc = v - mu
    var = jnp.mean(vc * vc, axis=-1, keepdims=True)
    v = vc * lax.rsqrt(var + EPS) * lng_ref[...] + lnb_ref[...]
    row = lax.broadcasted_iota(jnp.int32, (CHUNK, A_GROUPS * CHUNK), 0)
    col = lax.broadcasted_iota(jnp.int32, (CHUNK, A_GROUPS * CHUNK), 1) % CHUNK
    w_all = jnp.concatenate([ws_ref[g] for g in range(A_GROUPS)], axis=1)
    wtril = jnp.where(row >= col, w_all, 0.0).astype(BF16)
    grp = lax.broadcasted_iota(jnp.int32, (CHUNK, W_BRANCH), 1) // (W_BRANCH // A_GROUPS)
    svs = []
    for c in range(ts // CHUNK):
        vch = v[c * CHUNK:(c + 1) * CHUNK]
        stacked = jnp.concatenate(
            [jnp.where(grp == g, vch, 0.0) for g in range(A_GROUPS)], axis=0).astype(BF16)
        svs.append(_dot(wtril, stacked) + bs_ref[...])
    sv = jnp.concatenate(svs, axis=0)
    pa_ref[0, rows, :] = (u * sv).astype(BF16)

    prod = zl[:, 768:1024] * zl[:, 1024:1280]

    bbuf[0:CONV_HALO] = jnp.where(i > 0, bbuf_prev[ts:ts + CONV_HALO], 0.0)
    dbuf[0:POOL_HALO] = jnp.where(i > 0, dbuf_prev[ts:ts + POOL_HALO], 0.0)
    bbuf[CONV_HALO:ts + CONV_HALO] = prod
    cw = conv_ref[...]
    conv = (cw[0:1] * bbuf[pl.ds(CONV_HALO - 2, ts), :]
            + cw[1:2] * bbuf[pl.ds(CONV_HALO - 1, ts), :]
            + cw[2:3] * prod)
    pb_ref[0, rows, :] = (zl[:, 512:768] * conv).astype(BF16)

    dz = zl[:, 1280:1536]
    dbuf[POOL_HALO:ts + POOL_HALO] = dz
    lane = lax.broadcasted_iota(jnp.int32, (ts, 128), 1)
    first_half = lane < 64
    tpos = (lax.broadcasted_iota(jnp.int32, (ts, 128), 0) + (i * ts + 1)).astype(F32)

    def window_sums(e, levels):
        out = []
        s = e
        for k in range(levels):
            s = s + pltpu.roll(s, 1 << k, 0)
            out.append(s[POOL_HALO:])
        return out

    lo = window_sums(dbuf[:, 0:128], 2)
    hi = window_sums(dbuf[:, 128:256], 4)
    pooled_lo = jnp.where(first_half, lo[0], lo[1]) / jnp.minimum(tpos, jnp.where(first_half, 2.0, 4.0))
    pooled_hi = jnp.where(first_half, hi[2], hi[3]) / jnp.minimum(tpos, jnp.where(first_half, 8.0, 16.0))
    y = jnp.concatenate([pooled_lo, pooled_hi], axis=1) - dz
    pd_ref[0, rows, :] = (_dot(y.astype(BF16), wpool_ref[...]) * dscale_ref[...]).astype(BF16)

    sub0 = lax.broadcasted_iota(jnp.int32, (ts, 128), 1) < HEAD_DIM
    for g, (q_ref, dil) in enumerate(zip(q_refs, DILATIONS)):
        n = ts // dil
        if dil == 1:
            z = zl[:, W_LOCAL:W_LOCAL + W_QKV]
        else:
            hp = jnp.concatenate(
                [jnp.concatenate([h_scr[c, pl.ds(r, n, stride=dil), :] for r in range(dil)], axis=0)
                 for c in range(D_MODEL // 128)], axis=1).astype(BF16)
            c0 = W_LOCAL + g * W_QKV
            z = _dot(hp, w_ref[:, c0:c0 + W_QKV])
        q = z[:, 0:256] * (HEAD_DIM ** -0.5)
        pieces = []
        for pr in range(2):
            qp = q[:, pr * 128:(pr + 1) * 128]
            pieces.append(jnp.where(sub0, qp, 0.0))
            pieces.append(jnp.where(sub0, 0.0, qp))
            pieces.append(z[:, 256 + pr * 128:256 + (pr + 1) * 128])
            pieces.append(z[:, 512 + pr * 128:512 + (pr + 1) * 128])
        val = jnp.concatenate(pieces, axis=1).astype(BF16)
        for r in range(dil):
            q_ref[0, r, pl.ds(s * n, n), :] = val[r * n:(r + 1) * n]


def _mix_call(layer, x, g, w_mix, lng, lnb, ws, bs_full, conv, wpool_bd, dscale, cast_jobs=()):
    B, S, D = x.shape
    ts = MIX_TILE
    nt = S // ts
    tok = lambda b, i: (b, i, 0)
    casts, c_ops, c_in, c_shapes, c_out = _cast_plumbing(cast_jobs, B * nt, lambda b, i: b * nt + i)
    out_shape = (
        jax.ShapeDtypeStruct((B, S, W_BRANCH), BF16),
        jax.ShapeDtypeStruct((B, S, W_BRANCH), BF16),
        jax.ShapeDtypeStruct((B, S, W_BRANCH), BF16),
    ) + tuple(jax.ShapeDtypeStruct((B, dil, S // dil, W_QKV_OUT), BF16) for dil in DILATIONS)
    in_specs = [
        pl.BlockSpec((1, ts, D), tok),
    ] + [_const_spec(a.shape) for a in (g, w_mix, lng, lnb, ws, bs_full, conv, wpool_bd, dscale)]
    out_specs = (
        pl.BlockSpec((1, ts, W_BRANCH), tok),
        pl.BlockSpec((1, ts, W_BRANCH), tok),
        pl.BlockSpec((1, ts, W_BRANCH), tok),
    ) + tuple(pl.BlockSpec((1, dil, ts // dil, W_QKV_OUT), lambda b, i: (b, 0, i, 0))
              for dil in DILATIONS)
    return pl.pallas_call(
        functools.partial(_mix_kernel, layer=layer, casts=casts),
        grid=(B, nt),
        in_specs=in_specs + c_in,
        out_specs=out_specs + tuple(c_out),
        out_shape=out_shape + tuple(c_shapes),
        scratch_shapes=[
            pltpu.VMEM((ts // MIX_SUB, D // 128, MIX_SUB, 128), F32),
            pltpu.VMEM((ts // MIX_SUB, MIX_SUB + CONV_HALO, W_BRANCH), F32),
            pltpu.VMEM((ts // MIX_SUB, MIX_SUB + POOL_HALO, W_BRANCH), F32),
        ],
        compiler_params=pltpu.CompilerParams(
            dimension_semantics=("arbitrary", "arbitrary"), vmem_limit_bytes=VMEM_LIMIT),
        name="mix",
    )(x, g, w_mix, lng, lnb, ws, bs_full, conv, wpool_bd, dscale, *c_ops)


def _attn_kernel(*refs, casts):
    ins, cast_src, (out_ref,), cast_dst, (o_scr, l_scr) = _split_refs(refs, 5, 1, casts)
    q0_ref, q1_ref, q2_ref, bcur_ref, bprev_ref = ins
    _run_casts(casts, cast_src, cast_dst)
    q_refs = (q0_ref, q1_ref, q2_ref)
    S = out_ref.shape[1]
    pr = pl.program_id(1)
    nb = S // CHUNK

    def cols(g, c0):
        return q_refs[g][0, :, c0:c0 + 128].reshape(nb, CHUNK, 128)

    jb = ATTN_BLOCKS_PER_PIECE
    sub0 = lax.broadcasted_iota(jnp.int32, (jb, CHUNK, 128), 2) < HEAD_DIM
    qk = lambda a, b: jnp.einsum("jqd,jkd->jqk", a, b, preferred_element_type=F32)
    pv = lambda a, b: jnp.einsum("jqk,jkd->jqd", a, b, preferred_element_type=F32)
    hh0 = 2 * pr

    for g, dil in enumerate(DILATIONS):
        blocks_per_seq = nb // dil
        has_prev = blocks_per_seq > 1
        q = jnp.concatenate([cols(g, 0), cols(g, 128)], axis=1)
        k = cols(g, 256)
        v = cols(g, 384)
        bias_cur = jnp.concatenate([bcur_ref[g, hh0], bcur_ref[g, hh0 + 1]], axis=0)[None]
        if has_prev:
            k = jnp.concatenate([jnp.concatenate([k[:1], k[:-1]], axis=0), k], axis=1)
            v = jnp.concatenate([jnp.concatenate([v[:1], v[:-1]], axis=0), v], axis=1)
            bias_prev = jnp.concatenate([bprev_ref[g, hh0], bprev_ref[g, hh0 + 1]], axis=0)
        for j0 in range(0, nb, jb):
            s = qk(q[j0:j0 + jb], k[j0:j0 + jb])
            s_cur = s[:, :, -CHUNK:] + bias_cur
            s_prev = [None if not has_prev or (j0 + jj) % blocks_per_seq == 0
                      else s[jj, :, :CHUNK] + bias_prev for jj in range(jb)]
            top = jnp.stack([s_cur[jj] if s_prev[jj] is None else jnp.maximum(s_cur[jj], s_prev[jj])
                             for jj in range(jb)])
            row_max = jnp.max(top, axis=-1, keepdims=True)
            e = jnp.exp(s_cur - row_max)
            den = e
            if has_prev:
                e_prev = jnp.stack([jnp.zeros((2 * CHUNK, CHUNK), F32) if s_prev[jj] is None
                                    else jnp.exp(s_prev[jj] - row_max[jj]) for jj in range(jb)])
                den = e + e_prev
                e = jnp.concatenate([e_prev, e], axis=-1)
            den = jnp.sum(den, axis=-1, keepdims=True)
            acc = pv(e.astype(BF16), v[j0:j0 + jb])
            o2 = acc / den
            l2 = jnp.broadcast_to(row_max + jnp.log(den), o2.shape)
            o = jnp.where(sub0, o2[:, :CHUNK], o2[:, CHUNK:])
            lse = jnp.where(sub0, l2[:, :CHUNK], l2[:, CHUNK:])
            for jj in range(jb):
                r, blk = divmod(j0 + jj, blocks_per_seq)
                rows = (pl.ds(blk * CHUNK * dil + r, CHUNK, stride=dil) if dil > 1
                        else pl.ds((j0 + jj) * CHUNK, CHUNK))
                o_scr[g, rows, :] = o[jj]
                l_scr[g, rows, :] = lse[jj]

    rows = 256
    for c in range(S // rows):
        sl = pl.ds(c * rows, rows)
        l0, l1, l2 = l_scr[0, sl, :], l_scr[1, sl, :], l_scr[2, sl, :]
        m = jnp.maximum(jnp.maximum(l0, l1), l2)
        e0, e1, e2 = jnp.exp(l0 - m), jnp.exp(l1 - m), jnp.exp(l2 - m)
        num = e0 * o_scr[0, sl, :] + e1 * o_scr[1, sl, :] + e2 * o_scr[2, sl, :]
        out_ref[0, sl, :] = (num / (e0 + e1 + e2)).astype(BF16)


def _attn_call(qs, bcur, bprev, cast_jobs=()):
    B = qs[0].shape[0]
    S = qs[0].shape[1] * qs[0].shape[2]
    operands = [q.reshape(B, S, W_QKV_OUT) for q in qs]
    spec = pl.BlockSpec((1, S, W_QKV_OUT // 2), lambda b, p: (b, 0, p))
    casts, c_ops, c_in, c_shapes, c_out = _cast_plumbing(cast_jobs, B * 2, lambda b, p: b * 2 + p)
    return pl.pallas_call(
        functools.partial(_attn_kernel, casts=casts),
        grid=(B, 2),
        in_specs=[spec, spec, spec, _const_spec(bcur.shape), _const_spec(bprev.shape)] + c_in,
        out_specs=(pl.BlockSpec((1, S, 128), lambda b, p: (b, 0, p)),) + tuple(c_out),
        out_shape=(jax.ShapeDtypeStruct((B, S, W_BRANCH), BF16),) + tuple(c_shapes),
        scratch_shapes=[
            pltpu.VMEM((3, S, 128), F32),
            pltpu.VMEM((3, S, 128), F32),
        ],
        compiler_params=pltpu.CompilerParams(
            dimension_semantics=("arbitrary", "arbitrary"), vmem_limit_bytes=VMEM_LIMIT),
        name="attn",
    )(*operands, bcur, bprev, *c_ops)


def _merge_kernel(*refs, layer, casts):
    ins, cast_src, (o_ref,), cast_dst, _ = _split_refs(refs, 9, 1, casts)
    x_ref, pa_ref, pb_ref, pc_ref, pd_ref, g_ref, wg_ref, wout_ref, wo_ref = ins
    g_ref = _layer_row(g_ref, layer)
    _run_casts(casts, cast_src, cast_dst)
    for s in range(TOK_TILE // SUB_TILE):
        rows = pl.ds(s * SUB_TILE, SUB_TILE)
        x = x_ref[rows, :]
        hb = _rmsnorm(x, g_ref[...]).astype(BF16)
        merged = None
        gates = _dot(hb, wg_ref[...])
        for br, p_ref in enumerate((pa_ref, pb_ref, pc_ref, pd_ref)):
            gate = jax.nn.sigmoid(gates[:, br * D_MODEL:(br + 1) * D_MODEL])
            term = gate * _dot(p_ref[rows, :], wout_ref[br])
            merged = term if merged is None else merged + term
        o_ref[rows, :] = x + _dot(merged.astype(BF16), wo_ref[...])


def _merge_call(layer, x, pa, pb, pc, pd, g, w_gate, w_out, w_o, cast_jobs=()):
    T, D = x.shape
    tt = TOK_TILE
    tok = lambda i: (i, 0)
    pspec = pl.BlockSpec((tt, W_BRANCH), tok)
    casts, c_ops, c_in, c_shapes, c_out = _cast_plumbing(cast_jobs, T // tt, lambda i: i)
    return pl.pallas_call(
        functools.partial(_merge_kernel, layer=layer, casts=casts),
        grid=(T // tt,),
        in_specs=[pl.BlockSpec((tt, D), tok), pspec, pspec, pspec, pspec,
                  _const_spec(g.shape), _const_spec(w_gate.shape), _const_spec(w_out.shape),
                  _const_spec(w_o.shape)] + c_in,
        out_specs=(pl.BlockSpec((tt, D), tok),) + tuple(c_out),
        out_shape=(jax.ShapeDtypeStruct((T, D), F32),) + tuple(c_shapes),
        compiler_params=pltpu.CompilerParams(
            dimension_semantics=("arbitrary",), vmem_limit_bytes=VMEM_LIMIT),
        name="merge",
    )(x, pa, pb, pc, pd, g, w_gate, w_out, w_o, *c_ops)


def _ffn_kernel(*refs, layer, final_norm, casts):
    (x_ref, g_ref, w1_ref, w2_ref, fg_ref), cast_src, (o_ref,), cast_dst, _ = _split_refs(refs, 5, 1, casts)
    g_ref = _layer_row(g_ref, layer)
    _run_casts(casts, cast_src, cast_dst)
    for s in range(TOK_TILE // SUB_TILE):
        rows = pl.ds(s * SUB_TILE, SUB_TILE)
        x = x_ref[rows, :]
        hb = _rmsnorm(x, g_ref[...]).astype(BF16)
        a = _dot(hb, w1_ref[...])
        y = x + _dot(jnp.square(jnp.maximum(a, 0.0)).astype(BF16), w2_ref[...])
        if final_norm:
            y = _rmsnorm(y, fg_ref[...])
        o_ref[rows, :] = y


def _ffn_call(layer, x, g, w1, w2, final_g, final_norm, cast_jobs=()):
    T, D = x.shape
    tt = TOK_TILE
    tok = lambda i: (i, 0)
    casts, c_ops, c_in, c_shapes, c_out = _cast_plumbing(cast_jobs, T // tt, lambda i: i)
    return pl.pallas_call(
        functools.partial(_ffn_kernel, layer=layer, final_norm=final_norm, casts=casts),
        grid=(T // tt,),
        in_specs=[pl.BlockSpec((tt, D), tok), _const_spec(g.shape), _const_spec(w1.shape),
                  _const_spec(w2.shape), _const_spec((1, D))] + c_in,
        out_specs=(pl.BlockSpec((tt, D), tok),) + tuple(c_out),
        out_shape=(jax.ShapeDtypeStruct((T, D), F32),) + tuple(c_shapes),
        compiler_params=pltpu.CompilerParams(
            dimension_semantics=("arbitrary",), vmem_limit_bytes=VMEM_LIMIT),
        name="ffn",
    )(x, g, w1, w2, final_g, *c_ops)


def _t5_bucket_np(dist):
    max_exact = N_BUCKETS // 2
    d_f = np.maximum(dist, 1).astype(np.float32)
    ratio = np.log(d_f / np.float32(max_exact)) / np.float32(math.log(MAX_DISTANCE / max_exact))
    large = max_exact + (ratio * np.float32(N_BUCKETS - max_exact)).astype(np.int32)
    large = np.minimum(large, N_BUCKETS - 1)
    return np.where(dist < max_exact, dist, large)


def _bucket_indices():
    qi = np.arange(CHUNK)[:, None]
    ki = np.arange(CHUNK)[None, :]
    d_cur = qi - ki
    d_prev = qi + CHUNK - ki
    cur = [np.where(d_cur >= 0, _t5_bucket_np(np.clip(d_cur, 0, None) * dil), -1) for dil in DILATIONS]
    prev = [np.where(d_prev <= CHUNK, _t5_bucket_np(d_prev * dil), -1) for dil in DILATIONS[:2]]
    return np.stack(cur).astype(np.int32), np.stack(prev).astype(np.int32)


def _bias_kernel(tab_ref, icur_ref, iprev_ref, bcur_ref, bprev_ref):
    for idx_ref, out_ref in ((icur_ref, bcur_ref), (iprev_ref, bprev_ref)):
        for g in range(idx_ref.shape[0]):
            idx = idx_ref[g]
            accs = [jnp.full((CHUNK, CHUNK), NEG_BIG, F32) for _ in range(HEADS_PER_GROUP)]
            for b in range(N_BUCKETS):
                hit = idx == b
                for h in range(HEADS_PER_GROUP):
                    accs[h] = jnp.where(hit, tab_ref[b, g * HEADS_PER_GROUP + h], accs[h])
            for h in range(HEADS_PER_GROUP):
                out_ref[g, h] = accs[h]


def _bias_tables(rel_bias):
    icur, iprev = _bucket_indices()
    vm = pl.BlockSpec(memory_space=pltpu.VMEM)
    return pl.pallas_call(
        _bias_kernel,
        in_specs=[pl.BlockSpec(memory_space=pltpu.SMEM), vm, vm],
        out_specs=(vm, vm),
        out_shape=(jax.ShapeDtypeStruct((3, HEADS_PER_GROUP, CHUNK, CHUNK), F32),
                   jax.ShapeDtypeStruct((2, HEADS_PER_GROUP, CHUNK, CHUNK), F32)),
        name="bias",
    )(rel_bias, jnp.asarray(icur), jnp.asarray(iprev))


def kernel(x, norm_mix_g, w_in, a_ln_g, a_ln_b, a_ws, a_bs, w_a_out, b_conv, w_b_out, rel_bias,
           w_c_out, d_w, d_scale, w_d_out, w_o, norm_ff_g, w_ff1, w_ff2, final_g):
    B, S, D = x.shape
    depth = w_in.shape[0]
    bcur, bprev = _bias_tables(rel_bias)
    w_outs = (w_a_out, w_b_out, w_c_out, w_d_out)
    w_mix_b = jnp.concatenate([w_in[0][:, a:b] for a, b in _MIX_COLS], axis=1).astype(BF16)
    bs_full = jnp.repeat(jnp.swapaxes(a_bs, 1, 2), W_BRANCH // A_GROUPS, axis=2)
    eye = jnp.eye(POOL_GROUPS, dtype=d_w.dtype)
    wpool_bd = jnp.einsum("lgde,gh->lgdhe", d_w, eye).reshape(depth, W_BRANCH, W_BRANCH).astype(BF16)
    final_row = final_g[None]
    T = B * S
    for l in range(depth):
        more = l + 1 < depth

        mix_jobs = (("plain", 0, (w_o,)), ("w_out", 0, w_outs)) if l == 0 else ()
        pa, pb, pd, q0, q1, q2, *cast = _mix_call(
            l, x, norm_mix_g, w_mix_b, a_ln_g, a_ln_b, a_ws, bs_full, b_conv, wpool_bd, d_scale,
            cast_jobs=mix_jobs)
        if l == 0:
            w_o_b, w_out_b = cast

        attn_jobs = (("plain", 0, (w_ff1,)), ("plain", 0, (w_ff2,)), ("gate", 0, (w_in,))) if l == 0 else ()
        pc, *cast = _attn_call((q0, q1, q2), bcur, bprev, cast_jobs=attn_jobs)
        if l == 0:
            w_ff1_b, w_ff2_b, w_gate_b = cast

        merge_jobs = (("w_in", l + 1, (w_in,)),) if more else ()
        x2, *cast = _merge_call(l, x.reshape(T, D), pa.reshape(T, -1), pb.reshape(T, -1),
                                pc.reshape(T, -1), pd.reshape(T, -1), norm_mix_g,
                                w_gate_b, w_out_b, w_o_b, cast_jobs=merge_jobs)
        if more:
            w_mix_b, w_gate_b = cast

        ffn_jobs = (("plain", l + 1, (w_ff1,)), ("plain", l + 1, (w_ff2,)), ("plain", l + 1, (w_o,)),
                    ("w_out", l + 1, w_outs)) if more else ()
        x2, *cast = _ffn_call(l, x2, norm_ff_g, w_ff1_b, w_ff2_b, final_row,
                              final_norm=not more, cast_jobs=ffn_jobs)
        if more:
            w_ff1_b, w_ff2_b, w_o_b, w_out_b = cast
        x = x2.reshape(B, S, D)
    return x
```

```python
import functools
import math

import jax
import jax.numpy as jnp
import numpy as np
from jax import lax
from jax.experimental import pallas as pl
from jax.experimental.pallas import tpu as pltpu

F32 = jnp.float32
BF16 = jnp.bfloat16

D_MODEL = 1024
W_BRANCH = 256
A_GROUPS = 4
CHUNK = 128
CONV_WIDTH = 3
DILATIONS = (1, 4, 16)
WINDOWS = (128, 512, 2048)
HEADS_PER_GROUP = 4
HEAD_DIM = 64
POOL_WINDOWS = (2, 4, 8, 16)
POOL_GROUPS = len(POOL_WINDOWS)
POOL_HALO = 16
CONV_HALO = 8
N_BRANCH = 4
D_FF = 4 * D_MODEL
N_BUCKETS = 32
MAX_DISTANCE = 2048
EPS = 1e-6
NEG_BIG = -1e30
COL_Q = 5 * W_BRANCH
COL_K = COL_Q + 3 * W_BRANCH
COL_V = COL_K + 3 * W_BRANCH
COL_D = COL_V + 3 * W_BRANCH
COL_GATE = COL_D + W_BRANCH
W_LOCAL = 6 * W_BRANCH
W_QKV = 3 * W_BRANCH
W_QKV_OUT = 4 * W_BRANCH

MIX_TILE = 1024
MIX_SUB = 512
TOK_TILE = 1024
SUB_TILE = 512
ATTN_BLOCKS_PER_PIECE = 8
VMEM_LIMIT = 56 * 1024 * 1024


def _rmsnorm(x, g):
    return x * lax.rsqrt(jnp.mean(x * x, axis=-1, keepdims=True) + EPS) * g


def _gelu_tanh(x):
    c = math.sqrt(2.0 / math.pi)
    return x * (0.5 * (1.0 + jnp.tanh(c * (x + 0.044715 * (x * x * x)))))


def _dot(a, b):
    return jnp.dot(a, b, preferred_element_type=F32)


def _const_spec(shape):
    n = len(shape)
    return pl.BlockSpec(shape, lambda *_: (0,) * n, pipeline_mode=pl.Buffered(1))


_CAST_ARITY = {"plain": (1, 1), "w_in": (1, 2), "gate": (1, 1), "w_out": (N_BRANCH, 1)}
_MIX_COLS = ((0, COL_Q), (COL_D, COL_GATE)) + tuple(
    (col + g * W_BRANCH, col + (g + 1) * W_BRANCH)
    for g in range(len(DILATIONS)) for col in (COL_Q, COL_K, COL_V))


def _cast_plumbing(jobs, n_steps, step_of):
    kinds, operands, in_specs, out_shapes, out_specs = [], [], [], [], []
    for kind, layer, srcs in jobs:
        kinds.append(kind)
        R, C = srcs[0].shape[1:]
        rows = R // n_steps
        assert rows * n_steps == R and rows % 16 == 0, (kind, R, n_steps)
        for src in srcs:
            operands.append(src)
            in_specs.append(pl.BlockSpec((1, rows, C), lambda *g, layer=layer: (layer, step_of(*g), 0)))
        row_blk = lambda *g: (step_of(*g), 0)
        if kind == "w_out":
            out_shapes.append(jax.ShapeDtypeStruct((len(srcs), R, C), BF16))
            out_specs.append(pl.BlockSpec((len(srcs), rows, C), lambda *g: (0, step_of(*g), 0)))
        elif kind in ("w_in", "gate"):
            widths = (C - COL_GATE,) if kind == "gate" else (sum(b - a for a, b in _MIX_COLS), C - COL_GATE)
            for width in widths:
                out_shapes.append(jax.ShapeDtypeStruct((R, width), BF16))
                out_specs.append(pl.BlockSpec((rows, width), row_blk))
        else:
            out_shapes.append(jax.ShapeDtypeStruct((R, C), BF16))
            out_specs.append(pl.BlockSpec((rows, C), row_blk))
    return tuple(kinds), operands, in_specs, out_shapes, out_specs


def _split_refs(refs, n_in, n_out, kinds):
    n_ci = sum(_CAST_ARITY[k][0] for k in kinds)
    n_co = sum(_CAST_ARITY[k][1] for k in kinds)
    a, b, c = n_in, n_in + n_ci, n_in + n_ci + n_out
    return refs[:a], refs[a:b], refs[b:c], refs[c:c + n_co], refs[c + n_co:]


def _run_casts(kinds, src_refs, dst_refs):
    i = o = 0
    for kind in kinds:
        n_i, n_o = _CAST_ARITY[kind]
        srcs, dsts = src_refs[i:i + n_i], dst_refs[o:o + n_o]
        i, o = i + n_i, o + n_o
        if kind == "plain":
            dsts[0][...] = srcs[0][0].astype(BF16)
        elif kind == "w_out":
            for k, src in enumerate(srcs):
                dsts[0][k] = src[0].astype(BF16)
        elif kind == "gate":
            dsts[0][...] = srcs[0][0, :, COL_GATE:].astype(BF16)
        else:
            c = 0
            for a, b in _MIX_COLS:
                dsts[0][:, c:c + b - a] = srcs[0][0, :, a:b].astype(BF16)
                c += b - a
            dsts[1][...] = srcs[0][0, :, COL_GATE:].astype(BF16)


def _layer_row(ref, layer):
    return ref.at[pl.ds(layer, 1)]


def _mix_kernel(*refs, layer, casts):
    ins, cast_src, outs, cast_dst, (h_scr, bbuf, dbuf) = _split_refs(refs, 10, 6, casts)
    x_ref, g_ref, w_ref, lng_ref, lnb_ref, ws_ref, bs_ref, conv_ref, wpool_ref, dscale_ref = ins
    g_ref, lng_ref, lnb_ref, dscale_ref = (_layer_row(r, layer) for r in (g_ref, lng_ref, lnb_ref, dscale_ref))
    ws_ref, bs_ref, conv_ref, wpool_ref = (r.at[layer] for r in (ws_ref, bs_ref, conv_ref, wpool_ref))
    pa_ref, pb_ref, pd_ref, q0_ref, q1_ref, q2_ref = outs
    _run_casts(casts, cast_src, cast_dst)
    n_sub = MIX_TILE // MIX_SUB
    first = (pl.program_id(0) == 0) & (pl.program_id(1) == 0)

    @pl.when(first)
    def _():
        bbuf[...] = jnp.zeros(bbuf.shape, F32)
        dbuf[...] = jnp.zeros(dbuf.shape, F32)

    for s in range(n_sub):
        _mix_subtile(s, pl.program_id(1) * n_sub + s, x_ref, g_ref, w_ref, lng_ref, lnb_ref, ws_ref,
                     bs_ref, conv_ref, wpool_ref, dscale_ref, pa_ref, pb_ref, pd_ref,
                     (q0_ref, q1_ref, q2_ref), h_scr.at[s], bbuf.at[s], bbuf.at[(s - 1) % n_sub],
                     dbuf.at[s], dbuf.at[(s - 1) % n_sub])


def _mix_subtile(s, i, x_ref, g_ref, w_ref, lng_ref, lnb_ref, ws_ref, bs_ref, conv_ref, wpool_ref,
                 dscale_ref, pa_ref, pb_ref, pd_ref, q_refs, h_scr, bbuf, bbuf_prev, dbuf, dbuf_prev):
    ts = MIX_SUB
    rows = pl.ds(s * ts, ts)
    x = x_ref[0, rows, :]
    h = _rmsnorm(x, g_ref[...])
    for c in range(D_MODEL // 128):
        h_scr[0, c] = h[:, c * 128:(c + 1) * 128]
    hb = h.astype(BF16)
    zl = _dot(hb, w_ref[:, 0:W_LOCAL + W_QKV])

    u = _gelu_tanh(zl[:, 0:256])
    v = _gelu_tanh(zl[:, 256:512])
    mu = jnp.mean(v, axis=-1, keepdims=True)
    vc = v - mu
    var = jnp.mean(vc * vc, axis=-1, keepdims=True)
    v = vc * lax.rsqrt(var + EPS) * lng_ref[...] + lnb_ref[...]
    row = lax.broadcasted_iota(jnp.int32, (CHUNK, A_GROUPS * CHUNK), 0)
    col = lax.broadcasted_iota(jnp.int32, (CHUNK, A_GROUPS * CHUNK), 1) % CHUNK
    w_all = jnp.concatenate([ws_ref[g] for g in range(A_GROUPS)], axis=1)
    wtril = jnp.where(row >= col, w_all, 0.0).astype(BF16)
    grp = lax.broadcasted_iota(jnp.int32, (CHUNK, W_BRANCH), 1) // (W_BRANCH // A_GROUPS)
    svs = []
    for c in range(ts // CHUNK):
        vch = v[c * CHUNK:(c + 1) * CHUNK]
        stacked = jnp.concatenate(
            [jnp.where(grp == g, vch, 0.0) for g in range(A_GROUPS)], axis=0).astype(BF16)
        svs.append(_dot(wtril, stacked) + bs_ref[...])
    sv = jnp.concatenate(svs, axis=0)
    pa_ref[0, rows, :] = (u * sv).astype(BF16)

    prod = zl[:, 768:1024] * zl[:, 1024:1280]

    bbuf[0:CONV_HALO] = jnp.where(i > 0, bbuf_prev[ts:ts + CONV_HALO], 0.0)
    dbuf[0:POOL_HALO] = jnp.where(i > 0, dbuf_prev[ts:ts + POOL_HALO], 0.0)
    bbuf[CONV_HALO:ts + CONV_HALO] = prod
    cw = conv_ref[...]
    conv = (cw[0:1] * bbuf[pl.ds(CONV_HALO - 2, ts), :]
            + cw[1:2] * bbuf[pl.ds(CONV_HALO - 1, ts), :]
            + cw[2:3] * prod)
    pb_ref[0, rows, :] = (zl[:, 512:768] * conv).astype(BF16)

    dz = zl[:, 1280:1536]
    dbuf[POOL_HALO:ts + POOL_HALO] = dz
    lane = lax.broadcasted_iota(jnp.int32, (ts, 128), 1)
    first_half = lane < 64
    tpos = (lax.broadcasted_iota(jnp.int32, (ts, 128), 0) + (i * ts + 1)).astype(F32)

    def window_sums(e, levels):
        out = []
        s = e
        for k in range(levels):
            s = s + pltpu.roll(s, 1 << k, 0)
            out.append(s[POOL_HALO:])
        return out

    lo = window_sums(dbuf[:, 0:128], 2)
    hi = window_sums(dbuf[:, 128:256], 4)
    pooled_lo = jnp.where(first_half, lo[0], lo[1]) / jnp.minimum(tpos, jnp.where(first_half, 2.0, 4.0))
    pooled_hi = jnp.where(first_half, hi[2], hi[3]) / jnp.minimum(tpos, jnp.where(first_half, 8.0, 16.0))
    y = jnp.concatenate([pooled_lo, pooled_hi], axis=1) - dz
    pd_ref[0, rows, :] = (_dot(y.astype(BF16), wpool_ref[...]) * dscale_ref[...]).astype(BF16)

    sub0 = lax.broadcasted_iota(jnp.int32, (ts, 128), 1) < HEAD_DIM
    for g, (q_ref, dil) in enumerate(zip(q_refs, DILATIONS)):
        n = ts // dil
        if dil == 1:
            z = zl[:, W_LOCAL:W_LOCAL + W_QKV]
        else:
            prev_dil = DILATIONS[g - 1]
            src, n_prev, step = h_scr.at[g - 1], ts // prev_dil, dil // prev_dil
            slabs = [jnp.concatenate([src[c, pl.ds((r % prev_dil) * n_prev + r // prev_dil, n, stride=step), :]
                                      for r in range(dil)], axis=0) for c in range(D_MODEL // 128)]
            if g + 1 < len(DILATIONS):
                for c, slab in enumerate(slabs):
                    h_scr[g, c] = slab
            hp = jnp.concatenate(slabs, axis=1).astype(BF16)
            c0 = W_LOCAL + g * W_QKV
            z = _dot(hp, w_ref[:, c0:c0 + W_QKV])
        q = z[:, 0:256] * (HEAD_DIM ** -0.5)
        pieces = []
        for pr in range(2):
            qp = q[:, pr * 128:(pr + 1) * 128]
            pieces.append(jnp.where(sub0, qp, 0.0))
            pieces.append(jnp.where(sub0, 0.0, qp))
            pieces.append(z[:, 256 + pr * 128:256 + (pr + 1) * 128])
            pieces.append(z[:, 512 + pr * 128:512 + (pr + 1) * 128])
        val = jnp.concatenate(pieces, axis=1).astype(BF16)
        for r in range(dil):
            q_ref[0, r, pl.ds(s * n, n), :] = val[r * n:(r + 1) * n]


def _mix_call(layer, x, g, w_mix, lng, lnb, ws, bs_full, conv, wpool_bd, dscale, cast_jobs=()):
    B, S, D = x.shape
    ts = MIX_TILE
    nt = S // ts
    tok = lambda b, i: (b, i, 0)
    casts, c_ops, c_in, c_shapes, c_out = _cast_plumbing(cast_jobs, B * nt, lambda b, i: b * nt + i)
    out_shape = (
        jax.ShapeDtypeStruct((B, S, W_BRANCH), BF16),
        jax.ShapeDtypeStruct((B, S, W_BRANCH), BF16),
        jax.ShapeDtypeStruct((B, S, W_BRANCH), BF16),
    ) + tuple(jax.ShapeDtypeStruct((B, dil, S // dil, W_QKV_OUT), BF16) for dil in DILATIONS)
    in_specs = [
        pl.BlockSpec((1, ts, D), tok),
    ] + [_const_spec(a.shape) for a in (g, w_mix, lng, lnb, ws, bs_full, conv, wpool_bd, dscale)]
    out_specs = (
        pl.BlockSpec((1, ts, W_BRANCH), tok),
        pl.BlockSpec((1, ts, W_BRANCH), tok),
        pl.BlockSpec((1, ts, W_BRANCH), tok),
    ) + tuple(pl.BlockSpec((1, dil, ts // dil, W_QKV_OUT), lambda b, i: (b, 0, i, 0))
              for dil in DILATIONS)
    return pl.pallas_call(
        functools.partial(_mix_kernel, layer=layer, casts=casts),
        grid=(B, nt),
        in_specs=in_specs + c_in,
        out_specs=out_specs + tuple(c_out),
        out_shape=out_shape + tuple(c_shapes),
        scratch_shapes=[
            pltpu.VMEM((ts // MIX_SUB, len(DILATIONS) - 1, D // 128, MIX_SUB, 128), F32),
            pltpu.VMEM((ts // MIX_SUB, MIX_SUB + CONV_HALO, W_BRANCH), F32),
            pltpu.VMEM((ts // MIX_SUB, MIX_SUB + POOL_HALO, W_BRANCH), F32),
        ],
        compiler_params=pltpu.CompilerParams(
            dimension_semantics=("arbitrary", "arbitrary"), vmem_limit_bytes=VMEM_LIMIT),
        name="mix",
    )(x, g, w_mix, lng, lnb, ws, bs_full, conv, wpool_bd, dscale, *c_ops)


def _attn_kernel(*refs, casts):
    ins, cast_src, (out_ref,), cast_dst, (o_scr, l_scr, t_scr) = _split_refs(refs, 5, 1, casts)
    q0_ref, q1_ref, q2_ref, bcur_ref, bprev_ref = ins
    _run_casts(casts, cast_src, cast_dst)
    q_refs = (q0_ref, q1_ref, q2_ref)
    S = out_ref.shape[1]
    pr = pl.program_id(1)
    nb = S // CHUNK

    def cols(g, c0):
        return q_refs[g][0, :, c0:c0 + 128].reshape(nb, CHUNK, 128)

    jb = ATTN_BLOCKS_PER_PIECE
    sub0 = lax.broadcasted_iota(jnp.int32, (jb, CHUNK, 128), 2) < HEAD_DIM
    qk = lambda a, b: jnp.einsum("jqd,jkd->jqk", a, b, preferred_element_type=F32)
    pv = lambda a, b: jnp.einsum("jqk,jkd->jqd", a, b, preferred_element_type=F32)
    hh0 = 2 * pr

    for g, dil in enumerate(DILATIONS):
        blocks_per_seq = nb // dil
        has_prev = blocks_per_seq > 1
        q = jnp.concatenate([cols(g, 0), cols(g, 128)], axis=1)
        k = cols(g, 256)
        v = cols(g, 384)
        bias_cur = jnp.concatenate([bcur_ref[g, hh0], bcur_ref[g, hh0 + 1]], axis=0)[None]
        if has_prev:
            k = jnp.concatenate([jnp.concatenate([k[:1], k[:-1]], axis=0), k], axis=1)
            v = jnp.concatenate([jnp.concatenate([v[:1], v[:-1]], axis=0), v], axis=1)
            bias_prev = jnp.concatenate([bprev_ref[g, hh0], bprev_ref[g, hh0 + 1]], axis=0)
        for j0 in range(0, nb, jb):
            s = qk(q[j0:j0 + jb], k[j0:j0 + jb])
            s_cur = s[:, :, -CHUNK:] + bias_cur
            s_prev = [None if not has_prev or (j0 + jj) % blocks_per_seq == 0
                      else s[jj, :, :CHUNK] + bias_prev for jj in range(jb)]
            top = jnp.stack([s_cur[jj] if s_prev[jj] is None else jnp.maximum(s_cur[jj], s_prev[jj])
                             for jj in range(jb)])
            row_max = jnp.max(top, axis=-1, keepdims=True)
            e = jnp.exp(s_cur - row_max)
            den = e
            if has_prev:
                e_prev = jnp.stack([jnp.zeros((2 * CHUNK, CHUNK), F32) if s_prev[jj] is None
                                    else jnp.exp(s_prev[jj] - row_max[jj]) for jj in range(jb)])
                den = e + e_prev
                e = jnp.concatenate([e_prev, e], axis=-1)
            den = jnp.sum(den, axis=-1, keepdims=True)
            acc = pv(e.astype(BF16), v[j0:j0 + jb])
            o2 = acc / den
            l2 = jnp.broadcast_to(row_max + jnp.log(den), o2.shape)
            o = jnp.where(sub0, o2[:, :CHUNK], o2[:, CHUNK:])
            lse = jnp.where(sub0, l2[:, :CHUNK], l2[:, CHUNK:])
            two_stage = g >= 2
            prev_dil = DILATIONS[g - 1] if two_stage else 1
            step = dil // prev_dil
            for jj in range(jb):
                r, blk = divmod(j0 + jj, blocks_per_seq)
                start = (r % prev_dil) * (S // prev_dil) + blk * CHUNK * step + r // prev_dil
                rows = pl.ds(start, CHUNK, stride=step) if step > 1 else pl.ds(start, CHUNK)
                if two_stage:
                    t_scr[0, rows, :] = o[jj]
                    t_scr[1, rows, :] = lse[jj]
                else:
                    o_scr[g, rows, :] = o[jj]
                    l_scr[g, rows, :] = lse[jj]
        if two_stage:
            n_prev = S // prev_dil
            for r in range(prev_dil):
                rows = pl.ds(r, n_prev, stride=prev_dil)
                o_scr[g, rows, :] = t_scr[0, r * n_prev:(r + 1) * n_prev, :]
                l_scr[g, rows, :] = t_scr[1, r * n_prev:(r + 1) * n_prev, :]

    rows = 256
    for c in range(S // rows):
        sl = pl.ds(c * rows, rows)
        l0, l1, l2 = l_scr[0, sl, :], l_scr[1, sl, :], l_scr[2, sl, :]
        m = jnp.maximum(jnp.maximum(l0, l1), l2)
        e0, e1, e2 = jnp.exp(l0 - m), jnp.exp(l1 - m), jnp.exp(l2 - m)
        num = e0 * o_scr[0, sl, :] + e1 * o_scr[1, sl, :] + e2 * o_scr[2, sl, :]
        out_ref[0, sl, :] = (num / (e0 + e1 + e2)).astype(BF16)


def _attn_call(qs, bcur, bprev, cast_jobs=()):
    B = qs[0].shape[0]
    S = qs[0].shape[1] * qs[0].shape[2]
    operands = [q.reshape(B, S, W_QKV_OUT) for q in qs]
    spec = pl.BlockSpec((1, S, W_QKV_OUT // 2), lambda b, p: (b, 0, p))
    casts, c_ops, c_in, c_shapes, c_out = _cast_plumbing(cast_jobs, B * 2, lambda b, p: b * 2 + p)
    return pl.pallas_call(
        functools.partial(_attn_kernel, casts=casts),
        grid=(B, 2),
        in_specs=[spec, spec, spec, _const_spec(bcur.shape), _const_spec(bprev.shape)] + c_in,
        out_specs=(pl.BlockSpec((1, S, 128), lambda b, p: (b, 0, p)),) + tuple(c_out),
        out_shape=(jax.ShapeDtypeStruct((B, S, W_BRANCH), BF16),) + tuple(c_shapes),
        scratch_shapes=[
            pltpu.VMEM((3, S, 128), F32),
            pltpu.VMEM((3, S, 128), F32),
            pltpu.VMEM((2, S, 128), F32),
        ],
        compiler_params=pltpu.CompilerParams(
            dimension_semantics=("arbitrary", "arbitrary"), vmem_limit_bytes=VMEM_LIMIT),
        name="attn",
    )(*operands, bcur, bprev, *c_ops)


def _merge_kernel(*refs, layer, casts):
    ins, cast_src, (o_ref,), cast_dst, _ = _split_refs(refs, 9, 1, casts)
    x_ref, pa_ref, pb_ref, pc_ref, pd_ref, g_ref, wg_ref, wout_ref, wo_ref = ins
    g_ref = _layer_row(g_ref, layer)
    _run_casts(casts, cast_src, cast_dst)
    for s in range(TOK_TILE // SUB_TILE):
        rows = pl.ds(s * SUB_TILE, SUB_TILE)
        x = x_ref[rows, :]
        hb = _rmsnorm(x, g_ref[...]).astype(BF16)
        merged = None
        gates = _dot(hb, wg_ref[...])
        for br, p_ref in enumerate((pa_ref, pb_ref, pc_ref, pd_ref)):
            gate = jax.nn.sigmoid(gates[:, br * D_MODEL:(br + 1) * D_MODEL])
            term = gate * _dot(p_ref[rows, :], wout_ref[br])
            merged = term if merged is None else merged + term
        o_ref[rows, :] = x + _dot(merged.astype(BF16), wo_ref[...])


def _merge_call(layer, x, pa, pb, pc, pd, g, w_gate, w_out, w_o, cast_jobs=()):
    T, D = x.shape
    tt = TOK_TILE
    tok = lambda i: (i, 0)
    pspec = pl.BlockSpec((tt, W_BRANCH), tok)
    casts, c_ops, c_in, c_shapes, c_out = _cast_plumbing(cast_jobs, T // tt, lambda i: i)
    return pl.pallas_call(
        functools.partial(_merge_kernel, layer=layer, casts=casts),
        grid=(T // tt,),
        in_specs=[pl.BlockSpec((tt, D), tok), pspec, pspec, pspec, pspec,
                  _const_spec(g.shape), _const_spec(w_gate.shape), _const_spec(w_out.shape),
                  _const_spec(w_o.shape)] + c_in,
        out_specs=(pl.BlockSpec((tt, D), tok),) + tuple(c_out),
        out_shape=(jax.ShapeDtypeStruct((T, D), F32),) + tuple(c_shapes),
        compiler_params=pltpu.CompilerParams(
            dimension_semantics=("arbitrary",), vmem_limit_bytes=VMEM_LIMIT),
        name="merge",
    )(x, pa, pb, pc, pd, g, w_gate, w_out, w_o, *c_ops)


def _ffn_kernel(*refs, layer, final_norm, casts):
    (x_ref, g_ref, w1_ref, w2_ref, fg_ref), cast_src, (o_ref,), cast_dst, _ = _split_refs(refs, 5, 1, casts)
    g_ref = _layer_row(g_ref, layer)
    _run_casts(casts, cast_src, cast_dst)
    for s in range(TOK_TILE // SUB_TILE):
        rows = pl.ds(s * SUB_TILE, SUB_TILE)
        x = x_ref[rows, :]
        hb = _rmsnorm(x, g_ref[...]).astype(BF16)
        a = _dot(hb, w1_ref[...])
        y = x + _dot(jnp.square(jnp.maximum(a, 0.0)).astype(BF16), w2_ref[...])
        if final_norm:
            y = _rmsnorm(y, fg_ref[...])
        o_ref[rows, :] = y


def _ffn_call(layer, x, g, w1, w2, final_g, final_norm, cast_jobs=()):
    T, D = x.shape
    tt = TOK_TILE
    tok = lambda i: (i, 0)
    casts, c_ops, c_in, c_shapes, c_out = _cast_plumbing(cast_jobs, T // tt, lambda i: i)
    return pl.pallas_call(
        functools.partial(_ffn_kernel, layer=layer, final_norm=final_norm, casts=casts),
        grid=(T // tt,),
        in_specs=[pl.BlockSpec((tt, D), tok), _const_spec(g.shape), _const_spec(w1.shape),
                  _const_spec(w2.shape), _const_spec((1, D))] + c_in,
        out_specs=(pl.BlockSpec((tt, D), tok),) + tuple(c_out),
        out_shape=(jax.ShapeDtypeStruct((T, D), F32),) + tuple(c_shapes),
        compiler_params=pltpu.CompilerParams(
            dimension_semantics=("arbitrary",), vmem_limit_bytes=VMEM_LIMIT),
        name="ffn",
    )(x, g, w1, w2, final_g, *c_ops)


def _t5_bucket_np(dist):
    max_exact = N_BUCKETS // 2
    d_f = np.maximum(dist, 1).astype(np.float32)
    ratio = np.log(d_f / np.float32(max_exact)) / np.float32(math.log(MAX_DISTANCE / max_exact))
    large = max_exact + (ratio * np.float32(N_BUCKETS - max_exact)).astype(np.int32)
    large = np.minimum(large, N_BUCKETS - 1)
    return np.where(dist < max_exact, dist, large)


def _bucket_indices():
    qi = np.arange(CHUNK)[:, None]
    ki = np.arange(CHUNK)[None, :]
    d_cur = qi - ki
    d_prev = qi + CHUNK - ki
    cur = [np.where(d_cur >= 0, _t5_bucket_np(np.clip(d_cur, 0, None) * dil), -1) for dil in DILATIONS]
    prev = [np.where(d_prev <= CHUNK, _t5_bucket_np(d_prev * dil), -1) for dil in DILATIONS[:2]]
    return np.stack(cur).astype(np.int32), np.stack(prev).astype(np.int32)


def _bias_kernel(tab_ref, icur_ref, iprev_ref, bcur_ref, bprev_ref):
    for idx_ref, out_ref in ((icur_ref, bcur_ref), (iprev_ref, bprev_ref)):
        for g in range(idx_ref.shape[0]):
            idx = idx_ref[g]
            accs = [jnp.full((CHUNK, CHUNK), NEG_BIG, F32) for _ in range(HEADS_PER_GROUP)]
            for b in range(N_BUCKETS):
                hit = idx == b
                for h in range(HEADS_PER_GROUP):
                    accs[h] = jnp.where(hit, tab_ref[b, g * HEADS_PER_GROUP + h], accs[h])
            for h in range(HEADS_PER_GROUP):
                out_ref[g, h] = accs[h]


def _bias_tables(rel_bias):
    icur, iprev = _bucket_indices()
    vm = pl.BlockSpec(memory_space=pltpu.VMEM)
    return pl.pallas_call(
        _bias_kernel,
        in_specs=[pl.BlockSpec(memory_space=pltpu.SMEM), vm, vm],
        out_specs=(vm, vm),
        out_shape=(jax.ShapeDtypeStruct((3, HEADS_PER_GROUP, CHUNK, CHUNK), F32),
                   jax.ShapeDtypeStruct((2, HEADS_PER_GROUP, CHUNK, CHUNK), F32)),
        name="bias",
    )(rel_bias, jnp.asarray(icur), jnp.asarray(iprev))


def kernel(x, norm_mix_g, w_in, a_ln_g, a_ln_b, a_ws, a_bs, w_a_out, b_conv, w_b_out, rel_bias,
           w_c_out, d_w, d_scale, w_d_out, w_o, norm_ff_g, w_ff1, w_ff2, final_g):
    B, S, D = x.shape
    depth = w_in.shape[0]
    bcur, bprev = _bias_tables(rel_bias)
    w_outs = (w_a_out, w_b_out, w_c_out, w_d_out)
    w_mix_b = jnp.concatenate([w_in[0][:, a:b] for a, b in _MIX_COLS], axis=1).astype(BF16)
    bs_full = jnp.repeat(jnp.swapaxes(a_bs, 1, 2), W_BRANCH // A_GROUPS, axis=2)
    eye = jnp.eye(POOL_GROUPS, dtype=d_w.dtype)
    wpool_bd = jnp.einsum("lgde,gh->lgdhe", d_w, eye).reshape(depth, W_BRANCH, W_BRANCH).astype(BF16)
    final_row = final_g[None]
    T = B * S
    for l in range(depth):
        more = l + 1 < depth

        mix_jobs = (("plain", 0, (w_o,)), ("w_out", 0, w_outs)) if l == 0 else ()
        pa, pb, pd, q0, q1, q2, *cast = _mix_call(
            l, x, norm_mix_g, w_mix_b, a_ln_g, a_ln_b, a_ws, bs_full, b_conv, wpool_bd, d_scale,
            cast_jobs=mix_jobs)
        if l == 0:
            w_o_b, w_out_b = cast

        attn_jobs = (("plain", 0, (w_ff1,)), ("plain", 0, (w_ff2,)), ("gate", 0, (w_in,))) if l == 0 else ()
        pc, *cast = _attn_call((q0, q1, q2), bcur, bprev, cast_jobs=attn_jobs)
        if l == 0:
            w_ff1_b, w_ff2_b, w_gate_b = cast

        merge_jobs = (("w_in", l + 1, (w_in,)),) if more else ()
        x2, *cast = _merge_call(l, x.reshape(T, D), pa.reshape(T, -1), pb.reshape(T, -1),
                                pc.reshape(T, -1), pd.reshape(T, -1), norm_mix_g,
                                w_gate_b, w_out_b, w_o_b, cast_jobs=merge_jobs)
        if more:
            w_mix_b, w_gate_b = cast

        ffn_jobs = (("plain", l + 1, (w_ff1,)), ("plain", l + 1, (w_ff2,)), ("plain", l + 1, (w_o,)),
                    ("w_out", l + 1, w_outs)) if more else ()
        x2, *cast = _ffn_call(l, x2, norm_ff_g, w_ff1_b, w_ff2_b, final_row,
                              final_norm=not more, cast_jobs=ffn_jobs)
        if more:
            w_ff1_b, w_ff2_b, w_o_b, w_out_b = cast
        x = x2.reshape(B, S, D)
    return x
```

```python
import functools
import math

import jax
import jax.numpy as jnp
import numpy as np
from jax import lax
from jax.experimental import pallas as pl
from jax.experimental.pallas import tpu as pltpu

F32 = jnp.float32
BF16 = jnp.bfloat16

D_MODEL = 1024
W_BRANCH = 256
A_GROUPS = 4
CHUNK = 128
CONV_WIDTH = 3
DILATIONS = (1, 4, 16)
WINDOWS = (128, 512, 2048)
assert all(w == CHUNK * d for w, d in zip(WINDOWS, DILATIONS))
HEADS_PER_GROUP = 4
HEAD_DIM = 64
POOL_WINDOWS = (2, 4, 8, 16)
POOL_GROUPS = len(POOL_WINDOWS)
POOL_HALO = 16
CONV_HALO = 8
N_BRANCH = 4
N_BUCKETS = 32
MAX_DISTANCE = 2048
EPS = 1e-6
NEG_BIG = -1e30
COL_Q = 5 * W_BRANCH
COL_K = COL_Q + 3 * W_BRANCH
COL_V = COL_K + 3 * W_BRANCH
COL_D = COL_V + 3 * W_BRANCH
COL_GATE = COL_D + W_BRANCH
W_LOCAL = 6 * W_BRANCH
W_QKV = 3 * W_BRANCH
W_QKV_OUT = 4 * W_BRANCH

MIX_TILE = 1024
MIX_SUB = 512
TOK_TILE = 1024
SUB_TILE = 512
ATTN_BLOCKS_PER_PIECE = 8
VMEM_LIMIT = 56 * 1024 * 1024


def _rmsnorm(x, g):
    return x * lax.rsqrt(jnp.mean(x * x, axis=-1, keepdims=True) + EPS) * g


def _gelu_tanh(x):
    c = math.sqrt(2.0 / math.pi)
    return x * (0.5 * (1.0 + jnp.tanh(c * (x + 0.044715 * (x * x * x)))))


def _dot(a, b):
    return jnp.dot(a, b, preferred_element_type=F32)


def _const_spec(shape):
    n = len(shape)
    return pl.BlockSpec(shape, lambda *_: (0,) * n, pipeline_mode=pl.Buffered(1))


_CAST_ARITY = {"plain": (1, 1), "w_in": (1, 2), "gate": (1, 1), "w_out": (N_BRANCH, 1)}
_MIX_COLS = ((0, COL_Q), (COL_D, COL_GATE)) + tuple(
    (col + g * W_BRANCH, col + (g + 1) * W_BRANCH)
    for g in range(len(DILATIONS)) for col in (COL_Q, COL_K, COL_V))


def _cast_plumbing(jobs, n_steps, step_of):
    kinds, operands, in_specs, out_shapes, out_specs = [], [], [], [], []
    for kind, layer, srcs in jobs:
        kinds.append(kind)
        R, C = srcs[0].shape[1:]
        rows = R // n_steps
        assert rows * n_steps == R and rows % 16 == 0, (kind, R, n_steps)
        for src in srcs:
            operands.append(src)
            in_specs.append(pl.BlockSpec((1, rows, C), lambda *g, layer=layer: (layer, step_of(*g), 0)))
        row_blk = lambda *g: (step_of(*g), 0)
        if kind == "w_out":
            out_shapes.append(jax.ShapeDtypeStruct((len(srcs), R, C), BF16))
            out_specs.append(pl.BlockSpec((len(srcs), rows, C), lambda *g: (0, step_of(*g), 0)))
        elif kind in ("w_in", "gate"):
            widths = (C - COL_GATE,) if kind == "gate" else (sum(b - a for a, b in _MIX_COLS), C - COL_GATE)
            for width in widths:
                out_shapes.append(jax.ShapeDtypeStruct((R, width), BF16))
                out_specs.append(pl.BlockSpec((rows, width), row_blk))
        else:
            out_shapes.append(jax.ShapeDtypeStruct((R, C), BF16))
            out_specs.append(pl.BlockSpec((rows, C), row_blk))
    return tuple(kinds), operands, in_specs, out_shapes, out_specs


def _split_refs(refs, n_in, n_out, kinds):
    n_ci = sum(_CAST_ARITY[k][0] for k in kinds)
    n_co = sum(_CAST_ARITY[k][1] for k in kinds)
    a, b, c = n_in, n_in + n_ci, n_in + n_ci + n_out
    return refs[:a], refs[a:b], refs[b:c], refs[c:c + n_co], refs[c + n_co:]


def _run_casts(kinds, src_refs, dst_refs):
    i = o = 0
    for kind in kinds:
        n_i, n_o = _CAST_ARITY[kind]
        srcs, dsts = src_refs[i:i + n_i], dst_refs[o:o + n_o]
        i, o = i + n_i, o + n_o
        if kind == "plain":
            dsts[0][...] = srcs[0][0].astype(BF16)
        elif kind == "w_out":
            for k, src in enumerate(srcs):
                dsts[0][k] = src[0].astype(BF16)
        elif kind == "gate":
            dsts[0][...] = srcs[0][0, :, COL_GATE:].astype(BF16)
        else:
            c = 0
            for a, b in _MIX_COLS:
                dsts[0][:, c:c + b - a] = srcs[0][0, :, a:b].astype(BF16)
                c += b - a
            dsts[1][...] = srcs[0][0, :, COL_GATE:].astype(BF16)


def _layer_row(ref, layer):
    return ref.at[pl.ds(layer, 1)]


def _mix_kernel(*refs, layer, casts):
    ins, cast_src, outs, cast_dst, (h_scr, bbuf, dbuf) = _split_refs(refs, 10, 6, casts)
    x_ref, g_ref, w_ref, lng_ref, lnb_ref, ws_ref, bs_ref, conv_ref, wpool_ref, dscale_ref = ins
    g_ref, lng_ref, lnb_ref, dscale_ref = (_layer_row(r, layer) for r in (g_ref, lng_ref, lnb_ref, dscale_ref))
    ws_ref, bs_ref, conv_ref, wpool_ref = (r.at[layer] for r in (ws_ref, bs_ref, conv_ref, wpool_ref))
    pa_ref, pb_ref, pd_ref, q0_ref, q1_ref, q2_ref = outs
    _run_casts(casts, cast_src, cast_dst)
    n_sub = MIX_TILE // MIX_SUB
    first = (pl.program_id(0) == 0) & (pl.program_id(1) == 0)

    @pl.when(first)
    def _():
        bbuf[...] = jnp.zeros(bbuf.shape, F32)
        dbuf[...] = jnp.zeros(dbuf.shape, F32)

    for s in range(n_sub):
        _mix_subtile(s, pl.program_id(1) * n_sub + s, x_ref, g_ref, w_ref, lng_ref, lnb_ref, ws_ref,
                     bs_ref, conv_ref, wpool_ref, dscale_ref, pa_ref, pb_ref, pd_ref,
                     (q0_ref, q1_ref, q2_ref), h_scr.at[s], bbuf.at[s], bbuf.at[(s - 1) % n_sub],
                     dbuf.at[s], dbuf.at[(s - 1) % n_sub])


def _mix_subtile(s, i, x_ref, g_ref, w_ref, lng_ref, lnb_ref, ws_ref, bs_ref, conv_ref, wpool_ref,
                 dscale_ref, pa_ref, pb_ref, pd_ref, q_refs, h_scr, bbuf, bbuf_prev, dbuf, dbuf_prev):
    ts = MIX_SUB
    rows = pl.ds(s * ts, ts)
    x = x_ref[0, rows, :]
    h = _rmsnorm(x, g_ref[...])
    for c in range(D_MODEL // 128):
        h_scr[0, c] = h[:, c * 128:(c + 1) * 128]
    hb = h.astype(BF16)
    zl = _dot(hb, w_ref[:, 0:W_LOCAL + W_QKV])

    u = _gelu_tanh(zl[:, 0:256])
    v = _gelu_tanh(zl[:, 256:512])
    mu = jnp.mean(v, axis=-1, keepdims=True)
    vc = v - mu
    var = jnp.mean(vc * vc, axis=-1, keepdims=True)
    v = vc * lax.rsqrt(var + EPS) * lng_ref[...] + lnb_ref[...]
    row = lax.broadcasted_iota(jnp.int32, (CHUNK, A_GROUPS * CHUNK), 0)
    col = lax.broadcasted_iota(jnp.int32, (CHUNK, A_GROUPS * CHUNK), 1) % CHUNK
    w_all = jnp.concatenate([ws_ref[g] for g in range(A_GROUPS)], axis=1)
    wtril = jnp.where(row >= col, w_all, 0.0).astype(BF16)
    grp = lax.broadcasted_iota(jnp.int32, (CHUNK, W_BRANCH), 1) // (W_BRANCH // A_GROUPS)
    svs = []
    for c in range(ts // CHUNK):
        vch = v[c * CHUNK:(c + 1) * CHUNK]
        stacked = jnp.concatenate(
            [jnp.where(grp == g, vch, 0.0) for g in range(A_GROUPS)], axis=0).astype(BF16)
        svs.append(_dot(wtril, stacked) + bs_ref[...])
    sv = jnp.concatenate(svs, axis=0)
    pa_ref[0, rows, :] = (u * sv).astype(BF16)

    prod = zl[:, 768:1024] * zl[:, 1024:1280]

    bbuf[0:CONV_HALO] = jnp.where(i > 0, bbuf_prev[ts:ts + CONV_HALO], 0.0)
    dbuf[0:POOL_HALO] = jnp.where(i > 0, dbuf_prev[ts:ts + POOL_HALO], 0.0)
    bbuf[CONV_HALO:ts + CONV_HALO] = prod
    cw = conv_ref[...]
    conv = (cw[0:1] * bbuf[pl.ds(CONV_HALO - 2, ts), :]
            + cw[1:2] * bbuf[pl.ds(CONV_HALO - 1, ts), :]
            + cw[2:3] * prod)
    pb_ref[0, rows, :] = (zl[:, 512:768] * conv).astype(BF16)

    dz = zl[:, 1280:1536]
    dbuf[POOL_HALO:ts + POOL_HALO] = dz
    lane = lax.broadcasted_iota(jnp.int32, (ts, 128), 1)
    first_half = lane < 64
    tpos = (lax.broadcasted_iota(jnp.int32, (ts, 128), 0) + (i * ts + 1)).astype(F32)

    def window_sums(e, levels):
        out = []
        s = e
        for k in range(levels):
            s = s + pltpu.roll(s, 1 << k, 0)
            out.append(s[POOL_HALO:])
        return out

    lo = window_sums(dbuf[:, 0:128], 2)
    hi = window_sums(dbuf[:, 128:256], 4)
    pooled_lo = jnp.where(first_half, lo[0], lo[1]) / jnp.minimum(tpos, jnp.where(first_half, 2.0, 4.0))
    pooled_hi = jnp.where(first_half, hi[2], hi[3]) / jnp.minimum(tpos, jnp.where(first_half, 8.0, 16.0))
    y = jnp.concatenate([pooled_lo, pooled_hi], axis=1) - dz
    pd_ref[0, rows, :] = (_dot(y.astype(BF16), wpool_ref[...]) * dscale_ref[...]).astype(BF16)

    sub0 = lax.broadcasted_iota(jnp.int32, (ts, 128), 1) < HEAD_DIM
    for g, (q_ref, dil) in enumerate(zip(q_refs, DILATIONS)):
        n = ts // dil
        if dil == 1:
            z = zl[:, W_LOCAL:W_LOCAL + W_QKV]
        else:
            prev_dil = DILATIONS[g - 1]
            src, n_prev, step = h_scr.at[g - 1], ts // prev_dil, dil // prev_dil
            slabs = [jnp.concatenate([src[c, pl.ds((r % prev_dil) * n_prev + r // prev_dil, n, stride=step), :]
                                      for r in range(dil)], axis=0) for c in range(D_MODEL // 128)]
            if g + 1 < len(DILATIONS):
                for c, slab in enumerate(slabs):
                    h_scr[g, c] = slab
            hp = jnp.concatenate(slabs, axis=1).astype(BF16)
            c0 = W_LOCAL + g * W_QKV
            z = _dot(hp, w_ref[:, c0:c0 + W_QKV])
        q = z[:, 0:256] * (HEAD_DIM ** -0.5)
        pieces = []
        for pr in range(2):
            qp = q[:, pr * 128:(pr + 1) * 128]
            pieces.append(jnp.where(sub0, qp, 0.0))
            pieces.append(jnp.where(sub0, 0.0, qp))
            pieces.append(z[:, 256 + pr * 128:256 + (pr + 1) * 128])
            pieces.append(z[:, 512 + pr * 128:512 + (pr + 1) * 128])
        val = jnp.concatenate(pieces, axis=1).astype(BF16)
        for r in range(dil):
            q_ref[0, r, pl.ds(s * n, n), :] = val[r * n:(r + 1) * n]


def _mix_call(layer, x, g, w_mix, lng, lnb, ws, bs_full, conv, wpool_bd, dscale, cast_jobs=()):
    B, S, D = x.shape
    ts = MIX_TILE
    nt = S // ts
    tok = lambda b, i: (b, i, 0)
    casts, c_ops, c_in, c_shapes, c_out = _cast_plumbing(cast_jobs, B * nt, lambda b, i: b * nt + i)
    out_shape = (
        jax.ShapeDtypeStruct((B, S, W_BRANCH), BF16),
        jax.ShapeDtypeStruct((B, S, W_BRANCH), BF16),
        jax.ShapeDtypeStruct((B, S, W_BRANCH), BF16),
    ) + tuple(jax.ShapeDtypeStruct((B, dil, S // dil, W_QKV_OUT), BF16) for dil in DILATIONS)
    in_specs = [
        pl.BlockSpec((1, ts, D), tok),
    ] + [_const_spec(a.shape) for a in (g, w_mix, lng, lnb, ws, bs_full, conv, wpool_bd, dscale)]
    out_specs = (
        pl.BlockSpec((1, ts, W_BRANCH), tok),
        pl.BlockSpec((1, ts, W_BRANCH), tok),
        pl.BlockSpec((1, ts, W_BRANCH), tok),
    ) + tuple(pl.BlockSpec((1, dil, ts // dil, W_QKV_OUT), lambda b, i: (b, 0, i, 0))
              for dil in DILATIONS)
    return pl.pallas_call(
        functools.partial(_mix_kernel, layer=layer, casts=casts),
        grid=(B, nt),
        in_specs=in_specs + c_in,
        out_specs=out_specs + tuple(c_out),
        out_shape=out_shape + tuple(c_shapes),
        scratch_shapes=[
            pltpu.VMEM((ts // MIX_SUB, len(DILATIONS) - 1, D // 128, MIX_SUB, 128), F32),
            pltpu.VMEM((ts // MIX_SUB, MIX_SUB + CONV_HALO, W_BRANCH), F32),
            pltpu.VMEM((ts // MIX_SUB, MIX_SUB + POOL_HALO, W_BRANCH), F32),
        ],
        compiler_params=pltpu.CompilerParams(
            dimension_semantics=("arbitrary", "arbitrary"), vmem_limit_bytes=VMEM_LIMIT),
        name="mix",
    )(x, g, w_mix, lng, lnb, ws, bs_full, conv, wpool_bd, dscale, *c_ops)


def _attn_kernel(*refs, casts):
    ins, cast_src, (out_ref,), cast_dst, (o_scr, l_scr, t_scr) = _split_refs(refs, 5, 1, casts)
    q0_ref, q1_ref, q2_ref, bcur_ref, bprev_ref = ins
    _run_casts(casts, cast_src, cast_dst)
    q_refs = (q0_ref, q1_ref, q2_ref)
    S = out_ref.shape[1]
    pr = pl.program_id(1)
    nb = S // CHUNK

    def cols(g, c0):
        return q_refs[g][0, :, c0:c0 + 128].reshape(nb, CHUNK, 128)

    jb = ATTN_BLOCKS_PER_PIECE
    sub0 = lax.broadcasted_iota(jnp.int32, (jb, CHUNK, 128), 2) < HEAD_DIM
    qk = lambda a, b: jnp.einsum("jqd,jkd->jqk", a, b, preferred_element_type=F32)
    pv = lambda a, b: jnp.einsum("jqk,jkd->jqd", a, b, preferred_element_type=F32)
    hh0 = 2 * pr

    for g, dil in enumerate(DILATIONS):
        blocks_per_seq = nb // dil
        has_prev = blocks_per_seq > 1
        q = jnp.concatenate([cols(g, 0), cols(g, 128)], axis=1)
        k = cols(g, 256)
        v = cols(g, 384)
        bias_cur = jnp.concatenate([bcur_ref[g, hh0], bcur_ref[g, hh0 + 1]], axis=0)[None]
        if has_prev:
            k = jnp.concatenate([jnp.concatenate([k[:1], k[:-1]], axis=0), k], axis=1)
            v = jnp.concatenate([jnp.concatenate([v[:1], v[:-1]], axis=0), v], axis=1)
            bias_prev = jnp.concatenate([bprev_ref[g, hh0], bprev_ref[g, hh0 + 1]], axis=0)
        for j0 in range(0, nb, jb):
            s = qk(q[j0:j0 + jb], k[j0:j0 + jb])
            s_cur = s[:, :, -CHUNK:] + bias_cur
            s_prev = [None if not has_prev or (j0 + jj) % blocks_per_seq == 0
                      else s[jj, :, :CHUNK] + bias_prev for jj in range(jb)]
            top = jnp.stack([s_cur[jj] if s_prev[jj] is None else jnp.maximum(s_cur[jj], s_prev[jj])
                             for jj in range(jb)])
            row_max = jnp.max(top, axis=-1, keepdims=True)
            e = jnp.exp(s_cur - row_max)
            den = e
            if has_prev:
                e_prev = jnp.stack([jnp.zeros((2 * CHUNK, CHUNK), F32) if s_prev[jj] is None
                                    else jnp.exp(s_prev[jj] - row_max[jj]) for jj in range(jb)])
                den = e + e_prev
                e = jnp.concatenate([e_prev, e], axis=-1)
            den = jnp.sum(den, axis=-1, keepdims=True)
            acc = pv(e.astype(BF16), v[j0:j0 + jb])
            o2 = acc / den
            l2 = jnp.broadcast_to(row_max + jnp.log(den), o2.shape)
            o = jnp.where(sub0, o2[:, :CHUNK], o2[:, CHUNK:])
            lse = jnp.where(sub0, l2[:, :CHUNK], l2[:, CHUNK:])
            two_stage = g >= 2
            prev_dil = DILATIONS[g - 1] if two_stage else 1
            step = dil // prev_dil
            for jj in range(jb):
                r, blk = divmod(j0 + jj, blocks_per_seq)
                start = (r % prev_dil) * (S // prev_dil) + blk * CHUNK * step + r // prev_dil
                rows = pl.ds(start, CHUNK, stride=step) if step > 1 else pl.ds(start, CHUNK)
                if two_stage:
                    t_scr[0, rows, :] = o[jj]
                    t_scr[1, rows, :] = lse[jj]
                else:
                    o_scr[g, rows, :] = o[jj]
                    l_scr[g, rows, :] = lse[jj]
        if two_stage:
            n_prev = S // prev_dil
            for r in range(prev_dil):
                rows = pl.ds(r, n_prev, stride=prev_dil)
                o_scr[g, rows, :] = t_scr[0, r * n_prev:(r + 1) * n_prev, :]
                l_scr[g, rows, :] = t_scr[1, r * n_prev:(r + 1) * n_prev, :]

    rows = 256
    for c in range(S // rows):
        sl = pl.ds(c * rows, rows)
        l0, l1, l2 = l_scr[0, sl, :], l_scr[1, sl, :], l_scr[2, sl, :]
        m = jnp.maximum(jnp.maximum(l0, l1), l2)
        e0, e1, e2 = jnp.exp(l0 - m), jnp.exp(l1 - m), jnp.exp(l2 - m)
        num = e0 * o_scr[0, sl, :] + e1 * o_scr[1, sl, :] + e2 * o_scr[2, sl, :]
        out_ref[0, sl, :] = (num / (e0 + e1 + e2)).astype(BF16)


def _attn_call(qs, bcur, bprev, cast_jobs=()):
    B = qs[0].shape[0]
    S = qs[0].shape[1] * qs[0].shape[2]
    operands = [q.reshape(B, S, W_QKV_OUT) for q in qs]
    spec = pl.BlockSpec((1, S, W_QKV_OUT // 2), lambda b, p: (b, 0, p))
    casts, c_ops, c_in, c_shapes, c_out = _cast_plumbing(cast_jobs, B * 2, lambda b, p: b * 2 + p)
    return pl.pallas_call(
        functools.partial(_attn_kernel, casts=casts),
        grid=(B, 2),
        in_specs=[spec, spec, spec, _const_spec(bcur.shape), _const_spec(bprev.shape)] + c_in,
        out_specs=(pl.BlockSpec((1, S, 128), lambda b, p: (b, 0, p)),) + tuple(c_out),
        out_shape=(jax.ShapeDtypeStruct((B, S, W_BRANCH), BF16),) + tuple(c_shapes),
        scratch_shapes=[
            pltpu.VMEM((3, S, 128), F32),
            pltpu.VMEM((3, S, 128), F32),
            pltpu.VMEM((2, S, 128), F32),
        ],
        compiler_params=pltpu.CompilerParams(
            dimension_semantics=("arbitrary", "arbitrary"), vmem_limit_bytes=VMEM_LIMIT),
        name="attn",
    )(*operands, bcur, bprev, *c_ops)


def _merge_kernel(*refs, layer, casts):
    ins, cast_src, (o_ref,), cast_dst, _ = _split_refs(refs, 9, 1, casts)
    x_ref, pa_ref, pb_ref, pc_ref, pd_ref, g_ref, wg_ref, wout_ref, wo_ref = ins
    g_ref = _layer_row(g_ref, layer)
    _run_casts(casts, cast_src, cast_dst)
    for s in range(TOK_TILE // SUB_TILE):
        rows = pl.ds(s * SUB_TILE, SUB_TILE)
        x = x_ref[rows, :]
        hb = _rmsnorm(x, g_ref[...]).astype(BF16)
        merged = None
        gates = _dot(hb, wg_ref[...])
        for br, p_ref in enumerate((pa_ref, pb_ref, pc_ref, pd_ref)):
            gate = jax.nn.sigmoid(gates[:, br * D_MODEL:(br + 1) * D_MODEL])
            term = gate * _dot(p_ref[rows, :], wout_ref[br])
            merged = term if merged is None else merged + term
        o_ref[rows, :] = x + _dot(merged.astype(BF16), wo_ref[...])


def _merge_call(layer, x, pa, pb, pc, pd, g, w_gate, w_out, w_o, cast_jobs=()):
    T, D = x.shape
    tt = TOK_TILE
    tok = lambda i: (i, 0)
    pspec = pl.BlockSpec((tt, W_BRANCH), tok)
    casts, c_ops, c_in, c_shapes, c_out = _cast_plumbing(cast_jobs, T // tt, lambda i: i)
    return pl.pallas_call(
        functools.partial(_merge_kernel, layer=layer, casts=casts),
        grid=(T // tt,),
        in_specs=[pl.BlockSpec((tt, D), tok), pspec, pspec, pspec, pspec,
                  _const_spec(g.shape), _const_spec(w_gate.shape), _const_spec(w_out.shape),
                  _const_spec(w_o.shape)] + c_in,
        out_specs=(pl.BlockSpec((tt, D), tok),) + tuple(c_out),
        out_shape=(jax.ShapeDtypeStruct((T, D), F32),) + tuple(c_shapes),
        compiler_params=pltpu.CompilerParams(
            dimension_semantics=("arbitrary",), vmem_limit_bytes=VMEM_LIMIT),
        name="merge",
    )(x, pa, pb, pc, pd, g, w_gate, w_out, w_o, *c_ops)


def _ffn_kernel(*refs, layer, final_norm, casts):
    (x_ref, g_ref, w1_ref, w2_ref, fg_ref), cast_src, (o_ref,), cast_dst, _ = _split_refs(refs, 5, 1, casts)
    g_ref = _layer_row(g_ref, layer)
    _run_casts(casts, cast_src, cast_dst)
    for s in range(TOK_TILE // SUB_TILE):
        rows = pl.ds(s * SUB_TILE, SUB_TILE)
        x = x_ref[rows, :]
        hb = _rmsnorm(x, g_ref[...]).astype(BF16)
        a = _dot(hb, w1_ref[...])
        y = x + _dot(jnp.square(jnp.maximum(a, 0.0)).astype(BF16), w2_ref[...])
        if final_norm:
            y = _rmsnorm(y, fg_ref[...])
        o_ref[rows, :] = y


def _ffn_call(layer, x, g, w1, w2, final_g, final_norm, cast_jobs=()):
    T, D = x.shape
    tt = TOK_TILE
    tok = lambda i: (i, 0)
    casts, c_ops, c_in, c_shapes, c_out = _cast_plumbing(cast_jobs, T // tt, lambda i: i)
    return pl.pallas_call(
        functools.partial(_ffn_kernel, layer=layer, final_norm=final_norm, casts=casts),
        grid=(T // tt,),
        in_specs=[pl.BlockSpec((tt, D), tok), _const_spec(g.shape), _const_spec(w1.shape),
                  _const_spec(w2.shape), _const_spec((1, D))] + c_in,
        out_specs=(pl.BlockSpec((tt, D), tok),) + tuple(c_out),
        out_shape=(jax.ShapeDtypeStruct((T, D), F32),) + tuple(c_shapes),
        compiler_params=pltpu.CompilerParams(
            dimension_semantics=("arbitrary",), vmem_limit_bytes=VMEM_LIMIT),
        name="ffn",
    )(x, g, w1, w2, final_g, *c_ops)


def _t5_bucket_np(dist):
    max_exact = N_BUCKETS // 2
    d_f = np.maximum(dist, 1).astype(np.float32)
    ratio = np.log(d_f / np.float32(max_exact)) / np.float32(math.log(MAX_DISTANCE / max_exact))
    large = max_exact + (ratio * np.float32(N_BUCKETS - max_exact)).astype(np.int32)
    large = np.minimum(large, N_BUCKETS - 1)
    return np.where(dist < max_exact, dist, large)


def _bucket_indices():
    qi = np.arange(CHUNK)[:, None]
    ki = np.arange(CHUNK)[None, :]
    d_cur = qi - ki
    d_prev = qi + CHUNK - ki
    cur = [np.where(d_cur >= 0, _t5_bucket_np(np.clip(d_cur, 0, None) * dil), -1) for dil in DILATIONS]
    prev = [np.where(d_prev <= CHUNK, _t5_bucket_np(d_prev * dil), -1) for dil in DILATIONS[:2]]
    return np.stack(cur).astype(np.int32), np.stack(prev).astype(np.int32)


def _bias_kernel(tab_ref, icur_ref, iprev_ref, bcur_ref, bprev_ref):
    for idx_ref, out_ref in ((icur_ref, bcur_ref), (iprev_ref, bprev_ref)):
        for g in range(idx_ref.shape[0]):
            idx = idx_ref[g]
            accs = [jnp.full((CHUNK, CHUNK), NEG_BIG, F32) for _ in range(HEADS_PER_GROUP)]
            for b in range(N_BUCKETS):
                hit = idx == b
                for h in range(HEADS_PER_GROUP):
                    accs[h] = jnp.where(hit, tab_ref[b, g * HEADS_PER_GROUP + h], accs[h])
            for h in range(HEADS_PER_GROUP):
                out_ref[g, h] = accs[h]


def _bias_tables(rel_bias):
    icur, iprev = _bucket_indices()
    vm = pl.BlockSpec(memory_space=pltpu.VMEM)
    return pl.pallas_call(
        _bias_kernel,
        in_specs=[pl.BlockSpec(memory_space=pltpu.SMEM), vm, vm],
        out_specs=(vm, vm),
        out_shape=(jax.ShapeDtypeStruct((3, HEADS_PER_GROUP, CHUNK, CHUNK), F32),
                   jax.ShapeDtypeStruct((2, HEADS_PER_GROUP, CHUNK, CHUNK), F32)),
        name="bias",
    )(rel_bias, jnp.asarray(icur), jnp.asarray(iprev))


def kernel(x, norm_mix_g, w_in, a_ln_g, a_ln_b, a_ws, a_bs, w_a_out, b_conv, w_b_out, rel_bias,
           w_c_out, d_w, d_scale, w_d_out, w_o, norm_ff_g, w_ff1, w_ff2, final_g):
    B, S, D = x.shape
    depth = w_in.shape[0]
    bcur, bprev = _bias_tables(rel_bias)
    w_outs = (w_a_out, w_b_out, w_c_out, w_d_out)
    w_mix_b = jnp.concatenate([w_in[0][:, a:b] for a, b in _MIX_COLS], axis=1).astype(BF16)
    bs_full = jnp.repeat(jnp.swapaxes(a_bs, 1, 2), W_BRANCH // A_GROUPS, axis=2)
    eye = jnp.eye(POOL_GROUPS, dtype=d_w.dtype)
    wpool_bd = jnp.einsum("lgde,gh->lgdhe", d_w, eye).reshape(depth, W_BRANCH, W_BRANCH).astype(BF16)
    final_row = final_g[None]
    T = B * S
    for l in range(depth):
        more = l + 1 < depth

        mix_jobs = (("plain", 0, (w_o,)), ("w_out", 0, w_outs)) if l == 0 else ()
        pa, pb, pd, q0, q1, q2, *cast = _mix_call(
            l, x, norm_mix_g, w_mix_b, a_ln_g, a_ln_b, a_ws, bs_full, b_conv, wpool_bd, d_scale,
            cast_jobs=mix_jobs)
        if l == 0:
            w_o_b, w_out_b = cast

        attn_jobs = (("plain", 0, (w_ff1,)), ("plain", 0, (w_ff2,)), ("gate", 0, (w_in,))) if l == 0 else ()
        pc, *cast = _attn_call((q0, q1, q2), bcur, bprev, cast_jobs=attn_jobs)
        if l == 0:
            w_ff1_b, w_ff2_b, w_gate_b = cast

        merge_jobs = (("w_in", l + 1, (w_in,)),) if more else ()
        x2, *cast = _merge_call(l, x.reshape(T, D), pa.reshape(T, -1), pb.reshape(T, -1),
                                pc.reshape(T, -1), pd.reshape(T, -1), norm_mix_g,
                                w_gate_b, w_out_b, w_o_b, cast_jobs=merge_jobs)
        if more:
            w_mix_b, w_gate_b = cast

        ffn_jobs = (("plain", l + 1, (w_ff1,)), ("plain", l + 1, (w_ff2,)), ("plain", l + 1, (w_o,)),
                    ("w_out", l + 1, w_outs)) if more else ()
        x2, *cast = _ffn_call(l, x2, norm_ff_g, w_ff1_b, w_ff2_b, final_row,
                              final_norm=not more, cast_jobs=ffn_jobs)
        if more:
            w_ff1_b, w_ff2_b, w_o_b, w_out_b = cast
        x = x2.reshape(B, S, D)
    return x
```

```python
import functools
import math

import jax
import jax.numpy as jnp
import numpy as np
from jax import lax
from jax.experimental import pallas as pl
from jax.experimental.pallas import tpu as pltpu

F32 = jnp.float32
BF16 = jnp.bfloat16

D_MODEL = 1024
W_BRANCH = 256
A_GROUPS = 4
CHUNK = 128
CONV_WIDTH = 3
DILATIONS = (1, 4, 16)
WINDOWS = (128, 512, 2048)
assert all(w == CHUNK * d for w, d in zip(WINDOWS, DILATIONS))
HEADS_PER_GROUP = 4
HEAD_DIM = 64
POOL_WINDOWS = (2, 4, 8, 16)
POOL_GROUPS = len(POOL_WINDOWS)
POOL_HALO = 16
CONV_HALO = 8
N_BRANCH = 4
N_BUCKETS = 32
MAX_DISTANCE = 2048
EPS = 1e-6
NEG_BIG = -1e30
COL_Q = 5 * W_BRANCH
COL_K = COL_Q + 3 * W_BRANCH
COL_V = COL_K + 3 * W_BRANCH
COL_D = COL_V + 3 * W_BRANCH
COL_GATE = COL_D + W_BRANCH
W_LOCAL = 6 * W_BRANCH
W_QKV = 3 * W_BRANCH
W_QKV_OUT = 4 * W_BRANCH

MIX_TILE = 1024
MIX_SUB = 256
TOK_TILE = 1024
MERGE_SUB = 256
FFN_SUB = 512
ATTN_BLOCKS_PER_PIECE = 8
VMEM_LIMIT = 56 * 1024 * 1024


def _rmsnorm(x, g):
    return x * lax.rsqrt(jnp.mean(x * x, axis=-1, keepdims=True) + EPS) * g


def _gelu_tanh(x):
    c = math.sqrt(2.0 / math.pi)
    return x * (0.5 * (1.0 + jnp.tanh(c * (x + 0.044715 * (x * x * x)))))


def _dot(a, b):
    return jnp.dot(a, b, preferred_element_type=F32)


def _const_spec(shape):
    n = len(shape)
    return pl.BlockSpec(shape, lambda *_: (0,) * n, pipeline_mode=pl.Buffered(1))


_CAST_ARITY = {"plain": (1, 1), "w_in": (1, 2), "gate": (1, 1), "w_out": (N_BRANCH, 1)}
_MIX_COLS = ((0, COL_Q), (COL_D, COL_GATE)) + tuple(
    (col + g * W_BRANCH, col + (g + 1) * W_BRANCH)
    for g in range(len(DILATIONS)) for col in (COL_Q, COL_K, COL_V))


def _cast_plumbing(jobs, n_steps, step_of):
    kinds, operands, in_specs, out_shapes, out_specs = [], [], [], [], []
    for kind, layer, srcs in jobs:
        kinds.append(kind)
        R, C = srcs[0].shape[1:]
        rows = R // n_steps
        assert rows * n_steps == R and rows % 16 == 0, (kind, R, n_steps)
        for src in srcs:
            operands.append(src)
            in_specs.append(pl.BlockSpec((1, rows, C), lambda *g, layer=layer: (layer, step_of(*g), 0)))
        row_blk = lambda *g: (step_of(*g), 0)
        if kind == "w_out":
            out_shapes.append(jax.ShapeDtypeStruct((len(srcs), R, C), BF16))
            out_specs.append(pl.BlockSpec((len(srcs), rows, C), lambda *g: (0, step_of(*g), 0)))
        elif kind in ("w_in", "gate"):
            widths = (C - COL_GATE,) if kind == "gate" else (sum(b - a for a, b in _MIX_COLS), C - COL_GATE)
            for width in widths:
                out_shapes.append(jax.ShapeDtypeStruct((R, width), BF16))
                out_specs.append(pl.BlockSpec((rows, width), row_blk))
        else:
            out_shapes.append(jax.ShapeDtypeStruct((R, C), BF16))
            out_specs.append(pl.BlockSpec((rows, C), row_blk))
    return tuple(kinds), operands, in_specs, out_shapes, out_specs


def _split_refs(refs, n_in, n_out, kinds):
    n_ci = sum(_CAST_ARITY[k][0] for k in kinds)
    n_co = sum(_CAST_ARITY[k][1] for k in kinds)
    a, b, c = n_in, n_in + n_ci, n_in + n_ci + n_out
    return refs[:a], refs[a:b], refs[b:c], refs[c:c + n_co], refs[c + n_co:]


def _run_casts(kinds, src_refs, dst_refs):
    i = o = 0
    for kind in kinds:
        n_i, n_o = _CAST_ARITY[kind]
        srcs, dsts = src_refs[i:i + n_i], dst_refs[o:o + n_o]
        i, o = i + n_i, o + n_o
        if kind == "plain":
            dsts[0][...] = srcs[0][0].astype(BF16)
        elif kind == "w_out":
            for k, src in enumerate(srcs):
                dsts[0][k] = src[0].astype(BF16)
        elif kind == "gate":
            dsts[0][...] = srcs[0][0, :, COL_GATE:].astype(BF16)
        else:
            c = 0
            for a, b in _MIX_COLS:
                dsts[0][:, c:c + b - a] = srcs[0][0, :, a:b].astype(BF16)
                c += b - a
            dsts[1][...] = srcs[0][0, :, COL_GATE:].astype(BF16)


def _layer_row(ref, layer):
    return ref.at[pl.ds(layer, 1)]


def _mix_kernel(*refs, layer, casts):
    ins, cast_src, outs, cast_dst, (h_scr, bbuf, dbuf) = _split_refs(refs, 10, 6, casts)
    x_ref, g_ref, w_ref, lng_ref, lnb_ref, ws_ref, bs_ref, conv_ref, wpool_ref, dscale_ref = ins
    g_ref, lng_ref, lnb_ref, dscale_ref = (_layer_row(r, layer) for r in (g_ref, lng_ref, lnb_ref, dscale_ref))
    ws_ref, bs_ref, conv_ref, wpool_ref = (r.at[layer] for r in (ws_ref, bs_ref, conv_ref, wpool_ref))
    pa_ref, pb_ref, pd_ref, q0_ref, q1_ref, q2_ref = outs
    _run_casts(casts, cast_src, cast_dst)
    n_sub = MIX_TILE // MIX_SUB
    first = (pl.program_id(0) == 0) & (pl.program_id(1) == 0)

    @pl.when(first)
    def _():
        bbuf[...] = jnp.zeros(bbuf.shape, F32)
        dbuf[...] = jnp.zeros(dbuf.shape, F32)

    for s in range(n_sub):
        _mix_subtile(s, pl.program_id(1) * n_sub + s, x_ref, g_ref, w_ref, lng_ref, lnb_ref, ws_ref,
                     bs_ref, conv_ref, wpool_ref, dscale_ref, pa_ref, pb_ref, pd_ref,
                     (q0_ref, q1_ref, q2_ref), h_scr.at[s], bbuf.at[s], bbuf.at[(s - 1) % n_sub],
                     dbuf.at[s], dbuf.at[(s - 1) % n_sub])


def _mix_subtile(s, i, x_ref, g_ref, w_ref, lng_ref, lnb_ref, ws_ref, bs_ref, conv_ref, wpool_ref,
                 dscale_ref, pa_ref, pb_ref, pd_ref, q_refs, h_scr, bbuf, bbuf_prev, dbuf, dbuf_prev):
    ts = MIX_SUB
    rows = pl.ds(s * ts, ts)
    x = x_ref[0, rows, :]
    h = _rmsnorm(x, g_ref[...])
    for c in range(D_MODEL // 128):
        h_scr[0, c] = h[:, c * 128:(c + 1) * 128]
    hb = h.astype(BF16)
    zl = _dot(hb, w_ref[:, 0:W_LOCAL + W_QKV])

    u = _gelu_tanh(zl[:, 0:256])
    v = _gelu_tanh(zl[:, 256:512])
    mu = jnp.mean(v, axis=-1, keepdims=True)
    vc = v - mu
    var = jnp.mean(vc * vc, axis=-1, keepdims=True)
    v = vc * lax.rsqrt(var + EPS) * lng_ref[...] + lnb_ref[...]
    row = lax.broadcasted_iota(jnp.int32, (CHUNK, A_GROUPS * CHUNK), 0)
    col = lax.broadcasted_iota(jnp.int32, (CHUNK, A_GROUPS * CHUNK), 1) % CHUNK
    w_all = jnp.concatenate([ws_ref[g] for g in range(A_GROUPS)], axis=1)
    wtril = jnp.where(row >= col, w_all, 0.0).astype(BF16)
    grp = lax.broadcasted_iota(jnp.int32, (CHUNK, W_BRANCH), 1) // (W_BRANCH // A_GROUPS)
    svs = []
    for c in range(ts // CHUNK):
        vch = v[c * CHUNK:(c + 1) * CHUNK]
        stacked = jnp.concatenate(
            [jnp.where(grp == g, vch, 0.0) for g in range(A_GROUPS)], axis=0).astype(BF16)
        svs.append(_dot(wtril, stacked) + bs_ref[...])
    sv = jnp.concatenate(svs, axis=0)
    pa_ref[0, rows, :] = (u * sv).astype(BF16)

    prod = zl[:, 768:1024] * zl[:, 1024:1280]

    bbuf[0:CONV_HALO] = jnp.where(i > 0, bbuf_prev[ts:ts + CONV_HALO], 0.0)
    dbuf[0:POOL_HALO] = jnp.where(i > 0, dbuf_prev[ts:ts + POOL_HALO], 0.0)
    bbuf[CONV_HALO:ts + CONV_HALO] = prod
    cw = conv_ref[...]
    conv = (cw[0:1] * bbuf[pl.ds(CONV_HALO - 2, ts), :]
            + cw[1:2] * bbuf[pl.ds(CONV_HALO - 1, ts), :]
            + cw[2:3] * prod)
    pb_ref[0, rows, :] = (zl[:, 512:768] * conv).astype(BF16)

    dz = zl[:, 1280:1536]
    dbuf[POOL_HALO:ts + POOL_HALO] = dz
    lane = lax.broadcasted_iota(jnp.int32, (ts, 128), 1)
    first_half = lane < 64
    tpos = (lax.broadcasted_iota(jnp.int32, (ts, 128), 0) + (i * ts + 1)).astype(F32)

    def window_sums(e, levels):
        out = []
        s = e
        for k in range(levels):
            s = s + pltpu.roll(s, 1 << k, 0)
            out.append(s[POOL_HALO:])
        return out

    lo = window_sums(dbuf[:, 0:128], 2)
    hi = window_sums(dbuf[:, 128:256], 4)
    pooled_lo = jnp.where(first_half, lo[0], lo[1]) / jnp.minimum(tpos, jnp.where(first_half, 2.0, 4.0))
    pooled_hi = jnp.where(first_half, hi[2], hi[3]) / jnp.minimum(tpos, jnp.where(first_half, 8.0, 16.0))
    y = jnp.concatenate([pooled_lo, pooled_hi], axis=1) - dz
    pd_ref[0, rows, :] = (_dot(y.astype(BF16), wpool_ref[...]) * dscale_ref[...]).astype(BF16)

    sub0 = lax.broadcasted_iota(jnp.int32, (ts, 128), 1) < HEAD_DIM
    for g, (q_ref, dil) in enumerate(zip(q_refs, DILATIONS)):
        n = ts // dil
        if dil == 1:
            z = zl[:, W_LOCAL:W_LOCAL + W_QKV]
        else:
            prev_dil = DILATIONS[g - 1]
            src, n_prev, step = h_scr.at[g - 1], ts // prev_dil, dil // prev_dil
            slabs = [jnp.concatenate([src[c, pl.ds((r % prev_dil) * n_prev + r // prev_dil, n, stride=step), :]
                                      for r in range(dil)], axis=0) for c in range(D_MODEL // 128)]
            if g + 1 < len(DILATIONS):
                for c, slab in enumerate(slabs):
                    h_scr[g, c] = slab
            hp = jnp.concatenate(slabs, axis=1).astype(BF16)
            c0 = W_LOCAL + g * W_QKV
            z = _dot(hp, w_ref[:, c0:c0 + W_QKV])
        q = z[:, 0:256] * (HEAD_DIM ** -0.5)
        pieces = []
        for pr in range(2):
            qp = q[:, pr * 128:(pr + 1) * 128]
            pieces.append(jnp.where(sub0, qp, 0.0))
            pieces.append(jnp.where(sub0, 0.0, qp))
            pieces.append(z[:, 256 + pr * 128:256 + (pr + 1) * 128])
            pieces.append(z[:, 512 + pr * 128:512 + (pr + 1) * 128])
        val = jnp.concatenate(pieces, axis=1).astype(BF16)
        for r in range(dil):
            q_ref[0, r, pl.ds(s * n, n), :] = val[r * n:(r + 1) * n]


def _mix_call(layer, x, g, w_mix, lng, lnb, ws, bs_full, conv, wpool_bd, dscale, cast_jobs=()):
    B, S, D = x.shape
    ts = MIX_TILE
    nt = S // ts
    tok = lambda b, i: (b, i, 0)
    casts, c_ops, c_in, c_shapes, c_out = _cast_plumbing(cast_jobs, B * nt, lambda b, i: b * nt + i)
    out_shape = (
        jax.ShapeDtypeStruct((B, S, W_BRANCH), BF16),
        jax.ShapeDtypeStruct((B, S, W_BRANCH), BF16),
        jax.ShapeDtypeStruct((B, S, W_BRANCH), BF16),
    ) + tuple(jax.ShapeDtypeStruct((B, dil, S // dil, W_QKV_OUT), BF16) for dil in DILATIONS)
    in_specs = [
        pl.BlockSpec((1, ts, D), tok),
    ] + [_const_spec(a.shape) for a in (g, w_mix, lng, lnb, ws, bs_full, conv, wpool_bd, dscale)]
    out_specs = (
        pl.BlockSpec((1, ts, W_BRANCH), tok),
        pl.BlockSpec((1, ts, W_BRANCH), tok),
        pl.BlockSpec((1, ts, W_BRANCH), tok),
    ) + tuple(pl.BlockSpec((1, dil, ts // dil, W_QKV_OUT), lambda b, i: (b, 0, i, 0))
              for dil in DILATIONS)
    return pl.pallas_call(
        functools.partial(_mix_kernel, layer=layer, casts=casts),
        grid=(B, nt),
        in_specs=in_specs + c_in,
        out_specs=out_specs + tuple(c_out),
        out_shape=out_shape + tuple(c_shapes),
        scratch_shapes=[
            pltpu.VMEM((ts // MIX_SUB, len(DILATIONS) - 1, D // 128, MIX_SUB, 128), F32),
            pltpu.VMEM((ts // MIX_SUB, MIX_SUB + CONV_HALO, W_BRANCH), F32),
            pltpu.VMEM((ts // MIX_SUB, MIX_SUB + POOL_HALO, W_BRANCH), F32),
        ],
        compiler_params=pltpu.CompilerParams(
            dimension_semantics=("arbitrary", "arbitrary"), vmem_limit_bytes=VMEM_LIMIT),
        name="mix",
    )(x, g, w_mix, lng, lnb, ws, bs_full, conv, wpool_bd, dscale, *c_ops)


def _attn_kernel(*refs, casts):
    ins, cast_src, (out_ref,), cast_dst, (o_scr, l_scr, t_scr) = _split_refs(refs, 5, 1, casts)
    q0_ref, q1_ref, q2_ref, bcur_ref, bprev_ref = ins
    _run_casts(casts, cast_src, cast_dst)
    q_refs = (q0_ref, q1_ref, q2_ref)
    S = out_ref.shape[1]
    pr = pl.program_id(1)
    nb = S // CHUNK

    def cols(g, c0):
        return q_refs[g][0, :, c0:c0 + 128].reshape(nb, CHUNK, 128)

    jb = ATTN_BLOCKS_PER_PIECE
    sub0 = lax.broadcasted_iota(jnp.int32, (jb, CHUNK, 128), 2) < HEAD_DIM
    qk = lambda a, b: jnp.einsum("jqd,jkd->jqk", a, b, preferred_element_type=F32)
    pv = lambda a, b: jnp.einsum("jqk,jkd->jqd", a, b, preferred_element_type=F32)
    hh0 = 2 * pr

    for g, dil in enumerate(DILATIONS):
        blocks_per_seq = nb // dil
        has_prev = blocks_per_seq > 1
        q = jnp.concatenate([cols(g, 0), cols(g, 128)], axis=1)
        k = cols(g, 256)
        v = cols(g, 384)
        bias_cur = jnp.concatenate([bcur_ref[g, hh0], bcur_ref[g, hh0 + 1]], axis=0)[None]
        if has_prev:
            k = jnp.concatenate([jnp.concatenate([k[:1], k[:-1]], axis=0), k], axis=1)
            v = jnp.concatenate([jnp.concatenate([v[:1], v[:-1]], axis=0), v], axis=1)
            bias_prev = jnp.concatenate([bprev_ref[g, hh0], bprev_ref[g, hh0 + 1]], axis=0)
        for j0 in range(0, nb, jb):
            s = qk(q[j0:j0 + jb], k[j0:j0 + jb])
            s_cur = s[:, :, -CHUNK:] + bias_cur
            s_prev = [None if not has_prev or (j0 + jj) % blocks_per_seq == 0
                      else s[jj, :, :CHUNK] + bias_prev for jj in range(jb)]
            top = jnp.stack([s_cur[jj] if s_prev[jj] is None else jnp.maximum(s_cur[jj], s_prev[jj])
                             for jj in range(jb)])
            row_max = jnp.max(top, axis=-1, keepdims=True)
            e = jnp.exp(s_cur - row_max)
            den = e
            if has_prev:
                e_prev = jnp.stack([jnp.zeros((2 * CHUNK, CHUNK), F32) if s_prev[jj] is None
                                    else jnp.exp(s_prev[jj] - row_max[jj]) for jj in range(jb)])
                den = e + e_prev
                e = jnp.concatenate([e_prev, e], axis=-1)
            den = jnp.sum(den, axis=-1, keepdims=True)
            acc = pv(e.astype(BF16), v[j0:j0 + jb])
            o2 = acc / den
            l2 = jnp.broadcast_to(row_max + jnp.log(den), o2.shape)
            o = jnp.where(sub0, o2[:, :CHUNK], o2[:, CHUNK:])
            lse = jnp.where(sub0, l2[:, :CHUNK], l2[:, CHUNK:])
            two_stage = g >= 2
            prev_dil = DILATIONS[g - 1] if two_stage else 1
            step = dil // prev_dil
            for jj in range(jb):
                r, blk = divmod(j0 + jj, blocks_per_seq)
                start = (r % prev_dil) * (S // prev_dil) + blk * CHUNK * step + r // prev_dil
                rows = pl.ds(start, CHUNK, stride=step) if step > 1 else pl.ds(start, CHUNK)
                if two_stage:
                    t_scr[0, rows, :] = o[jj]
                    t_scr[1, rows, :] = lse[jj]
                else:
                    o_scr[g, rows, :] = o[jj]
                    l_scr[g, rows, :] = lse[jj]
        if two_stage:
            n_prev = S // prev_dil
            for r in range(prev_dil):
                rows = pl.ds(r, n_prev, stride=prev_dil)
                o_scr[g, rows, :] = t_scr[0, r * n_prev:(r + 1) * n_prev, :]
                l_scr[g, rows, :] = t_scr[1, r * n_prev:(r + 1) * n_prev, :]

    rows = 256
    for c in range(S // rows):
        sl = pl.ds(c * rows, rows)
        l0, l1, l2 = l_scr[0, sl, :], l_scr[1, sl, :], l_scr[2, sl, :]
        m = jnp.maximum(jnp.maximum(l0, l1), l2)
        e0, e1, e2 = jnp.exp(l0 - m), jnp.exp(l1 - m), jnp.exp(l2 - m)
        num = e0 * o_scr[0, sl, :] + e1 * o_scr[1, sl, :] + e2 * o_scr[2, sl, :]
        out_ref[0, sl, :] = (num / (e0 + e1 + e2)).astype(BF16)


def _attn_call(qs, bcur, bprev, cast_jobs=()):
    B = qs[0].shape[0]
    S = qs[0].shape[1] * qs[0].shape[2]
    operands = [q.reshape(B, S, W_QKV_OUT) for q in qs]
    spec = pl.BlockSpec((1, S, W_QKV_OUT // 2), lambda b, p: (b, 0, p))
    casts, c_ops, c_in, c_shapes, c_out = _cast_plumbing(cast_jobs, B * 2, lambda b, p: b * 2 + p)
    return pl.pallas_call(
        functools.partial(_attn_kernel, casts=casts),
        grid=(B, 2),
        in_specs=[spec, spec, spec, _const_spec(bcur.shape), _const_spec(bprev.shape)] + c_in,
        out_specs=(pl.BlockSpec((1, S, 128), lambda b, p: (b, 0, p)),) + tuple(c_out),
        out_shape=(jax.ShapeDtypeStruct((B, S, W_BRANCH), BF16),) + tuple(c_shapes),
        scratch_shapes=[
            pltpu.VMEM((3, S, 128), F32),
            pltpu.VMEM((3, S, 128), F32),
            pltpu.VMEM((2, S, 128), F32),
        ],
        compiler_params=pltpu.CompilerParams(
            dimension_semantics=("arbitrary", "arbitrary"), vmem_limit_bytes=VMEM_LIMIT),
        name="attn",
    )(*operands, bcur, bprev, *c_ops)


def _merge_kernel(*refs, layer, casts):
    ins, cast_src, (o_ref,), cast_dst, _ = _split_refs(refs, 9, 1, casts)
    x_ref, pa_ref, pb_ref, pc_ref, pd_ref, g_ref, wg_ref, wout_ref, wo_ref = ins
    g_ref = _layer_row(g_ref, layer)
    _run_casts(casts, cast_src, cast_dst)
    for s in range(TOK_TILE // MERGE_SUB):
        rows = pl.ds(s * MERGE_SUB, MERGE_SUB)
        x = x_ref[rows, :]
        hb = _rmsnorm(x, g_ref[...]).astype(BF16)
        merged = None
        gates = _dot(hb, wg_ref[...])
        for br, p_ref in enumerate((pa_ref, pb_ref, pc_ref, pd_ref)):
            gate = jax.nn.sigmoid(gates[:, br * D_MODEL:(br + 1) * D_MODEL])
            term = gate * _dot(p_ref[rows, :], wout_ref[br])
            merged = term if merged is None else merged + term
        o_ref[rows, :] = x + _dot(merged.astype(BF16), wo_ref[...])


def _merge_call(layer, x, pa, pb, pc, pd, g, w_gate, w_out, w_o, cast_jobs=()):
    T, D = x.shape
    tt = TOK_TILE
    tok = lambda i: (i, 0)
    pspec = pl.BlockSpec((tt, W_BRANCH), tok)
    casts, c_ops, c_in, c_shapes, c_out = _cast_plumbing(cast_jobs, T // tt, lambda i: i)
    return pl.pallas_call(
        functools.partial(_merge_kernel, layer=layer, casts=casts),
        grid=(T // tt,),
        in_specs=[pl.BlockSpec((tt, D), tok), pspec, pspec, pspec, pspec,
                  _const_spec(g.shape), _const_spec(w_gate.shape), _const_spec(w_out.shape),
                  _const_spec(w_o.shape)] + c_in,
        out_specs=(pl.BlockSpec((tt, D), tok),) + tuple(c_out),
        out_shape=(jax.ShapeDtypeStruct((T, D), F32),) + tuple(c_shapes),
        compiler_params=pltpu.CompilerParams(
            dimension_semantics=("arbitrary",), vmem_limit_bytes=VMEM_LIMIT),
        name="merge",
    )(x, pa, pb, pc, pd, g, w_gate, w_out, w_o, *c_ops)


def _ffn_kernel(*refs, layer, final_norm, casts):
    (x_ref, g_ref, w1_ref, w2_ref, fg_ref), cast_src, (o_ref,), cast_dst, _ = _split_refs(refs, 5, 1, casts)
    g_ref = _layer_row(g_ref, layer)
    _run_casts(casts, cast_src, cast_dst)
    for s in range(TOK_TILE // FFN_SUB):
        rows = pl.ds(s * FFN_SUB, FFN_SUB)
        x = x_ref[rows, :]
        hb = _rmsnorm(x, g_ref[...]).astype(BF16)
        a = _dot(hb, w1_ref[...])
        y = x + _dot(jnp.square(jnp.maximum(a, 0.0)).astype(BF16), w2_ref[...])
        if final_norm:
            y = _rmsnorm(y, fg_ref[...])
        o_ref[rows, :] = y


def _ffn_call(layer, x, g, w1, w2, final_g, final_norm, cast_jobs=()):
    T, D = x.shape
    tt = TOK_TILE
    tok = lambda i: (i, 0)
    casts, c_ops, c_in, c_shapes, c_out = _cast_plumbing(cast_jobs, T // tt, lambda i: i)
    return pl.pallas_call(
        functools.partial(_ffn_kernel, layer=layer, final_norm=final_norm, casts=casts),
        grid=(T // tt,),
        in_specs=[pl.BlockSpec((tt, D), tok), _const_spec(g.shape), _const_spec(w1.shape),
                  _const_spec(w2.shape), _const_spec((1, D))] + c_in,
        out_specs=(pl.BlockSpec((tt, D), tok),) + tuple(c_out),
        out_shape=(jax.ShapeDtypeStruct((T, D), F32),) + tuple(c_shapes),
        compiler_params=pltpu.CompilerParams(
            dimension_semantics=("arbitrary",), vmem_limit_bytes=VMEM_LIMIT),
        name="ffn",
    )(x, g, w1, w2, final_g, *c_ops)


def _t5_bucket_np(dist):
    max_exact = N_BUCKETS // 2
    d_f = np.maximum(dist, 1).astype(np.float32)
    ratio = np.log(d_f / np.float32(max_exact)) / np.float32(math.log(MAX_DISTANCE / max_exact))
    large = max_exact + (ratio * np.float32(N_BUCKETS - max_exact)).astype(np.int32)
    large = np.minimum(large, N_BUCKETS - 1)
    return np.where(dist < max_exact, dist, large)


def _bucket_indices():
    qi = np.arange(CHUNK)[:, None]
    ki = np.arange(CHUNK)[None, :]
    d_cur = qi - ki
    d_prev = qi + CHUNK - ki
    cur = [np.where(d_cur >= 0, _t5_bucket_np(np.clip(d_cur, 0, None) * dil), -1) for dil in DILATIONS]
    prev = [np.where(d_prev <= CHUNK, _t5_bucket_np(d_prev * dil), -1) for dil in DILATIONS[:2]]
    return np.stack(cur).astype(np.int32), np.stack(prev).astype(np.int32)


def _bias_kernel(tab_ref, icur_ref, iprev_ref, bcur_ref, bprev_ref):
    for idx_ref, out_ref in ((icur_ref, bcur_ref), (iprev_ref, bprev_ref)):
        for g in range(idx_ref.shape[0]):
            idx = idx_ref[g]
            accs = [jnp.full((CHUNK, CHUNK), NEG_BIG, F32) for _ in range(HEADS_PER_GROUP)]
            for b in range(N_BUCKETS):
                hit = idx == b
                for h in range(HEADS_PER_GROUP):
                    accs[h] = jnp.where(hit, tab_ref[b, g * HEADS_PER_GROUP + h], accs[h])
            for h in range(HEADS_PER_GROUP):
                out_ref[g, h] = accs[h]


def _bias_tables(rel_bias):
    icur, iprev = _bucket_indices()
    vm = pl.BlockSpec(memory_space=pltpu.VMEM)
    return pl.pallas_call(
        _bias_kernel,
        in_specs=[pl.BlockSpec(memory_space=pltpu.SMEM), vm, vm],
        out_specs=(vm, vm),
        out_shape=(jax.ShapeDtypeStruct((3, HEADS_PER_GROUP, CHUNK, CHUNK), F32),
                   jax.ShapeDtypeStruct((2, HEADS_PER_GROUP, CHUNK, CHUNK), F32)),
        name="bias",
    )(rel_bias, jnp.asarray(icur), jnp.asarray(iprev))


def kernel(x, norm_mix_g, w_in, a_ln_g, a_ln_b, a_ws, a_bs, w_a_out, b_conv, w_b_out, rel_bias,
           w_c_out, d_w, d_scale, w_d_out, w_o, norm_ff_g, w_ff1, w_ff2, final_g):
    B, S, D = x.shape
    depth = w_in.shape[0]
    bcur, bprev = _bias_tables(rel_bias)
    w_outs = (w_a_out, w_b_out, w_c_out, w_d_out)
    w_mix_b = jnp.concatenate([w_in[0][:, a:b] for a, b in _MIX_COLS], axis=1).astype(BF16)
    bs_full = jnp.repeat(jnp.swapaxes(a_bs, 1, 2), W_BRANCH // A_GROUPS, axis=2)
    eye = jnp.eye(POOL_GROUPS, dtype=d_w.dtype)
    wpool_bd = jnp.einsum("lgde,gh->lgdhe", d_w, eye).reshape(depth, W_BRANCH, W_BRANCH).astype(BF16)
    final_row = final_g[None]
    T = B * S
    for l in range(depth):
        more = l + 1 < depth

        mix_jobs = (("plain", 0, (w_o,)), ("w_out", 0, w_outs)) if l == 0 else ()
        pa, pb, pd, q0, q1, q2, *cast = _mix_call(
            l, x, norm_mix_g, w_mix_b, a_ln_g, a_ln_b, a_ws, bs_full, b_conv, wpool_bd, d_scale,
            cast_jobs=mix_jobs)
        if l == 0:
            w_o_b, w_out_b = cast

        attn_jobs = (("plain", 0, (w_ff1,)), ("plain", 0, (w_ff2,)), ("gate", 0, (w_in,))) if l == 0 else ()
        pc, *cast = _attn_call((q0, q1, q2), bcur, bprev, cast_jobs=attn_jobs)
        if l == 0:
            w_ff1_b, w_ff2_b, w_gate_b = cast

        merge_jobs = (("w_in", l + 1, (w_in,)),) if more else ()
        x2, *cast = _merge_call(l, x.reshape(T, D), pa.reshape(T, -1), pb.reshape(T, -1),
                                pc.reshape(T, -1), pd.reshape(T, -1), norm_mix_g,
                                w_gate_b, w_out_b, w_o_b, cast_jobs=merge_jobs)
        if more:
            w_mix_b, w_gate_b = cast

        ffn_jobs = (("plain", l + 1, (w_ff1,)), ("plain", l + 1, (w_ff2,)), ("plain", l + 1, (w_o,)),
                    ("w_out", l + 1, w_outs)) if more else ()
        x2, *cast = _ffn_call(l, x2, norm_ff_g, w_ff1_b, w_ff2_b, final_row,
                              final_norm=not more, cast_jobs=ffn_jobs)
        if more:
            w_ff1_b, w_ff2_b, w_o_b, w_out_b = cast
        x = x2.reshape(B, S, D)
    return x
```

```python
import functools
import math

import jax
import jax.numpy as jnp
import numpy as np
from jax import lax
from jax.experimental import pallas as pl
from jax.experimental.pallas import tpu as pltpu

F32 = jnp.float32
BF16 = jnp.bfloat16

D_MODEL = 1024
W_BRANCH = 256
A_GROUPS = 4
CHUNK = 128
CONV_WIDTH = 3
DILATIONS = (1, 4, 16)
WINDOWS = (128, 512, 2048)
assert all(w == CHUNK * d for w, d in zip(WINDOWS, DILATIONS))
HEADS_PER_GROUP = 4
HEAD_DIM = 64
POOL_WINDOWS = (2, 4, 8, 16)
POOL_GROUPS = len(POOL_WINDOWS)
POOL_HALO = 16
CONV_HALO = 8
N_BRANCH = 4
N_BUCKETS = 32
MAX_DISTANCE = 2048
EPS = 1e-6
NEG_BIG = -1e30
COL_Q = 5 * W_BRANCH
COL_K = COL_Q + 3 * W_BRANCH
COL_V = COL_K + 3 * W_BRANCH
COL_D = COL_V + 3 * W_BRANCH
COL_GATE = COL_D + W_BRANCH
W_LOCAL = 6 * W_BRANCH
W_QKV = 3 * W_BRANCH
W_QKV_OUT = 4 * W_BRANCH

MIX_TILE = 1024
MIX_SUB = 256
TOK_TILE = 1024
MERGE_SUB = 256
FFN_SUB = 512
ATTN_BLOCKS_PER_PIECE = 8
VMEM_LIMIT = 56 * 1024 * 1024


def _rmsnorm(x, g):
    return x * lax.rsqrt(jnp.mean(x * x, axis=-1, keepdims=True) + EPS) * g


def _gelu_tanh(x):
    c = math.sqrt(2.0 / math.pi)
    return x * (0.5 * (1.0 + jnp.tanh(c * (x + 0.044715 * (x * x * x)))))


def _dot(a, b):
    return jnp.dot(a, b, preferred_element_type=F32)


def _const_spec(shape):
    n = len(shape)
    return pl.BlockSpec(shape, lambda *_: (0,) * n, pipeline_mode=pl.Buffered(1))


_CAST_ARITY = {"plain": (1, 1), "w_in": (1, 2), "gate": (1, 1), "w_out": (N_BRANCH, 1)}
_MIX_COLS = ((0, COL_Q), (COL_D, COL_GATE)) + tuple(
    (col + g * W_BRANCH, col + (g + 1) * W_BRANCH)
    for g in range(len(DILATIONS)) for col in (COL_Q, COL_K, COL_V))


def _cast_plumbing(jobs, n_steps, step_of):
    kinds, operands, in_specs, out_shapes, out_specs = [], [], [], [], []
    for kind, layer, srcs in jobs:
        kinds.append(kind)
        R, C = srcs[0].shape[1:]
        rows = R // n_steps
        assert rows * n_steps == R and rows % 16 == 0, (kind, R, n_steps)
        for src in srcs:
            operands.append(src)
            in_specs.append(pl.BlockSpec((1, rows, C), lambda *g, layer=layer: (layer, step_of(*g), 0)))
        row_blk = lambda *g: (step_of(*g), 0)
        if kind == "w_out":
            out_shapes.append(jax.ShapeDtypeStruct((len(srcs), R, C), BF16))
            out_specs.append(pl.BlockSpec((len(srcs), rows, C), lambda *g: (0, step_of(*g), 0)))
        elif kind in ("w_in", "gate"):
            widths = (C - COL_GATE,) if kind == "gate" else (sum(b - a for a, b in _MIX_COLS), C - COL_GATE)
            for width in widths:
                out_shapes.append(jax.ShapeDtypeStruct((R, width), BF16))
                out_specs.append(pl.BlockSpec((rows, width), row_blk))
        else:
            out_shapes.append(jax.ShapeDtypeStruct((R, C), BF16))
            out_specs.append(pl.BlockSpec((rows, C), row_blk))
    return tuple(kinds), operands, in_specs, out_shapes, out_specs


def _split_refs(refs, n_in, n_out, kinds):
    n_ci = sum(_CAST_ARITY[k][0] for k in kinds)
    n_co = sum(_CAST_ARITY[k][1] for k in kinds)
    a, b, c = n_in, n_in + n_ci, n_in + n_ci + n_out
    return refs[:a], refs[a:b], refs[b:c], refs[c:c + n_co], refs[c + n_co:]


def _run_casts(kinds, src_refs, dst_refs):
    i = o = 0
    for kind in kinds:
        n_i, n_o = _CAST_ARITY[kind]
        srcs, dsts = src_refs[i:i + n_i], dst_refs[o:o + n_o]
        i, o = i + n_i, o + n_o
        if kind == "plain":
            dsts[0][...] = srcs[0][0].astype(BF16)
        elif kind == "w_out":
            for k, src in enumerate(srcs):
                dsts[0][k] = src[0].astype(BF16)
        elif kind == "gate":
            dsts[0][...] = srcs[0][0, :, COL_GATE:].astype(BF16)
        else:
            c = 0
            for a, b in _MIX_COLS:
                dsts[0][:, c:c + b - a] = srcs[0][0, :, a:b].astype(BF16)
                c += b - a
            dsts[1][...] = srcs[0][0, :, COL_GATE:].astype(BF16)


def _layer_row(ref, layer):
    return ref.at[pl.ds(layer, 1)]


def _mix_kernel(*refs, layer, casts):
    ins, cast_src, outs, cast_dst, (h_scr, bbuf, dbuf) = _split_refs(refs, 10, 6, casts)
    x_ref, g_ref, w_ref, lng_ref, lnb_ref, ws_ref, bs_ref, conv_ref, wpool_ref, dscale_ref = ins
    g_ref, lng_ref, lnb_ref, dscale_ref = (_layer_row(r, layer) for r in (g_ref, lng_ref, lnb_ref, dscale_ref))
    ws_ref, bs_ref, conv_ref, wpool_ref = (r.at[layer] for r in (ws_ref, bs_ref, conv_ref, wpool_ref))
    pa_ref, pb_ref, pd_ref, q0_ref, q1_ref, q2_ref = outs
    _run_casts(casts, cast_src, cast_dst)
    n_sub = MIX_TILE // MIX_SUB
    first = (pl.program_id(0) == 0) & (pl.program_id(1) == 0)

    @pl.when(first)
    def _():
        bbuf[...] = jnp.zeros(bbuf.shape, F32)
        dbuf[...] = jnp.zeros(dbuf.shape, F32)

    for s in range(n_sub):
        _mix_subtile(s, pl.program_id(1) * n_sub + s, x_ref, g_ref, w_ref, lng_ref, lnb_ref, ws_ref,
                     bs_ref, conv_ref, wpool_ref, dscale_ref, pa_ref, pb_ref, pd_ref,
                     (q0_ref, q1_ref, q2_ref), h_scr.at[s], bbuf.at[s], bbuf.at[(s - 1) % n_sub],
                     dbuf.at[s], dbuf.at[(s - 1) % n_sub])


def _mix_subtile(s, i, x_ref, g_ref, w_ref, lng_ref, lnb_ref, ws_ref, bs_ref, conv_ref, wpool_ref,
                 dscale_ref, pa_ref, pb_ref, pd_ref, q_refs, h_scr, bbuf, bbuf_prev, dbuf, dbuf_prev):
    ts = MIX_SUB
    rows = pl.ds(s * ts, ts)
    x = x_ref[0, rows, :]
    h = _rmsnorm(x, g_ref[...])
    for c in range(D_MODEL // 128):
        h_scr[0, c] = h[:, c * 128:(c + 1) * 128]
    hb = h.astype(BF16)
    zl = _dot(hb, w_ref[:, 0:W_LOCAL + W_QKV])

    sub0 = lax.broadcasted_iota(jnp.int32, (ts, 128), 1) < HEAD_DIM
    for g, (q_ref, dil) in enumerate(zip(q_refs, DILATIONS)):
        n = ts // dil
        if dil == 1:
            z = zl[:, W_LOCAL:W_LOCAL + W_QKV]
        else:
            prev_dil = DILATIONS[g - 1]
            src, n_prev, step = h_scr.at[g - 1], ts // prev_dil, dil // prev_dil
            slabs = [jnp.concatenate([src[c, pl.ds((r % prev_dil) * n_prev + r // prev_dil, n, stride=step), :]
                                      for r in range(dil)], axis=0) for c in range(D_MODEL // 128)]
            if g + 1 < len(DILATIONS):
                for c, slab in enumerate(slabs):
                    h_scr[g, c] = slab
            hp = jnp.concatenate(slabs, axis=1).astype(BF16)
            c0 = W_LOCAL + g * W_QKV
            z = _dot(hp, w_ref[:, c0:c0 + W_QKV])
        q = z[:, 0:256] * (HEAD_DIM ** -0.5)
        pieces = []
        for pr in range(2):
            qp = q[:, pr * 128:(pr + 1) * 128]
            pieces.append(jnp.where(sub0, qp, 0.0))
            pieces.append(jnp.where(sub0, 0.0, qp))
            pieces.append(z[:, 256 + pr * 128:256 + (pr + 1) * 128])
            pieces.append(z[:, 512 + pr * 128:512 + (pr + 1) * 128])
        val = jnp.concatenate(pieces, axis=1).astype(BF16)
        for r in range(dil):
            q_ref[0, r, pl.ds(s * n, n), :] = val[r * n:(r + 1) * n]

    u = _gelu_tanh(zl[:, 0:256])
    v = _gelu_tanh(zl[:, 256:512])
    mu = jnp.mean(v, axis=-1, keepdims=True)
    vc = v - mu
    var = jnp.mean(vc * vc, axis=-1, keepdims=True)
    v = vc * lax.rsqrt(var + EPS) * lng_ref[...] + lnb_ref[...]
    row = lax.broadcasted_iota(jnp.int32, (CHUNK, A_GROUPS * CHUNK), 0)
    col = lax.broadcasted_iota(jnp.int32, (CHUNK, A_GROUPS * CHUNK), 1) % CHUNK
    w_all = jnp.concatenate([ws_ref[g] for g in range(A_GROUPS)], axis=1)
    wtril = jnp.where(row >= col, w_all, 0.0).astype(BF16)
    grp = lax.broadcasted_iota(jnp.int32, (CHUNK, W_BRANCH), 1) // (W_BRANCH // A_GROUPS)
    svs = []
    for c in range(ts // CHUNK):
        vch = v[c * CHUNK:(c + 1) * CHUNK]
        stacked = jnp.concatenate(
            [jnp.where(grp == g, vch, 0.0) for g in range(A_GROUPS)], axis=0).astype(BF16)
        svs.append(_dot(wtril, stacked) + bs_ref[...])
    sv = jnp.concatenate(svs, axis=0)
    pa_ref[0, rows, :] = (u * sv).astype(BF16)

    prod = zl[:, 768:1024] * zl[:, 1024:1280]

    bbuf[0:CONV_HALO] = jnp.where(i > 0, bbuf_prev[ts:ts + CONV_HALO], 0.0)
    dbuf[0:POOL_HALO] = jnp.where(i > 0, dbuf_prev[ts:ts + POOL_HALO], 0.0)
    bbuf[CONV_HALO:ts + CONV_HALO] = prod
    cw = conv_ref[...]
    conv = (cw[0:1] * bbuf[pl.ds(CONV_HALO - 2, ts), :]
            + cw[1:2] * bbuf[pl.ds(CONV_HALO - 1, ts), :]
            + cw[2:3] * prod)
    pb_ref[0, rows, :] = (zl[:, 512:768] * conv).astype(BF16)

    dz = zl[:, 1280:1536]
    dbuf[POOL_HALO:ts + POOL_HALO] = dz
    lane = lax.broadcasted_iota(jnp.int32, (ts, 128), 1)
    first_half = lane < 64
    tpos = (lax.broadcasted_iota(jnp.int32, (ts, 128), 0) + (i * ts + 1)).astype(F32)

    def window_sums(e, levels):
        out = []
        s = e
        for k in range(levels):
            s = s + pltpu.roll(s, 1 << k, 0)
            out.append(s[POOL_HALO:])
        return out

    lo = window_sums(dbuf[:, 0:128], 2)
    hi = window_sums(dbuf[:, 128:256], 4)
    pooled_lo = jnp.where(first_half, lo[0], lo[1]) / jnp.minimum(tpos, jnp.where(first_half, 2.0, 4.0))
    pooled_hi = jnp.where(first_half, hi[2], hi[3]) / jnp.minimum(tpos, jnp.where(first_half, 8.0, 16.0))
    y = jnp.concatenate([pooled_lo, pooled_hi], axis=1) - dz
    pd_ref[0, rows, :] = (_dot(y.astype(BF16), wpool_ref[...]) * dscale_ref[...]).astype(BF16)


def _mix_call(layer, x, g, w_mix, lng, lnb, ws, bs_full, conv, wpool_bd, dscale, cast_jobs=()):
    B, S, D = x.shape
    ts = MIX_TILE
    nt = S // ts
    tok = lambda b, i: (b, i, 0)
    casts, c_ops, c_in, c_shapes, c_out = _cast_plumbing(cast_jobs, B * nt, lambda b, i: b * nt + i)
    out_shape = (
        jax.ShapeDtypeStruct((B, S, W_BRANCH), BF16),
        jax.ShapeDtypeStruct((B, S, W_BRANCH), BF16),
        jax.ShapeDtypeStruct((B, S, W_BRANCH), BF16),
    ) + tuple(jax.ShapeDtypeStruct((B, dil, S // dil, W_QKV_OUT), BF16) for dil in DILATIONS)
    in_specs = [
        pl.BlockSpec((1, ts, D), tok),
    ] + [_const_spec(a.shape) for a in (g, w_mix, lng, lnb, ws, bs_full, conv, wpool_bd, dscale)]
    out_specs = (
        pl.BlockSpec((1, ts, W_BRANCH), tok),
        pl.BlockSpec((1, ts, W_BRANCH), tok),
        pl.BlockSpec((1, ts, W_BRANCH), tok),
    ) + tuple(pl.BlockSpec((1, dil, ts // dil, W_QKV_OUT), lambda b, i: (b, 0, i, 0))
              for dil in DILATIONS)
    return pl.pallas_call(
        functools.partial(_mix_kernel, layer=layer, casts=casts),
        grid=(B, nt),
        in_specs=in_specs + c_in,
        out_specs=out_specs + tuple(c_out),
        out_shape=out_shape + tuple(c_shapes),
        scratch_shapes=[
            pltpu.VMEM((ts // MIX_SUB, len(DILATIONS) - 1, D // 128, MIX_SUB, 128), F32),
            pltpu.VMEM((ts // MIX_SUB, MIX_SUB + CONV_HALO, W_BRANCH), F32),
            pltpu.VMEM((ts // MIX_SUB, MIX_SUB + POOL_HALO, W_BRANCH), F32),
        ],
        compiler_params=pltpu.CompilerParams(
            dimension_semantics=("arbitrary", "arbitrary"), vmem_limit_bytes=VMEM_LIMIT),
        name="mix",
    )(x, g, w_mix, lng, lnb, ws, bs_full, conv, wpool_bd, dscale, *c_ops)


def _attn_kernel(*refs, casts):
    ins, cast_src, (out_ref,), cast_dst, (o_scr, l_scr, t_scr) = _split_refs(refs, 5, 1, casts)
    q0_ref, q1_ref, q2_ref, bcur_ref, bprev_ref = ins
    _run_casts(casts, cast_src, cast_dst)
    q_refs = (q0_ref, q1_ref, q2_ref)
    S = out_ref.shape[1]
    pr = pl.program_id(1)
    nb = S // CHUNK

    def cols(g, c0):
        return q_refs[g][0, :, c0:c0 + 128].reshape(nb, CHUNK, 128)

    jb = ATTN_BLOCKS_PER_PIECE
    sub0 = lax.broadcasted_iota(jnp.int32, (jb, CHUNK, 128), 2) < HEAD_DIM
    qk = lambda a, b: jnp.einsum("jqd,jkd->jqk", a, b, preferred_element_type=F32)
    pv = lambda a, b: jnp.einsum("jqk,jkd->jqd", a, b, preferred_element_type=F32)
    hh0 = 2 * pr

    for g, dil in enumerate(DILATIONS):
        blocks_per_seq = nb // dil
        has_prev = blocks_per_seq > 1
        q = jnp.concatenate([cols(g, 0), cols(g, 128)], axis=1)
        k = cols(g, 256)
        v = cols(g, 384)
        bias_cur = jnp.concatenate([bcur_ref[g, hh0], bcur_ref[g, hh0 + 1]], axis=0)[None]
        if has_prev:
            k = jnp.concatenate([jnp.concatenate([k[:1], k[:-1]], axis=0), k], axis=1)
            v = jnp.concatenate([jnp.concatenate([v[:1], v[:-1]], axis=0), v], axis=1)
            bias_prev = jnp.concatenate([bprev_ref[g, hh0], bprev_ref[g, hh0 + 1]], axis=0)
        for j0 in range(0, nb, jb):
            s = qk(q[j0:j0 + jb], k[j0:j0 + jb])
            s_cur = s[:, :, -CHUNK:] + bias_cur
            s_prev = [None if not has_prev or (j0 + jj) % blocks_per_seq == 0
                      else s[jj, :, :CHUNK] + bias_prev for jj in range(jb)]
            top = jnp.stack([s_cur[jj] if s_prev[jj] is None else jnp.maximum(s_cur[jj], s_prev[jj])
                             for jj in range(jb)])
            row_max = jnp.max(top, axis=-1, keepdims=True)
            e = jnp.exp(s_cur - row_max)
            den = e
            if has_prev:
                e_prev = jnp.stack([jnp.zeros((2 * CHUNK, CHUNK), F32) if s_prev[jj] is None
                                    else jnp.exp(s_prev[jj] - row_max[jj]) for jj in range(jb)])
                den = e + e_prev
                e = jnp.concatenate([e_prev, e], axis=-1)
            den = jnp.sum(den, axis=-1, keepdims=True)
            acc = pv(e.astype(BF16), v[j0:j0 + jb])
            o2 = acc / den
            l2 = jnp.broadcast_to(row_max + jnp.log(den), o2.shape)
            o = jnp.where(sub0, o2[:, :CHUNK], o2[:, CHUNK:])
            lse = jnp.where(sub0, l2[:, :CHUNK], l2[:, CHUNK:])
            two_stage = g >= 2
            prev_dil = DILATIONS[g - 1] if two_stage else 1
            step = dil // prev_dil
            for jj in range(jb):
                r, blk = divmod(j0 + jj, blocks_per_seq)
                start = (r % prev_dil) * (S // prev_dil) + blk * CHUNK * step + r // prev_dil
                rows = pl.ds(start, CHUNK, stride=step) if step > 1 else pl.ds(start, CHUNK)
                if two_stage:
                    t_scr[0, rows, :] = o[jj]
                    t_scr[1, rows, :] = lse[jj]
                else:
                    o_scr[g, rows, :] = o[jj]
                    l_scr[g, rows, :] = lse[jj]
        if two_stage:
            n_prev = S // prev_dil
            for r in range(prev_dil):
                rows = pl.ds(r, n_prev, stride=prev_dil)
                o_scr[g, rows, :] = t_scr[0, r * n_prev:(r + 1) * n_prev, :]
                l_scr[g, rows, :] = t_scr[1, r * n_prev:(r + 1) * n_prev, :]

    rows = 256
    for c in range(S // rows):
        sl = pl.ds(c * rows, rows)
        l0, l1, l2 = l_scr[0, sl, :], l_scr[1, sl, :], l_scr[2, sl, :]
        m = jnp.maximum(jnp.maximum(l0, l1), l2)
        e0, e1, e2 = jnp.exp(l0 - m), jnp.exp(l1 - m), jnp.exp(l2 - m)
        num = e0 * o_scr[0, sl, :] + e1 * o_scr[1, sl, :] + e2 * o_scr[2, sl, :]
        out_ref[0, sl, :] = (num / (e0 + e1 + e2)).astype(BF16)


def _attn_call(qs, bcur, bprev, cast_jobs=()):
    B = qs[0].shape[0]
    S = qs[0].shape[1] * qs[0].shape[2]
    operands = [q.reshape(B, S, W_QKV_OUT) for q in qs]
    spec = pl.BlockSpec((1, S, W_QKV_OUT // 2), lambda b, p: (b, 0, p))
    casts, c_ops, c_in, c_shapes, c_out = _cast_plumbing(cast_jobs, B * 2, lambda b, p: b * 2 + p)
    return pl.pallas_call(
        functools.partial(_attn_kernel, casts=casts),
        grid=(B, 2),
        in_specs=[spec, spec, spec, _const_spec(bcur.shape), _const_spec(bprev.shape)] + c_in,
        out_specs=(pl.BlockSpec((1, S, 128), lambda b, p: (b, 0, p)),) + tuple(c_out),
        out_shape=(jax.ShapeDtypeStruct((B, S, W_BRANCH), BF16),) + tuple(c_shapes),
        scratch_shapes=[
            pltpu.VMEM((3, S, 128), F32),
            pltpu.VMEM((3, S, 128), F32),
            pltpu.VMEM((2, S, 128), F32),
        ],
        compiler_params=pltpu.CompilerParams(
            dimension_semantics=("arbitrary", "arbitrary"), vmem_limit_bytes=VMEM_LIMIT),
        name="attn",
    )(*operands, bcur, bprev, *c_ops)


def _merge_kernel(*refs, layer, casts):
    ins, cast_src, (o_ref,), cast_dst, _ = _split_refs(refs, 9, 1, casts)
    x_ref, pa_ref, pb_ref, pc_ref, pd_ref, g_ref, wg_ref, wout_ref, wo_ref = ins
    g_ref = _layer_row(g_ref, layer)
    _run_casts(casts, cast_src, cast_dst)
    for s in range(TOK_TILE // MERGE_SUB):
        rows = pl.ds(s * MERGE_SUB, MERGE_SUB)
        x = x_ref[rows, :]
        hb = _rmsnorm(x, g_ref[...]).astype(BF16)
        merged = None
        gates = _dot(hb, wg_ref[...])
        for br, p_ref in enumerate((pa_ref, pb_ref, pc_ref, pd_ref)):
            gate = jax.nn.sigmoid(gates[:, br * D_MODEL:(br + 1) * D_MODEL])
            term = gate * _dot(p_ref[rows, :], wout_ref[br])
            merged = term if merged is None else merged + term
        o_ref[rows, :] = x + _dot(merged.astype(BF16), wo_ref[...])


def _merge_call(layer, x, pa, pb, pc, pd, g, w_gate, w_out, w_o, cast_jobs=()):
    T, D = x.shape
    tt = TOK_TILE
    tok = lambda i: (i, 0)
    pspec = pl.BlockSpec((tt, W_BRANCH), tok)
    casts, c_ops, c_in, c_shapes, c_out = _cast_plumbing(cast_jobs, T // tt, lambda i: i)
    return pl.pallas_call(
        functools.partial(_merge_kernel, layer=layer, casts=casts),
        grid=(T // tt,),
        in_specs=[pl.BlockSpec((tt, D), tok), pspec, pspec, pspec, pspec,
                  _const_spec(g.shape), _const_spec(w_gate.shape), _const_spec(w_out.shape),
                  _const_spec(w_o.shape)] + c_in,
        out_specs=(pl.BlockSpec((tt, D), tok),) + tuple(c_out),
        out_shape=(jax.ShapeDtypeStruct((T, D), F32),) + tuple(c_shapes),
        compiler_params=pltpu.CompilerParams(
            dimension_semantics=("arbitrary",), vmem_limit_bytes=VMEM_LIMIT),
        name="merge",
    )(x, pa, pb, pc, pd, g, w_gate, w_out, w_o, *c_ops)


def _ffn_kernel(*refs, layer, final_norm, casts):
    (x_ref, g_ref, w1_ref, w2_ref, fg_ref), cast_src, (o_ref,), cast_dst, _ = _split_refs(refs, 5, 1, casts)
    g_ref = _layer_row(g_ref, layer)
    _run_casts(casts, cast_src, cast_dst)
    for s in range(TOK_TILE // FFN_SUB):
        rows = pl.ds(s * FFN_SUB, FFN_SUB)
        x = x_ref[rows, :]
        hb = _rmsnorm(x, g_ref[...]).astype(BF16)
        a = _dot(hb, w1_ref[...])
        y = x + _dot(jnp.square(jnp.maximum(a, 0.0)).astype(BF16), w2_ref[...])
        if final_norm:
            y = _rmsnorm(y, fg_ref[...])
        o_ref[rows, :] = y


def _ffn_call(layer, x, g, w1, w2, final_g, final_norm, cast_jobs=()):
    T, D = x.shape
    tt = TOK_TILE
    tok = lambda i: (i, 0)
    casts, c_ops, c_in, c_shapes, c_out = _cast_plumbing(cast_jobs, T // tt, lambda i: i)
    return pl.pallas_call(
        functools.partial(_ffn_kernel, layer=layer, final_norm=final_norm, casts=casts),
        grid=(T // tt,),
        in_specs=[pl.BlockSpec((tt, D), tok), _const_spec(g.shape), _const_spec(w1.shape),
                  _const_spec(w2.shape), _const_spec((1, D))] + c_in,
        out_specs=(pl.BlockSpec((tt, D), tok),) + tuple(c_out),
        out_shape=(jax.ShapeDtypeStruct((T, D), F32),) + tuple(c_shapes),
        compiler_params=pltpu.CompilerParams(
            dimension_semantics=("arbitrary",), vmem_limit_bytes=VMEM_LIMIT),
        name="ffn",
    )(x, g, w1, w2, final_g, *c_ops)


def _t5_bucket_np(dist):
    max_exact = N_BUCKETS // 2
    d_f = np.maximum(dist, 1).astype(np.float32)
    ratio = np.log(d_f / np.float32(max_exact)) / np.float32(math.log(MAX_DISTANCE / max_exact))
    large = max_exact + (ratio * np.float32(N_BUCKETS - max_exact)).astype(np.int32)
    large = np.minimum(large, N_BUCKETS - 1)
    return np.where(dist < max_exact, dist, large)


def _bucket_indices():
    qi = np.arange(CHUNK)[:, None]
    ki = np.arange(CHUNK)[None, :]
    d_cur = qi - ki
    d_prev = qi + CHUNK - ki
    cur = [np.where(d_cur >= 0, _t5_bucket_np(np.clip(d_cur, 0, None) * dil), -1) for dil in DILATIONS]
    prev = [np.where(d_prev <= CHUNK, _t5_bucket_np(d_prev * dil), -1) for dil in DILATIONS[:2]]
    return np.stack(cur).astype(np.int32), np.stack(prev).astype(np.int32)


def _bias_kernel(tab_ref, icur_ref, iprev_ref, bcur_ref, bprev_ref):
    for idx_ref, out_ref in ((icur_ref, bcur_ref), (iprev_ref, bprev_ref)):
        for g in range(idx_ref.shape[0]):
            idx = idx_ref[g]
            accs = [jnp.full((CHUNK, CHUNK), NEG_BIG, F32) for _ in range(HEADS_PER_GROUP)]
            for b in range(N_BUCKETS):
                hit = idx == b
                for h in range(HEADS_PER_GROUP):
                    accs[h] = jnp.where(hit, tab_ref[b, g * HEADS_PER_GROUP + h], accs[h])
            for h in range(HEADS_PER_GROUP):
                out_ref[g, h] = accs[h]


def _bias_tables(rel_bias):
    icur, iprev = _bucket_indices()
    vm = pl.BlockSpec(memory_space=pltpu.VMEM)
    return pl.pallas_call(
        _bias_kernel,
        in_specs=[pl.BlockSpec(memory_space=pltpu.SMEM), vm, vm],
        out_specs=(vm, vm),
        out_shape=(jax.ShapeDtypeStruct((3, HEADS_PER_GROUP, CHUNK, CHUNK), F32),
                   jax.ShapeDtypeStruct((2, HEADS_PER_GROUP, CHUNK, CHUNK), F32)),
        name="bias",
    )(rel_bias, jnp.asarray(icur), jnp.asarray(iprev))


def kernel(x, norm_mix_g, w_in, a_ln_g, a_ln_b, a_ws, a_bs, w_a_out, b_conv, w_b_out, rel_bias,
           w_c_out, d_w, d_scale, w_d_out, w_o, norm_ff_g, w_ff1, w_ff2, final_g):
    B, S, D = x.shape
    depth = w_in.shape[0]
    bcur, bprev = _bias_tables(rel_bias)
    w_outs = (w_a_out, w_b_out, w_c_out, w_d_out)
    w_mix_b = jnp.concatenate([w_in[0][:, a:b] for a, b in _MIX_COLS], axis=1).astype(BF16)
    bs_full = jnp.repeat(jnp.swapaxes(a_bs, 1, 2), W_BRANCH // A_GROUPS, axis=2)
    eye = jnp.eye(POOL_GROUPS, dtype=d_w.dtype)
    wpool_bd = jnp.einsum("lgde,gh->lgdhe", d_w, eye).reshape(depth, W_BRANCH, W_BRANCH).astype(BF16)
    final_row = final_g[None]
    T = B * S
    for l in range(depth):
        more = l + 1 < depth

        mix_jobs = (("plain", 0, (w_o,)), ("w_out", 0, w_outs)) if l == 0 else ()
        pa, pb, pd, q0, q1, q2, *cast = _mix_call(
            l, x, norm_mix_g, w_mix_b, a_ln_g, a_ln_b, a_ws, bs_full, b_conv, wpool_bd, d_scale,
            cast_jobs=mix_jobs)
        if l == 0:
            w_o_b, w_out_b = cast

        attn_jobs = (("plain", 0, (w_ff1,)), ("plain", 0, (w_ff2,)), ("gate", 0, (w_in,))) if l == 0 else ()
        pc, *cast = _attn_call((q0, q1, q2), bcur, bprev, cast_jobs=attn_jobs)
        if l == 0:
            w_ff1_b, w_ff2_b, w_gate_b = cast

        merge_jobs = (("w_in", l + 1, (w_in,)),) if more else ()
        x2, *cast = _merge_call(l, x.reshape(T, D), pa.reshape(T, -1), pb.reshape(T, -1),
                                pc.reshape(T, -1), pd.reshape(T, -1), norm_mix_g,
                                w_gate_b, w_out_b, w_o_b, cast_jobs=merge_jobs)
        if more:
            w_mix_b, w_gate_b = cast

        ffn_jobs = (("plain", l + 1, (w_ff1,)), ("plain", l + 1, (w_ff2,)), ("plain", l + 1, (w_o,)),
                    ("w_out", l + 1, w_outs)) if more else ()
        x2, *cast = _ffn_call(l, x2, norm_ff_g, w_ff1_b, w_ff2_b, final_row,
                              final_norm=not more, cast_jobs=ffn_jobs)
        if more:
            w_ff1_b, w_ff2_b, w_o_b, w_out_b = cast
        x = x2.reshape(B, S, D)
    return x
```

```python
import functools
import math

import jax
import jax.numpy as jnp
import numpy as np
from jax import lax
from jax.experimental import pallas as pl
from jax.experimental.pallas import tpu as pltpu

F32 = jnp.float32
BF16 = jnp.bfloat16

D_MODEL = 1024
W_BRANCH = 256
A_GROUPS = 4
CHUNK = 128
CONV_WIDTH = 3
DILATIONS = (1, 4, 16)
WINDOWS = (128, 512, 2048)
assert all(w == CHUNK * d for w, d in zip(WINDOWS, DILATIONS))
HEADS_PER_GROUP = 4
HEAD_DIM = 64
POOL_WINDOWS = (2, 4, 8, 16)
POOL_GROUPS = len(POOL_WINDOWS)
POOL_HALO = 16
CONV_HALO = 8
N_BRANCH = 4
N_BUCKETS = 32
MAX_DISTANCE = 2048
EPS = 1e-6
NEG_BIG = -1e30
COL_Q = 5 * W_BRANCH
COL_K = COL_Q + 3 * W_BRANCH
COL_V = COL_K + 3 * W_BRANCH
COL_D = COL_V + 3 * W_BRANCH
COL_GATE = COL_D + W_BRANCH
W_LOCAL = 6 * W_BRANCH
W_QKV = 3 * W_BRANCH
W_QKV_OUT = 4 * W_BRANCH

MIX_TILE = 1024
MIX_SUB = 256
TOK_TILE = 1024
MERGE_SUB = 256
FFN_SUB = 512
ATTN_BLOCKS_PER_PIECE = 8
VMEM_LIMIT = 56 * 1024 * 1024


def _rmsnorm(x, g):
    return x * lax.rsqrt(jnp.mean(x * x, axis=-1, keepdims=True) + EPS) * g


def _gelu_tanh(x):
    c = math.sqrt(2.0 / math.pi)
    return x * (0.5 * (1.0 + jnp.tanh(c * (x + 0.044715 * (x * x * x)))))


def _dot(a, b):
    return jnp.dot(a, b, preferred_element_type=F32)


def _const_spec(shape):
    n = len(shape)
    return pl.BlockSpec(shape, lambda *_: (0,) * n, pipeline_mode=pl.Buffered(1))


_CAST_ARITY = {"plain": (1, 1), "w_in": (1, 2), "gate": (1, 1), "w_out": (N_BRANCH, 1)}
_MIX_COLS = ((0, COL_Q), (COL_D, COL_GATE)) + tuple(
    (col + g * W_BRANCH, col + (g + 1) * W_BRANCH)
    for g in range(len(DILATIONS)) for col in (COL_Q, COL_K, COL_V))


def _cast_plumbing(jobs, n_steps, step_of):
    kinds, operands, in_specs, out_shapes, out_specs = [], [], [], [], []
    for kind, layer, srcs in jobs:
        kinds.append(kind)
        R, C = srcs[0].shape[1:]
        rows = R // n_steps
        assert rows * n_steps == R and rows % 16 == 0, (kind, R, n_steps)
        for src in srcs:
            operands.append(src)
            in_specs.append(pl.BlockSpec((1, rows, C), lambda *g, layer=layer: (layer, step_of(*g), 0)))
        row_blk = lambda *g: (step_of(*g), 0)
        if kind == "w_out":
            out_shapes.append(jax.ShapeDtypeStruct((len(srcs), R, C), BF16))
            out_specs.append(pl.BlockSpec((len(srcs), rows, C), lambda *g: (0, step_of(*g), 0)))
        elif kind in ("w_in", "gate"):
            widths = (C - COL_GATE,) if kind == "gate" else (sum(b - a for a, b in _MIX_COLS), C - COL_GATE)
            for width in widths:
                out_shapes.append(jax.ShapeDtypeStruct((R, width), BF16))
                out_specs.append(pl.BlockSpec((rows, width), row_blk))
        else:
            out_shapes.append(jax.ShapeDtypeStruct((R, C), BF16))
            out_specs.append(pl.BlockSpec((rows, C), row_blk))
    return tuple(kinds), operands, in_specs, out_shapes, out_specs


def _split_refs(refs, n_in, n_out, kinds):
    n_ci = sum(_CAST_ARITY[k][0] for k in kinds)
    n_co = sum(_CAST_ARITY[k][1] for k in kinds)
    a, b, c = n_in, n_in + n_ci, n_in + n_ci + n_out
    return refs[:a], refs[a:b], refs[b:c], refs[c:c + n_co], refs[c + n_co:]


def _run_casts(kinds, src_refs, dst_refs):
    i = o = 0
    for kind in kinds:
        n_i, n_o = _CAST_ARITY[kind]
        srcs, dsts = src_refs[i:i + n_i], dst_refs[o:o + n_o]
        i, o = i + n_i, o + n_o
        if kind == "plain":
            dsts[0][...] = srcs[0][0].astype(BF16)
        elif kind == "w_out":
            for k, src in enumerate(srcs):
                dsts[0][k] = src[0].astype(BF16)
        elif kind == "gate":
            dsts[0][...] = srcs[0][0, :, COL_GATE:].astype(BF16)
        else:
            c = 0
            for a, b in _MIX_COLS:
                dsts[0][:, c:c + b - a] = srcs[0][0, :, a:b].astype(BF16)
                c += b - a
            dsts[1][...] = srcs[0][0, :, COL_GATE:].astype(BF16)


def _layer_row(ref, layer):
    return ref.at[pl.ds(layer, 1)]


def _mix_kernel(*refs, layer, casts):
    ins, cast_src, outs, cast_dst, (h_scr, bbuf, dbuf) = _split_refs(refs, 10, 6, casts)
    x_ref, g_ref, w_ref, lng_ref, lnb_ref, ws_ref, bs_ref, conv_ref, wpool_ref, dscale_ref = ins
    g_ref, lng_ref, lnb_ref, dscale_ref = (_layer_row(r, layer) for r in (g_ref, lng_ref, lnb_ref, dscale_ref))
    ws_ref, bs_ref, conv_ref, wpool_ref = (r.at[layer] for r in (ws_ref, bs_ref, conv_ref, wpool_ref))
    pa_ref, pb_ref, pd_ref, q0_ref, q1_ref, q2_ref = outs
    _run_casts(casts, cast_src, cast_dst)
    n_sub = MIX_TILE // MIX_SUB
    first = (pl.program_id(0) == 0) & (pl.program_id(1) == 0)

    @pl.when(first)
    def _():
        bbuf[...] = jnp.zeros(bbuf.shape, F32)
        dbuf[...] = jnp.zeros(dbuf.shape, F32)

    for s in range(n_sub):
        _mix_subtile(s, pl.program_id(1) * n_sub + s, x_ref, g_ref, w_ref, lng_ref, lnb_ref, ws_ref,
                     bs_ref, conv_ref, wpool_ref, dscale_ref, pa_ref, pb_ref, pd_ref,
                     (q0_ref, q1_ref, q2_ref), h_scr.at[s], bbuf.at[s], bbuf.at[(s - 1) % n_sub],
                     dbuf.at[s], dbuf.at[(s - 1) % n_sub])


def _mix_subtile(s, i, x_ref, g_ref, w_ref, lng_ref, lnb_ref, ws_ref, bs_ref, conv_ref, wpool_ref,
                 dscale_ref, pa_ref, pb_ref, pd_ref, q_refs, h_scr, bbuf, bbuf_prev, dbuf, dbuf_prev):
    ts = MIX_SUB
    rows = pl.ds(s * ts, ts)
    x = x_ref[0, rows, :]
    h = _rmsnorm(x, g_ref[...])
    for c in range(D_MODEL // 128):
        h_scr[0, c] = h[:, c * 128:(c + 1) * 128]
    hb = h.astype(BF16)
    zl = _dot(hb, w_ref[:, 0:W_LOCAL + W_QKV])

    sub0 = lax.broadcasted_iota(jnp.int32, (ts, 128), 1) < HEAD_DIM
    for g, (q_ref, dil) in enumerate(zip(q_refs, DILATIONS)):
        n = ts // dil
        if dil == 1:
            z = zl[:, W_LOCAL:W_LOCAL + W_QKV]
        else:
            prev_dil = DILATIONS[g - 1]
            src, n_prev, step = h_scr.at[g - 1], ts // prev_dil, dil // prev_dil
            slabs = [jnp.concatenate([src[c, pl.ds((r % prev_dil) * n_prev + r // prev_dil, n, stride=step), :]
                                      for r in range(dil)], axis=0) for c in range(D_MODEL // 128)]
            if g + 1 < len(DILATIONS):
                for c, slab in enumerate(slabs):
                    h_scr[g, c] = slab
            hp = jnp.concatenate(slabs, axis=1).astype(BF16)
            c0 = W_LOCAL + g * W_QKV
            z = _dot(hp, w_ref[:, c0:c0 + W_QKV])
        q = z[:, 0:256] * (HEAD_DIM ** -0.5)
        pieces = []
        for pr in range(2):
            qp = q[:, pr * 128:(pr + 1) * 128]
            pieces.append(jnp.where(sub0, qp, 0.0))
            pieces.append(jnp.where(sub0, 0.0, qp))
            pieces.append(z[:, 256 + pr * 128:256 + (pr + 1) * 128])
            pieces.append(z[:, 512 + pr * 128:512 + (pr + 1) * 128])
        val = jnp.concatenate(pieces, axis=1).astype(BF16)
        for r in range(dil):
            q_ref[0, r, pl.ds(s * n, n), :] = val[r * n:(r + 1) * n]

    u = _gelu_tanh(zl[:, 0:256])
    v = _gelu_tanh(zl[:, 256:512])
    mu = jnp.mean(v, axis=-1, keepdims=True)
    vc = v - mu
    var = jnp.mean(vc * vc, axis=-1, keepdims=True)
    v = vc * lax.rsqrt(var + EPS) * lng_ref[...] + lnb_ref[...]
    row = lax.broadcasted_iota(jnp.int32, (CHUNK, A_GROUPS * CHUNK), 0)
    col = lax.broadcasted_iota(jnp.int32, (CHUNK, A_GROUPS * CHUNK), 1) % CHUNK
    w_all = jnp.concatenate([ws_ref[g] for g in range(A_GROUPS)], axis=1)
    wtril = jnp.where(row >= col, w_all, 0.0).astype(BF16)
    grp = lax.broadcasted_iota(jnp.int32, (CHUNK, W_BRANCH), 1) // (W_BRANCH // A_GROUPS)
    svs = []
    for c in range(ts // CHUNK):
        vch = v[c * CHUNK:(c + 1) * CHUNK]
        stacked = jnp.concatenate(
            [jnp.where(grp == g, vch, 0.0) for g in range(A_GROUPS)], axis=0).astype(BF16)
        svs.append(_dot(wtril, stacked) + bs_ref[...])
    sv = jnp.concatenate(svs, axis=0)
    pa_ref[0, rows, :] = (u * sv).astype(BF16)

    prod = zl[:, 768:1024] * zl[:, 1024:1280]

    bbuf[0:CONV_HALO] = jnp.where(i > 0, bbuf_prev[ts:ts + CONV_HALO], 0.0)
    dbuf[0:POOL_HALO] = jnp.where(i > 0, dbuf_prev[ts:ts + POOL_HALO], 0.0)
    bbuf[CONV_HALO:ts + CONV_HALO] = prod
    cw = conv_ref[...]
    conv = (cw[0:1] * bbuf[pl.ds(CONV_HALO - 2, ts), :]
            + cw[1:2] * bbuf[pl.ds(CONV_HALO - 1, ts), :]
            + cw[2:3] * prod)
    pb_ref[0, rows, :] = (zl[:, 512:768] * conv).astype(BF16)

    dz = zl[:, 1280:1536]
    dbuf[POOL_HALO:ts + POOL_HALO] = dz
    lane = lax.broadcasted_iota(jnp.int32, (ts, 128), 1)
    first_half = lane < 64
    tpos = (lax.broadcasted_iota(jnp.int32, (ts, 128), 0) + (i * ts + 1)).astype(F32)

    def window_sums(e, levels):
        out = []
        s = e
        for k in range(levels):
            s = s + pltpu.roll(s, 1 << k, 0)
            out.append(s[POOL_HALO:])
        return out

    lo = window_sums(dbuf[:, 0:128], 2)
    hi = window_sums(dbuf[:, 128:256], 4)
    pooled_lo = jnp.where(first_half, lo[0], lo[1]) / jnp.minimum(tpos, jnp.where(first_half, 2.0, 4.0))
    pooled_hi = jnp.where(first_half, hi[2], hi[3]) / jnp.minimum(tpos, jnp.where(first_half, 8.0, 16.0))
    y = jnp.concatenate([pooled_lo, pooled_hi], axis=1) - dz
    pd_ref[0, rows, :] = (_dot(y.astype(BF16), wpool_ref[...]) * dscale_ref[...]).astype(BF16)


def _mix_call(layer, x, g, w_mix, lng, lnb, ws, bs_full, conv, wpool_bd, dscale, cast_jobs=()):
    B, S, D = x.shape
    ts = MIX_TILE
    nt = S // ts
    tok = lambda b, i: (b, i, 0)
    casts, c_ops, c_in, c_shapes, c_out = _cast_plumbing(cast_jobs, B * nt, lambda b, i: b * nt + i)
    out_shape = (
        jax.ShapeDtypeStruct((B, S, W_BRANCH), BF16),
        jax.ShapeDtypeStruct((B, S, W_BRANCH), BF16),
        jax.ShapeDtypeStruct((B, S, W_BRANCH), BF16),
    ) + tuple(jax.ShapeDtypeStruct((B, dil, S // dil, W_QKV_OUT), BF16) for dil in DILATIONS)
    in_specs = [
        pl.BlockSpec((1, ts, D), tok),
    ] + [_const_spec(a.shape) for a in (g, w_mix, lng, lnb, ws, bs_full, conv, wpool_bd, dscale)]
    out_specs = (
        pl.BlockSpec((1, ts, W_BRANCH), tok),
        pl.BlockSpec((1, ts, W_BRANCH), tok),
        pl.BlockSpec((1, ts, W_BRANCH), tok),
    ) + tuple(pl.BlockSpec((1, dil, ts // dil, W_QKV_OUT), lambda b, i: (b, 0, i, 0))
              for dil in DILATIONS)
    return pl.pallas_call(
        functools.partial(_mix_kernel, layer=layer, casts=casts),
        grid=(B, nt),
        in_specs=in_specs + c_in,
        out_specs=out_specs + tuple(c_out),
        out_shape=out_shape + tuple(c_shapes),
        scratch_shapes=[
            pltpu.VMEM((ts // MIX_SUB, len(DILATIONS) - 1, D // 128, MIX_SUB, 128), F32),
            pltpu.VMEM((ts // MIX_SUB, MIX_SUB + CONV_HALO, W_BRANCH), F32),
            pltpu.VMEM((ts // MIX_SUB, MIX_SUB + POOL_HALO, W_BRANCH), F32),
        ],
        compiler_params=pltpu.CompilerParams(
            dimension_semantics=("arbitrary", "arbitrary"), vmem_limit_bytes=VMEM_LIMIT),
        name="mix",
    )(x, g, w_mix, lng, lnb, ws, bs_full, conv, wpool_bd, dscale, *c_ops)


def _attn_kernel(*refs, casts):
    ins, cast_src, (out_ref,), cast_dst, (o_scr, l_scr, t_scr) = _split_refs(refs, 5, 1, casts)
    q0_ref, q1_ref, q2_ref, bcur_ref, bprev_ref = ins
    _run_casts(casts, cast_src, cast_dst)
    q_refs = (q0_ref, q1_ref, q2_ref)
    S = out_ref.shape[1]
    pr = pl.program_id(1)
    nb = S // CHUNK

    def cols(g, c0):
        return q_refs[g][0, :, c0:c0 + 128].reshape(nb, CHUNK, 128)

    jb = ATTN_BLOCKS_PER_PIECE
    sub0 = lax.broadcasted_iota(jnp.int32, (jb, CHUNK, 128), 2) < HEAD_DIM
    qk = lambda a, b: jnp.einsum("jqd,jkd->jqk", a, b, preferred_element_type=F32)
    pv = lambda a, b: jnp.einsum("jqk,jkd->jqd", a, b, preferred_element_type=F32)
    hh0 = 2 * pr

    for g, dil in enumerate(DILATIONS):
        blocks_per_seq = nb // dil
        has_prev = blocks_per_seq > 1
        q = jnp.concatenate([cols(g, 0), cols(g, 128)], axis=1)
        k = cols(g, 256)
        v = cols(g, 384)
        bias_cur = jnp.concatenate([bcur_ref[g, hh0], bcur_ref[g, hh0 + 1]], axis=0)[None]
        if has_prev:
            k = jnp.concatenate([jnp.concatenate([k[:1], k[:-1]], axis=0), k], axis=1)
            v = jnp.concatenate([jnp.concatenate([v[:1], v[:-1]], axis=0), v], axis=1)
            bias_prev = jnp.concatenate([bprev_ref[g, hh0], bprev_ref[g, hh0 + 1]], axis=0)
        for j0 in range(0, nb, jb):
            s = qk(q[j0:j0 + jb], k[j0:j0 + jb])
            s_cur = s[:, :, -CHUNK:] + bias_cur
            s_prev = [None if not has_prev or (j0 + jj) % blocks_per_seq == 0
                      else s[jj, :, :CHUNK] + bias_prev for jj in range(jb)]
            top = jnp.stack([s_cur[jj] if s_prev[jj] is None else jnp.maximum(s_cur[jj], s_prev[jj])
                             for jj in range(jb)])
            row_max = jnp.max(top, axis=-1, keepdims=True)
            e = jnp.exp(s_cur - row_max)
            den = e
            if has_prev:
                e_prev = jnp.stack([jnp.zeros((2 * CHUNK, CHUNK), F32) if s_prev[jj] is None
                                    else jnp.exp(s_prev[jj] - row_max[jj]) for jj in range(jb)])
                den = e + e_prev
                e = jnp.concatenate([e_prev, e], axis=-1)
            den = jnp.sum(den, axis=-1, keepdims=True)
            acc = pv(e.astype(BF16), v[j0:j0 + jb])
            o2 = acc / den
            l2 = jnp.broadcast_to(row_max + jnp.log(den), o2.shape)
            o = jnp.where(sub0, o2[:, :CHUNK], o2[:, CHUNK:])
            lse = jnp.where(sub0, l2[:, :CHUNK], l2[:, CHUNK:])
            two_stage = g >= 2
            prev_dil = DILATIONS[g - 1] if two_stage else 1
            step = dil // prev_dil
            for jj in range(jb):
                r, blk = divmod(j0 + jj, blocks_per_seq)
                start = (r % prev_dil) * (S // prev_dil) + blk * CHUNK * step + r // prev_dil
                rows = pl.ds(start, CHUNK, stride=step) if step > 1 else pl.ds(start, CHUNK)
                if two_stage:
                    t_scr[0, rows, :] = o[jj]
                    t_scr[1, rows, :] = lse[jj]
                else:
                    o_scr[g, rows, :] = o[jj]
                    l_scr[g, rows, :] = lse[jj]
        if two_stage:
            n_prev = S // prev_dil
            for r in range(prev_dil):
                rows = pl.ds(r, n_prev, stride=prev_dil)
                o_scr[g, rows, :] = t_scr[0, r * n_prev:(r + 1) * n_prev, :]
                l_scr[g, rows, :] = t_scr[1, r * n_prev:(r + 1) * n_prev, :]

    rows = 256
    for c in range(S // rows):
        sl = pl.ds(c * rows, rows)
        l0, l1, l2 = l_scr[0, sl, :], l_scr[1, sl, :], l_scr[2, sl, :]
        m = jnp.maximum(jnp.maximum(l0, l1), l2)
        e0, e1, e2 = jnp.exp(l0 - m), jnp.exp(l1 - m), jnp.exp(l2 - m)
        num = e0 * o_scr[0, sl, :] + e1 * o_scr[1, sl, :] + e2 * o_scr[2, sl, :]
        out_ref[0, sl, :] = (num / (e0 + e1 + e2)).astype(BF16)


def _attn_call(qs, bcur, bprev, cast_jobs=()):
    B = qs[0].shape[0]
    S = qs[0].shape[1] * qs[0].shape[2]
    operands = [q.reshape(B, S, W_QKV_OUT) for q in qs]
    spec = pl.BlockSpec((1, S, W_QKV_OUT // 2), lambda b, p: (b, 0, p))
    casts, c_ops, c_in, c_shapes, c_out = _cast_plumbing(cast_jobs, B * 2, lambda b, p: b * 2 + p)
    return pl.pallas_call(
        functools.partial(_attn_kernel, casts=casts),
        grid=(B, 2),
        in_specs=[spec, spec, spec, _const_spec(bcur.shape), _const_spec(bprev.shape)] + c_in,
        out_specs=(pl.BlockSpec((1, S, 128), lambda b, p: (b, 0, p)),) + tuple(c_out),
        out_shape=(jax.ShapeDtypeStruct((B, S, W_BRANCH), BF16),) + tuple(c_shapes),
        scratch_shapes=[
            pltpu.VMEM((3, S, 128), F32),
            pltpu.VMEM((3, S, 128), F32),
            pltpu.VMEM((2, S, 128), F32),
        ],
        compiler_params=pltpu.CompilerParams(
            dimension_semantics=("arbitrary", "arbitrary"), vmem_limit_bytes=VMEM_LIMIT),
        name="attn",
    )(*operands, bcur, bprev, *c_ops)


def _merge_kernel(*refs, layer, casts):
    ins, cast_src, (o_ref,), cast_dst, _ = _split_refs(refs, 9, 1, casts)
    x_ref, pa_ref, pb_ref, pc_ref, pd_ref, g_ref, wg_ref, wout_ref, wo_ref = ins
    g_ref = _layer_row(g_ref, layer)
    _run_casts(casts, cast_src, cast_dst)
    for s in range(TOK_TILE // MERGE_SUB):
        rows = pl.ds(s * MERGE_SUB, MERGE_SUB)
        ys = [_dot(p_ref[rows, :], wout_ref[br]) for br, p_ref in enumerate((pa_ref, pb_ref, pc_ref, pd_ref))]
        x = x_ref[rows, :]
        hb = _rmsnorm(x, g_ref[...]).astype(BF16)
        merged = None
        gates = _dot(hb, wg_ref[...])
        for br, y in enumerate(ys):
            term = jax.nn.sigmoid(gates[:, br * D_MODEL:(br + 1) * D_MODEL]) * y
            merged = term if merged is None else merged + term
        o_ref[rows, :] = x + _dot(merged.astype(BF16), wo_ref[...])


def _merge_call(layer, x, pa, pb, pc, pd, g, w_gate, w_out, w_o, cast_jobs=()):
    T, D = x.shape
    tt = TOK_TILE
    tok = lambda i: (i, 0)
    pspec = pl.BlockSpec((tt, W_BRANCH), tok)
    casts, c_ops, c_in, c_shapes, c_out = _cast_plumbing(cast_jobs, T // tt, lambda i: i)
    return pl.pallas_call(
        functools.partial(_merge_kernel, layer=layer, casts=casts),
        grid=(T // tt,),
        in_specs=[pl.BlockSpec((tt, D), tok), pspec, pspec, pspec, pspec,
                  _const_spec(g.shape), _const_spec(w_gate.shape), _const_spec(w_out.shape),
                  _const_spec(w_o.shape)] + c_in,
        out_specs=(pl.BlockSpec((tt, D), tok),) + tuple(c_out),
        out_shape=(jax.ShapeDtypeStruct((T, D), F32),) + tuple(c_shapes),
        compiler_params=pltpu.CompilerParams(
            dimension_semantics=("arbitrary",), vmem_limit_bytes=VMEM_LIMIT),
        name="merge",
    )(x, pa, pb, pc, pd, g, w_gate, w_out, w_o, *c_ops)


def _ffn_kernel(*refs, layer, final_norm, casts):
    (x_ref, g_ref, w1_ref, w2_ref, fg_ref), cast_src, (o_ref,), cast_dst, _ = _split_refs(refs, 5, 1, casts)
    g_ref = _layer_row(g_ref, layer)
    _run_casts(casts, cast_src, cast_dst)
    for s in range(TOK_TILE // FFN_SUB):
        rows = pl.ds(s * FFN_SUB, FFN_SUB)
        x = x_ref[rows, :]
        hb = _rmsnorm(x, g_ref[...]).astype(BF16)
        a = _dot(hb, w1_ref[...])
        y = x + _dot(jnp.square(jnp.maximum(a, 0.0)).astype(BF16), w2_ref[...])
        if final_norm:
            y = _rmsnorm(y, fg_ref[...])
        o_ref[rows, :] = y


def _ffn_call(layer, x, g, w1, w2, final_g, final_norm, cast_jobs=()):
    T, D = x.shape
    tt = TOK_TILE
    tok = lambda i: (i, 0)
    casts, c_ops, c_in, c_shapes, c_out = _cast_plumbing(cast_jobs, T // tt, lambda i: i)
    return pl.pallas_call(
        functools.partial(_ffn_kernel, layer=layer, final_norm=final_norm, casts=casts),
        grid=(T // tt,),
        in_specs=[pl.BlockSpec((tt, D), tok), _const_spec(g.shape), _const_spec(w1.shape),
                  _const_spec(w2.shape), _const_spec((1, D))] + c_in,
        out_specs=(pl.BlockSpec((tt, D), tok),) + tuple(c_out),
        out_shape=(jax.ShapeDtypeStruct((T, D), F32),) + tuple(c_shapes),
        compiler_params=pltpu.CompilerParams(
            dimension_semantics=("arbitrary",), vmem_limit_bytes=VMEM_LIMIT),
        name="ffn",
    )(x, g, w1, w2, final_g, *c_ops)


def _t5_bucket_np(dist):
    max_exact = N_BUCKETS // 2
    d_f = np.maximum(dist, 1).astype(np.float32)
    ratio = np.log(d_f / np.float32(max_exact)) / np.float32(math.log(MAX_DISTANCE / max_exact))
    large = max_exact + (ratio * np.float32(N_BUCKETS - max_exact)).astype(np.int32)
    large = np.minimum(large, N_BUCKETS - 1)
    return np.where(dist < max_exact, dist, large)


def _bucket_indices():
    qi = np.arange(CHUNK)[:, None]
    ki = np.arange(CHUNK)[None, :]
    d_cur = qi - ki
    d_prev = qi + CHUNK - ki
    cur = [np.where(d_cur >= 0, _t5_bucket_np(np.clip(d_cur, 0, None) * dil), -1) for dil in DILATIONS]
    prev = [np.where(d_prev <= CHUNK, _t5_bucket_np(d_prev * dil), -1) for dil in DILATIONS[:2]]
    return np.stack(cur).astype(np.int32), np.stack(prev).astype(np.int32)


def _bias_kernel(tab_ref, icur_ref, iprev_ref, bcur_ref, bprev_ref):
    for idx_ref, out_ref in ((icur_ref, bcur_ref), (iprev_ref, bprev_ref)):
        for g in range(idx_ref.shape[0]):
            idx = idx_ref[g]
            accs = [jnp.full((CHUNK, CHUNK), NEG_BIG, F32) for _ in range(HEADS_PER_GROUP)]
            for b in range(N_BUCKETS):
                hit = idx == b
                for h in range(HEADS_PER_GROUP):
                    accs[h] = jnp.where(hit, tab_ref[b, g * HEADS_PER_GROUP + h], accs[h])
            for h in range(HEADS_PER_GROUP):
                out_ref[g, h] = accs[h]


def _bias_tables(rel_bias):
    icur, iprev = _bucket_indices()
    vm = pl.BlockSpec(memory_space=pltpu.VMEM)
    return pl.pallas_call(
        _bias_kernel,
        in_specs=[pl.BlockSpec(memory_space=pltpu.SMEM), vm, vm],
        out_specs=(vm, vm),
        out_shape=(jax.ShapeDtypeStruct((3, HEADS_PER_GROUP, CHUNK, CHUNK), F32),
                   jax.ShapeDtypeStruct((2, HEADS_PER_GROUP, CHUNK, CHUNK), F32)),
        name="bias",
    )(rel_bias, jnp.asarray(icur), jnp.asarray(iprev))


def kernel(x, norm_mix_g, w_in, a_ln_g, a_ln_b, a_ws, a_bs, w_a_out, b_conv, w_b_out, rel_bias,
           w_c_out, d_w, d_scale, w_d_out, w_o, norm_ff_g, w_ff1, w_ff2, final_g):
    B, S, D = x.shape
    depth = w_in.shape[0]
    bcur, bprev = _bias_tables(rel_bias)
    w_outs = (w_a_out, w_b_out, w_c_out, w_d_out)
    w_mix_b = jnp.concatenate([w_in[0][:, a:b] for a, b in _MIX_COLS], axis=1).astype(BF16)
    bs_full = jnp.repeat(jnp.swapaxes(a_bs, 1, 2), W_BRANCH // A_GROUPS, axis=2)
    eye = jnp.eye(POOL_GROUPS, dtype=d_w.dtype)
    wpool_bd = jnp.einsum("lgde,gh->lgdhe", d_w, eye).reshape(depth, W_BRANCH, W_BRANCH).astype(BF16)
    final_row = final_g[None]
    T = B * S
    for l in range(depth):
        more = l + 1 < depth

        mix_jobs = (("plain", 0, (w_o,)), ("w_out", 0, w_outs)) if l == 0 else ()
        pa, pb, pd, q0, q1, q2, *cast = _mix_call(
            l, x, norm_mix_g, w_mix_b, a_ln_g, a_ln_b, a_ws, bs_full, b_conv, wpool_bd, d_scale,
            cast_jobs=mix_jobs)
        if l == 0:
            w_o_b, w_out_b = cast

        attn_jobs = (("plain", 0, (w_ff1,)), ("plain", 0, (w_ff2,)), ("gate", 0, (w_in,))) if l == 0 else ()
        pc, *cast = _attn_call((q0, q1, q2), bcur, bprev, cast_jobs=attn_jobs)
        if l == 0:
            w_ff1_b, w_ff2_b, w_gate_b = cast

        merge_jobs = (("w_in", l + 1, (w_in,)),) if more else ()
        x2, *cast = _merge_call(l, x.reshape(T, D), pa.reshape(T, -1), pb.reshape(T, -1),
                                pc.reshape(T, -1), pd.reshape(T, -1), norm_mix_g,
                                w_gate_b, w_out_b, w_o_b, cast_jobs=merge_jobs)
        if more:
            w_mix_b, w_gate_b = cast

        ffn_jobs = (("plain", l + 1, (w_ff1,)), ("plain", l + 1, (w_ff2,)), ("plain", l + 1, (w_o,)),
                    ("w_out", l + 1, w_outs)) if more else ()
        x2, *cast = _ffn_call(l, x2, norm_ff_g, w_ff1_b, w_ff2_b, final_row,
                              final_norm=not more, cast_jobs=ffn_jobs)
        if more:
            w_ff1_b, w_ff2_b, w_o_b, w_out_b = cast
        x = x2.reshape(B, S, D)
    return x
```

```python
import functools
import math

import jax
import jax.numpy as jnp
import numpy as np
from jax import lax
from jax.experimental import pallas as pl
from jax.experimental.pallas import tpu as pltpu

F32 = jnp.float32
BF16 = jnp.bfloat16

D_MODEL = 1024
W_BRANCH = 256
A_GROUPS = 4
CHUNK = 128
CONV_WIDTH = 3
DILATIONS = (1, 4, 16)
WINDOWS = (128, 512, 2048)
assert all(w == CHUNK * d for w, d in zip(WINDOWS, DILATIONS))
HEADS_PER_GROUP = 4
HEAD_DIM = 64
POOL_WINDOWS = (2, 4, 8, 16)
POOL_GROUPS = len(POOL_WINDOWS)
POOL_HALO = 16
CONV_HALO = 8
N_BRANCH = 4
N_BUCKETS = 32
MAX_DISTANCE = 2048
EPS = 1e-6
NEG_BIG = -1e30
COL_Q = 5 * W_BRANCH
COL_K = COL_Q + 3 * W_BRANCH
COL_V = COL_K + 3 * W_BRANCH
COL_D = COL_V + 3 * W_BRANCH
COL_GATE = COL_D + W_BRANCH
W_LOCAL = 6 * W_BRANCH
W_QKV = 3 * W_BRANCH
W_QKV_OUT = 4 * W_BRANCH

MIX_TILE = 1024
MIX_SUB = 256
TOK_TILE = 1024
MERGE_SUB = 256
FFN_SUB = 256
ATTN_BLOCKS_PER_PIECE = 8
VMEM_LIMIT = 56 * 1024 * 1024


def _rmsnorm(x, g):
    return x * lax.rsqrt(jnp.mean(x * x, axis=-1, keepdims=True) + EPS) * g


def _gelu_tanh(x):
    c = math.sqrt(2.0 / math.pi)
    return x * (0.5 * (1.0 + jnp.tanh(c * (x + 0.044715 * (x * x * x)))))


def _dot(a, b):
    return jnp.dot(a, b, preferred_element_type=F32)


def _const_spec(shape):
    n = len(shape)
    return pl.BlockSpec(shape, lambda *_: (0,) * n, pipeline_mode=pl.Buffered(1))


_CAST_ARITY = {"plain": (1, 1), "w_in": (1, 2), "gate": (1, 1), "w_out": (N_BRANCH, 1)}
_MIX_COLS = ((0, COL_Q), (COL_D, COL_GATE)) + tuple(
    (col + g * W_BRANCH, col + (g + 1) * W_BRANCH)
    for g in range(len(DILATIONS)) for col in (COL_Q, COL_K, COL_V))


def _cast_plumbing(jobs, n_steps, step_of):
    kinds, operands, in_specs, out_shapes, out_specs = [], [], [], [], []
    for kind, layer, srcs in jobs:
        kinds.append(kind)
        R, C = srcs[0].shape[1:]
        rows = R // n_steps
        assert rows * n_steps == R and rows % 16 == 0, (kind, R, n_steps)
        for src in srcs:
            operands.append(src)
            in_specs.append(pl.BlockSpec((1, rows, C), lambda *g, layer=layer: (layer, step_of(*g), 0)))
        row_blk = lambda *g: (step_of(*g), 0)
        if kind == "w_out":
            out_shapes.append(jax.ShapeDtypeStruct((len(srcs), R, C), BF16))
            out_specs.append(pl.BlockSpec((len(srcs), rows, C), lambda *g: (0, step_of(*g), 0)))
        elif kind in ("w_in", "gate"):
            widths = (C - COL_GATE,) if kind == "gate" else (sum(b - a for a, b in _MIX_COLS), C - COL_GATE)
            for width in widths:
                out_shapes.append(jax.ShapeDtypeStruct((R, width), BF16))
                out_specs.append(pl.BlockSpec((rows, width), row_blk))
        else:
            out_shapes.append(jax.ShapeDtypeStruct((R, C), BF16))
            out_specs.append(pl.BlockSpec((rows, C), row_blk))
    return tuple(kinds), operands, in_specs, out_shapes, out_specs


def _split_refs(refs, n_in, n_out, kinds):
    n_ci = sum(_CAST_ARITY[k][0] for k in kinds)
    n_co = sum(_CAST_ARITY[k][1] for k in kinds)
    a, b, c = n_in, n_in + n_ci, n_in + n_ci + n_out
    return refs[:a], refs[a:b], refs[b:c], refs[c:c + n_co], refs[c + n_co:]


def _run_casts(kinds, src_refs, dst_refs):
    i = o = 0
    for kind in kinds:
        n_i, n_o = _CAST_ARITY[kind]
        srcs, dsts = src_refs[i:i + n_i], dst_refs[o:o + n_o]
        i, o = i + n_i, o + n_o
        if kind == "plain":
            dsts[0][...] = srcs[0][0].astype(BF16)
        elif kind == "w_out":
            for k, src in enumerate(srcs):
                dsts[0][k] = src[0].astype(BF16)
        elif kind == "gate":
            dsts[0][...] = srcs[0][0, :, COL_GATE:].astype(BF16)
        else:
            c = 0
            for a, b in _MIX_COLS:
                dsts[0][:, c:c + b - a] = srcs[0][0, :, a:b].astype(BF16)
                c += b - a
            dsts[1][...] = srcs[0][0, :, COL_GATE:].astype(BF16)


def _layer_row(ref, layer):
    return ref.at[pl.ds(layer, 1)]


def _mix_kernel(*refs, layer, casts):
    ins, cast_src, outs, cast_dst, (h_scr, bbuf, dbuf) = _split_refs(refs, 10, 6, casts)
    x_ref, g_ref, w_ref, lng_ref, lnb_ref, ws_ref, bs_ref, conv_ref, wpool_ref, dscale_ref = ins
    g_ref, lng_ref, lnb_ref, dscale_ref = (_layer_row(r, layer) for r in (g_ref, lng_ref, lnb_ref, dscale_ref))
    ws_ref, bs_ref, conv_ref, wpool_ref = (r.at[layer] for r in (ws_ref, bs_ref, conv_ref, wpool_ref))
    pa_ref, pb_ref, pd_ref, q0_ref, q1_ref, q2_ref = outs
    _run_casts(casts, cast_src, cast_dst)
    n_sub = MIX_TILE // MIX_SUB
    first = (pl.program_id(0) == 0) & (pl.program_id(1) == 0)

    @pl.when(first)
    def _():
        bbuf[...] = jnp.zeros(bbuf.shape, F32)
        dbuf[...] = jnp.zeros(dbuf.shape, F32)

    for s in range(n_sub):
        _mix_subtile(s, pl.program_id(1) * n_sub + s, x_ref, g_ref, w_ref, lng_ref, lnb_ref, ws_ref,
                     bs_ref, conv_ref, wpool_ref, dscale_ref, pa_ref, pb_ref, pd_ref,
                     (q0_ref, q1_ref, q2_ref), h_scr.at[s], bbuf.at[s], bbuf.at[(s - 1) % n_sub],
                     dbuf.at[s], dbuf.at[(s - 1) % n_sub])


def _mix_subtile(s, i, x_ref, g_ref, w_ref, lng_ref, lnb_ref, ws_ref, bs_ref, conv_ref, wpool_ref,
                 dscale_ref, pa_ref, pb_ref, pd_ref, q_refs, h_scr, bbuf, bbuf_prev, dbuf, dbuf_prev):
    ts = MIX_SUB
    rows = pl.ds(s * ts, ts)
    x = x_ref[0, rows, :]
    h = _rmsnorm(x, g_ref[...])
    for c in range(D_MODEL // 128):
        h_scr[0, c] = h[:, c * 128:(c + 1) * 128]
    hb = h.astype(BF16)
    zl = _dot(hb, w_ref[:, 0:W_LOCAL + W_QKV])

    sub0 = lax.broadcasted_iota(jnp.int32, (ts, 128), 1) < HEAD_DIM
    for g, (q_ref, dil) in enumerate(zip(q_refs, DILATIONS)):
        n = ts // dil
        if dil == 1:
            z = zl[:, W_LOCAL:W_LOCAL + W_QKV]
        else:
            prev_dil = DILATIONS[g - 1]
            src, n_prev, step = h_scr.at[g - 1], ts // prev_dil, dil // prev_dil
            slabs = [jnp.concatenate([src[c, pl.ds((r % prev_dil) * n_prev + r // prev_dil, n, stride=step), :]
                                      for r in range(dil)], axis=0) for c in range(D_MODEL // 128)]
            if g + 1 < len(DILATIONS):
                for c, slab in enumerate(slabs):
                    h_scr[g, c] = slab
            hp = jnp.concatenate(slabs, axis=1).astype(BF16)
            c0 = W_LOCAL + g * W_QKV
            z = _dot(hp, w_ref[:, c0:c0 + W_QKV])
        q = z[:, 0:256] * (HEAD_DIM ** -0.5)
        pieces = []
        for pr in range(2):
            qp = q[:, pr * 128:(pr + 1) * 128]
            pieces.append(jnp.where(sub0, qp, 0.0))
            pieces.append(jnp.where(sub0, 0.0, qp))
            pieces.append(z[:, 256 + pr * 128:256 + (pr + 1) * 128])
            pieces.append(z[:, 512 + pr * 128:512 + (pr + 1) * 128])
        val = jnp.concatenate(pieces, axis=1).astype(BF16)
        for r in range(dil):
            q_ref[0, r, pl.ds(s * n, n), :] = val[r * n:(r + 1) * n]

    u = _gelu_tanh(zl[:, 0:256])
    v = _gelu_tanh(zl[:, 256:512])
    mu = jnp.mean(v, axis=-1, keepdims=True)
    vc = v - mu
    var = jnp.mean(vc * vc, axis=-1, keepdims=True)
    v = vc * lax.rsqrt(var + EPS) * lng_ref[...] + lnb_ref[...]
    row = lax.broadcasted_iota(jnp.int32, (CHUNK, A_GROUPS * CHUNK), 0)
    col = lax.broadcasted_iota(jnp.int32, (CHUNK, A_GROUPS * CHUNK), 1) % CHUNK
    w_all = jnp.concatenate([ws_ref[g] for g in range(A_GROUPS)], axis=1)
    wtril = jnp.where(row >= col, w_all, 0.0).astype(BF16)
    grp = lax.broadcasted_iota(jnp.int32, (CHUNK, W_BRANCH), 1) // (W_BRANCH // A_GROUPS)
    svs = []
    for c in range(ts // CHUNK):
        vch = v[c * CHUNK:(c + 1) * CHUNK]
        stacked = jnp.concatenate(
            [jnp.where(grp == g, vch, 0.0) for g in range(A_GROUPS)], axis=0).astype(BF16)
        svs.append(_dot(wtril, stacked) + bs_ref[...])
    sv = jnp.concatenate(svs, axis=0)
    pa_ref[0, rows, :] = (u * sv).astype(BF16)

    prod = zl[:, 768:1024] * zl[:, 1024:1280]

    bbuf[0:CONV_HALO] = jnp.where(i > 0, bbuf_prev[ts:ts + CONV_HALO], 0.0)
    dbuf[0:POOL_HALO] = jnp.where(i > 0, dbuf_prev[ts:ts + POOL_HALO], 0.0)
    bbuf[CONV_HALO:ts + CONV_HALO] = prod
    cw = conv_ref[...]
    conv = (cw[0:1] * bbuf[pl.ds(CONV_HALO - 2, ts), :]
            + cw[1:2] * bbuf[pl.ds(CONV_HALO - 1, ts), :]
            + cw[2:3] * prod)
    pb_ref[0, rows, :] = (zl[:, 512:768] * conv).astype(BF16)

    dz = zl[:, 1280:1536]
    dbuf[POOL_HALO:ts + POOL_HALO] = dz
    lane = lax.broadcasted_iota(jnp.int32, (ts, 128), 1)
    first_half = lane < 64
    tpos = (lax.broadcasted_iota(jnp.int32, (ts, 128), 0) + (i * ts + 1)).astype(F32)

    def window_sums(e, levels):
        out = []
        s = e
        for k in range(levels):
            s = s + pltpu.roll(s, 1 << k, 0)
            out.append(s[POOL_HALO:])
        return out

    lo = window_sums(dbuf[:, 0:128], 2)
    hi = window_sums(dbuf[:, 128:256], 4)
    pooled_lo = jnp.where(first_half, lo[0], lo[1]) / jnp.minimum(tpos, jnp.where(first_half, 2.0, 4.0))
    pooled_hi = jnp.where(first_half, hi[2], hi[3]) / jnp.minimum(tpos, jnp.where(first_half, 8.0, 16.0))
    y = jnp.concatenate([pooled_lo, pooled_hi], axis=1) - dz
    pd_ref[0, rows, :] = (_dot(y.astype(BF16), wpool_ref[...]) * dscale_ref[...]).astype(BF16)


def _mix_call(layer, x, g, w_mix, lng, lnb, ws, bs_full, conv, wpool_bd, dscale, cast_jobs=()):
    B, S, D = x.shape
    ts = MIX_TILE
    nt = S // ts
    tok = lambda b, i: (b, i, 0)
    casts, c_ops, c_in, c_shapes, c_out = _cast_plumbing(cast_jobs, B * nt, lambda b, i: b * nt + i)
    out_shape = (
        jax.ShapeDtypeStruct((B, S, W_BRANCH), BF16),
        jax.ShapeDtypeStruct((B, S, W_BRANCH), BF16),
        jax.ShapeDtypeStruct((B, S, W_BRANCH), BF16),
    ) + tuple(jax.ShapeDtypeStruct((B, dil, S // dil, W_QKV_OUT), BF16) for dil in DILATIONS)
    in_specs = [
        pl.BlockSpec((1, ts, D), tok),
    ] + [_const_spec(a.shape) for a in (g, w_mix, lng, lnb, ws, bs_full, conv, wpool_bd, dscale)]
    out_specs = (
        pl.BlockSpec((1, ts, W_BRANCH), tok),
        pl.BlockSpec((1, ts, W_BRANCH), tok),
        pl.BlockSpec((1, ts, W_BRANCH), tok),
    ) + tuple(pl.BlockSpec((1, dil, ts // dil, W_QKV_OUT), lambda b, i: (b, 0, i, 0))
              for dil in DILATIONS)
    return pl.pallas_call(
        functools.partial(_mix_kernel, layer=layer, casts=casts),
        grid=(B, nt),
        in_specs=in_specs + c_in,
        out_specs=out_specs + tuple(c_out),
        out_shape=out_shape + tuple(c_shapes),
        scratch_shapes=[
            pltpu.VMEM((ts // MIX_SUB, len(DILATIONS) - 1, D // 128, MIX_SUB, 128), F32),
            pltpu.VMEM((ts // MIX_SUB, MIX_SUB + CONV_HALO, W_BRANCH), F32),
            pltpu.VMEM((ts // MIX_SUB, MIX_SUB + POOL_HALO, W_BRANCH), F32),
        ],
        compiler_params=pltpu.CompilerParams(
            dimension_semantics=("arbitrary", "arbitrary"), vmem_limit_bytes=VMEM_LIMIT),
        name="mix",
    )(x, g, w_mix, lng, lnb, ws, bs_full, conv, wpool_bd, dscale, *c_ops)


def _attn_kernel(*refs, casts):
    ins, cast_src, (out_ref,), cast_dst, (o_scr, l_scr, t_scr) = _split_refs(refs, 5, 1, casts)
    q0_ref, q1_ref, q2_ref, bcur_ref, bprev_ref = ins
    _run_casts(casts, cast_src, cast_dst)
    q_refs = (q0_ref, q1_ref, q2_ref)
    S = out_ref.shape[1]
    pr = pl.program_id(1)
    nb = S // CHUNK

    def cols(g, c0):
        return q_refs[g][0, :, c0:c0 + 128].reshape(nb, CHUNK, 128)

    jb = ATTN_BLOCKS_PER_PIECE
    sub0 = lax.broadcasted_iota(jnp.int32, (jb, CHUNK, 128), 2) < HEAD_DIM
    qk = lambda a, b: jnp.einsum("jqd,jkd->jqk", a, b, preferred_element_type=F32)
    pv = lambda a, b: jnp.einsum("jqk,jkd->jqd", a, b, preferred_element_type=F32)
    hh0 = 2 * pr

    for g, dil in enumerate(DILATIONS):
        blocks_per_seq = nb // dil
        has_prev = blocks_per_seq > 1
        q = jnp.concatenate([cols(g, 0), cols(g, 128)], axis=1)
        k = cols(g, 256)
        v = cols(g, 384)
        bias_cur = jnp.concatenate([bcur_ref[g, hh0], bcur_ref[g, hh0 + 1]], axis=0)[None]
        if has_prev:
            k = jnp.concatenate([jnp.concatenate([k[:1], k[:-1]], axis=0), k], axis=1)
            v = jnp.concatenate([jnp.concatenate([v[:1], v[:-1]], axis=0), v], axis=1)
            bias_prev = jnp.concatenate([bprev_ref[g, hh0], bprev_ref[g, hh0 + 1]], axis=0)
        for j0 in range(0, nb, jb):
            s = qk(q[j0:j0 + jb], k[j0:j0 + jb])
            s_cur = s[:, :, -CHUNK:] + bias_cur
            s_prev = [None if not has_prev or (j0 + jj) % blocks_per_seq == 0
                      else s[jj, :, :CHUNK] + bias_prev for jj in range(jb)]
            top = jnp.stack([s_cur[jj] if s_prev[jj] is None else jnp.maximum(s_cur[jj], s_prev[jj])
                             for jj in range(jb)])
            row_max = jnp.max(top, axis=-1, keepdims=True)
            e = jnp.exp(s_cur - row_max)
            den = e
            if has_prev:
                e_prev = jnp.stack([jnp.zeros((2 * CHUNK, CHUNK), F32) if s_prev[jj] is None
                                    else jnp.exp(s_prev[jj] - row_max[jj]) for jj in range(jb)])
                den = e + e_prev
                e = jnp.concatenate([e_prev, e], axis=-1)
            den = jnp.sum(den, axis=-1, keepdims=True)
            acc = pv(e.astype(BF16), v[j0:j0 + jb])
            o2 = acc / den
            l2 = jnp.broadcast_to(row_max + jnp.log(den), o2.shape)
            o = jnp.where(sub0, o2[:, :CHUNK], o2[:, CHUNK:])
            lse = jnp.where(sub0, l2[:, :CHUNK], l2[:, CHUNK:])
            two_stage = g >= 2
            prev_dil = DILATIONS[g - 1] if two_stage else 1
            step = dil // prev_dil
            for jj in range(jb):
                r, blk = divmod(j0 + jj, blocks_per_seq)
                start = (r % prev_dil) * (S // prev_dil) + blk * CHUNK * step + r // prev_dil
                rows = pl.ds(start, CHUNK, stride=step) if step > 1 else pl.ds(start, CHUNK)
                if two_stage:
                    t_scr[0, rows, :] = o[jj]
                    t_scr[1, rows, :] = lse[jj]
                else:
                    o_scr[g, rows, :] = o[jj]
                    l_scr[g, rows, :] = lse[jj]
        if two_stage:
            n_prev = S // prev_dil
            for r in range(prev_dil):
                rows = pl.ds(r, n_prev, stride=prev_dil)
                o_scr[g, rows, :] = t_scr[0, r * n_prev:(r + 1) * n_prev, :]
                l_scr[g, rows, :] = t_scr[1, r * n_prev:(r + 1) * n_prev, :]

    rows = 256
    for c in range(S // rows):
        sl = pl.ds(c * rows, rows)
        l0, l1, l2 = l_scr[0, sl, :], l_scr[1, sl, :], l_scr[2, sl, :]
        m = jnp.maximum(jnp.maximum(l0, l1), l2)
        e0, e1, e2 = jnp.exp(l0 - m), jnp.exp(l1 - m), jnp.exp(l2 - m)
        num = e0 * o_scr[0, sl, :] + e1 * o_scr[1, sl, :] + e2 * o_scr[2, sl, :]
        out_ref[0, sl, :] = (num / (e0 + e1 + e2)).astype(BF16)


def _attn_call(qs, bcur, bprev, cast_jobs=()):
    B = qs[0].shape[0]
    S = qs[0].shape[1] * qs[0].shape[2]
    operands = [q.reshape(B, S, W_QKV_OUT) for q in qs]
    spec = pl.BlockSpec((1, S, W_QKV_OUT // 2), lambda b, p: (b, 0, p))
    casts, c_ops, c_in, c_shapes, c_out = _cast_plumbing(cast_jobs, B * 2, lambda b, p: b * 2 + p)
    return pl.pallas_call(
        functools.partial(_attn_kernel, casts=casts),
        grid=(B, 2),
        in_specs=[spec, spec, spec, _const_spec(bcur.shape), _const_spec(bprev.shape)] + c_in,
        out_specs=(pl.BlockSpec((1, S, 128), lambda b, p: (b, 0, p)),) + tuple(c_out),
        out_shape=(jax.ShapeDtypeStruct((B, S, W_BRANCH), BF16),) + tuple(c_shapes),
        scratch_shapes=[
            pltpu.VMEM((3, S, 128), F32),
            pltpu.VMEM((3, S, 128), F32),
            pltpu.VMEM((2, S, 128), F32),
        ],
        compiler_params=pltpu.CompilerParams(
            dimension_semantics=("arbitrary", "arbitrary"), vmem_limit_bytes=VMEM_LIMIT),
        name="attn",
    )(*operands, bcur, bprev, *c_ops)


def _merge_kernel(*refs, layer, casts):
    ins, cast_src, (o_ref,), cast_dst, _ = _split_refs(refs, 9, 1, casts)
    x_ref, pa_ref, pb_ref, pc_ref, pd_ref, g_ref, wg_ref, wout_ref, wo_ref = ins
    g_ref = _layer_row(g_ref, layer)
    _run_casts(casts, cast_src, cast_dst)
    for s in range(TOK_TILE // MERGE_SUB):
        rows = pl.ds(s * MERGE_SUB, MERGE_SUB)
        ys = [_dot(p_ref[rows, :], wout_ref[br]) for br, p_ref in enumerate((pa_ref, pb_ref, pc_ref, pd_ref))]
        x = x_ref[rows, :]
        hb = _rmsnorm(x, g_ref[...]).astype(BF16)
        merged = None
        gates = _dot(hb, wg_ref[...])
        for br, y in enumerate(ys):
            term = jax.nn.sigmoid(gates[:, br * D_MODEL:(br + 1) * D_MODEL]) * y
            merged = term if merged is None else merged + term
        o_ref[rows, :] = x + _dot(merged.astype(BF16), wo_ref[...])


def _merge_call(layer, x, pa, pb, pc, pd, g, w_gate, w_out, w_o, cast_jobs=()):
    T, D = x.shape
    tt = TOK_TILE
    tok = lambda i: (i, 0)
    pspec = pl.BlockSpec((tt, W_BRANCH), tok)
    casts, c_ops, c_in, c_shapes, c_out = _cast_plumbing(cast_jobs, T // tt, lambda i: i)
    return pl.pallas_call(
        functools.partial(_merge_kernel, layer=layer, casts=casts),
        grid=(T // tt,),
        in_specs=[pl.BlockSpec((tt, D), tok), pspec, pspec, pspec, pspec,
                  _const_spec(g.shape), _const_spec(w_gate.shape), _const_spec(w_out.shape),
                  _const_spec(w_o.shape)] + c_in,
        out_specs=(pl.BlockSpec((tt, D), tok),) + tuple(c_out),
        out_shape=(jax.ShapeDtypeStruct((T, D), F32),) + tuple(c_shapes),
        compiler_params=pltpu.CompilerParams(
            dimension_semantics=("arbitrary",), vmem_limit_bytes=VMEM_LIMIT),
        name="merge",
    )(x, pa, pb, pc, pd, g, w_gate, w_out, w_o, *c_ops)


def _ffn_kernel(*refs, layer, final_norm, casts):
    (x_ref, g_ref, w1_ref, w2_ref, fg_ref), cast_src, (o_ref,), cast_dst, _ = _split_refs(refs, 5, 1, casts)
    g_ref = _layer_row(g_ref, layer)
    _run_casts(casts, cast_src, cast_dst)
    for s in range(TOK_TILE // FFN_SUB):
        rows = pl.ds(s * FFN_SUB, FFN_SUB)
        x = x_ref[rows, :]
        hb = _rmsnorm(x, g_ref[...]).astype(BF16)
        a = _dot(hb, w1_ref[...])
        y = x + _dot(jnp.square(jnp.maximum(a, 0.0)).astype(BF16), w2_ref[...])
        if final_norm:
            y = _rmsnorm(y, fg_ref[...])
        o_ref[rows, :] = y


def _ffn_call(layer, x, g, w1, w2, final_g, final_norm, cast_jobs=()):
    T, D = x.shape
    tt = TOK_TILE
    tok = lambda i: (i, 0)
    casts, c_ops, c_in, c_shapes, c_out = _cast_plumbing(cast_jobs, T // tt, lambda i: i)
    return pl.pallas_call(
        functools.partial(_ffn_kernel, layer=layer, final_norm=final_norm, casts=casts),
        grid=(T // tt,),
        in_specs=[pl.BlockSpec((tt, D), tok), _const_spec(g.shape), _const_spec(w1.shape),
                  _const_spec(w2.shape), _const_spec((1, D))] + c_in,
        out_specs=(pl.BlockSpec((tt, D), tok),) + tuple(c_out),
        out_shape=(jax.ShapeDtypeStruct((T, D), F32),) + tuple(c_shapes),
        compiler_params=pltpu.CompilerParams(
            dimension_semantics=("arbitrary",), vmem_limit_bytes=VMEM_LIMIT),
        name="ffn",
    )(x, g, w1, w2, final_g, *c_ops)


def _t5_bucket_np(dist):
    max_exact = N_BUCKETS // 2
    d_f = np.maximum(dist, 1).astype(np.float32)
    ratio = np.log(d_f / np.float32(max_exact)) / np.float32(math.log(MAX_DISTANCE / max_exact))
    large = max_exact + (ratio * np.float32(N_BUCKETS - max_exact)).astype(np.int32)
    large = np.minimum(large, N_BUCKETS - 1)
    return np.where(dist < max_exact, dist, large)


def _bucket_indices():
    qi = np.arange(CHUNK)[:, None]
    ki = np.arange(CHUNK)[None, :]
    d_cur = qi - ki
    d_prev = qi + CHUNK - ki
    cur = [np.where(d_cur >= 0, _t5_bucket_np(np.clip(d_cur, 0, None) * dil), -1) for dil in DILATIONS]
    prev = [np.where(d_prev <= CHUNK, _t5_bucket_np(d_prev * dil), -1) for dil in DILATIONS[:2]]
    return np.stack(cur).astype(np.int32), np.stack(prev).astype(np.int32)


def _bias_kernel(tab_ref, icur_ref, iprev_ref, bcur_ref, bprev_ref):
    for idx_ref, out_ref in ((icur_ref, bcur_ref), (iprev_ref, bprev_ref)):
        for g in range(idx_ref.shape[0]):
            idx = idx_ref[g]
            accs = [jnp.full((CHUNK, CHUNK), NEG_BIG, F32) for _ in range(HEADS_PER_GROUP)]
            for b in range(N_BUCKETS):
                hit = idx == b
                for h in range(HEADS_PER_GROUP):
                    accs[h] = jnp.where(hit, tab_ref[b, g * HEADS_PER_GROUP + h], accs[h])
            for h in range(HEADS_PER_GROUP):
                out_ref[g, h] = accs[h]


def _bias_tables(rel_bias):
    icur, iprev = _bucket_indices()
    vm = pl.BlockSpec(memory_space=pltpu.VMEM)
    return pl.pallas_call(
        _bias_kernel,
        in_specs=[pl.BlockSpec(memory_space=pltpu.SMEM), vm, vm],
        out_specs=(vm, vm),
        out_shape=(jax.ShapeDtypeStruct((3, HEADS_PER_GROUP, CHUNK, CHUNK), F32),
                   jax.ShapeDtypeStruct((2, HEADS_PER_GROUP, CHUNK, CHUNK), F32)),
        name="bias",
    )(rel_bias, jnp.asarray(icur), jnp.asarray(iprev))


def kernel(x, norm_mix_g, w_in, a_ln_g, a_ln_b, a_ws, a_bs, w_a_out, b_conv, w_b_out, rel_bias,
           w_c_out, d_w, d_scale, w_d_out, w_o, norm_ff_g, w_ff1, w_ff2, final_g):
    B, S, D = x.shape
    depth = w_in.shape[0]
    bcur, bprev = _bias_tables(rel_bias)
    w_outs = (w_a_out, w_b_out, w_c_out, w_d_out)
    w_mix_b = jnp.concatenate([w_in[0][:, a:b] for a, b in _MIX_COLS], axis=1).astype(BF16)
    bs_full = jnp.repeat(jnp.swapaxes(a_bs, 1, 2), W_BRANCH // A_GROUPS, axis=2)
    eye = jnp.eye(POOL_GROUPS, dtype=d_w.dtype)
    wpool_bd = jnp.einsum("lgde,gh->lgdhe", d_w, eye).reshape(depth, W_BRANCH, W_BRANCH).astype(BF16)
    final_row = final_g[None]
    T = B * S
    for l in range(depth):
        more = l + 1 < depth

        mix_jobs = (("plain", 0, (w_o,)), ("w_out", 0, w_outs)) if l == 0 else ()
        pa, pb, pd, q0, q1, q2, *cast = _mix_call(
            l, x, norm_mix_g, w_mix_b, a_ln_g, a_ln_b, a_ws, bs_full, b_conv, wpool_bd, d_scale,
            cast_jobs=mix_jobs)
        if l == 0:
            w_o_b, w_out_b = cast

        attn_jobs = (("plain", 0, (w_ff1,)), ("plain", 0, (w_ff2,)), ("gate", 0, (w_in,))) if l == 0 else ()
        pc, *cast = _attn_call((q0, q1, q2), bcur, bprev, cast_jobs=attn_jobs)
        if l == 0:
            w_ff1_b, w_ff2_b, w_gate_b = cast

        merge_jobs = (("w_in", l + 1, (w_in,)),) if more else ()
        x2, *cast = _merge_call(l, x.reshape(T, D), pa.reshape(T, -1), pb.reshape(T, -1),
                                pc.reshape(T, -1), pd.reshape(T, -1), norm_mix_g,
                                w_gate_b, w_out_b, w_o_b, cast_jobs=merge_jobs)
        if more:
            w_mix_b, w_gate_b = cast

        ffn_jobs = (("plain", l + 1, (w_ff1,)), ("plain", l + 1, (w_ff2,)), ("plain", l + 1, (w_o,)),
                    ("w_out", l + 1, w_outs)) if more else ()
        x2, *cast = _ffn_call(l, x2, norm_ff_g, w_ff1_b, w_ff2_b, final_row,
                              final_norm=not more, cast_jobs=ffn_jobs)
        if more:
            w_ff1_b, w_ff2_b, w_o_b, w_out_b = cast
        x = x2.reshape(B, S, D)
    return x
```

```python
import functools
import math

import jax
import jax.numpy as jnp
import numpy as np
from jax import lax
from jax.experimental import pallas as pl
from jax.experimental.pallas import tpu as pltpu

F32 = jnp.float32
BF16 = jnp.bfloat16

D_MODEL = 1024
W_BRANCH = 256
A_GROUPS = 4
CHUNK = 128
CONV_WIDTH = 3
DILATIONS = (1, 4, 16)
WINDOWS = (128, 512, 2048)
assert all(w == CHUNK * d for w, d in zip(WINDOWS, DILATIONS))
HEADS_PER_GROUP = 4
HEAD_DIM = 64
POOL_WINDOWS = (2, 4, 8, 16)
POOL_GROUPS = len(POOL_WINDOWS)
POOL_HALO = 16
CONV_HALO = 8
N_BRANCH = 4
N_BUCKETS = 32
MAX_DISTANCE = 2048
EPS = 1e-6
NEG_BIG = -1e30
COL_Q = 5 * W_BRANCH
COL_K = COL_Q + 3 * W_BRANCH
COL_V = COL_K + 3 * W_BRANCH
COL_D = COL_V + 3 * W_BRANCH
COL_GATE = COL_D + W_BRANCH
W_LOCAL = 6 * W_BRANCH
W_QKV = 3 * W_BRANCH
W_QKV_OUT = 4 * W_BRANCH

MIX_TILE = 1024
MIX_SUB = 256
TOK_TILE = 1024
MERGE_SUB = 256
FFN_SUB = 512
ATTN_BLOCKS_PER_PIECE = 8
VMEM_LIMIT = 56 * 1024 * 1024


def _rmsnorm(x, g):
    return x * lax.rsqrt(jnp.mean(x * x, axis=-1, keepdims=True) + EPS) * g


def _gelu_tanh(x):
    c = math.sqrt(2.0 / math.pi)
    return x * (0.5 * (1.0 + jnp.tanh(c * (x + 0.044715 * (x * x * x)))))


def _dot(a, b):
    return jnp.dot(a, b, preferred_element_type=F32)


def _const_spec(shape):
    n = len(shape)
    return pl.BlockSpec(shape, lambda *_: (0,) * n, pipeline_mode=pl.Buffered(1))


_CAST_ARITY = {"plain": (1, 1), "w_in": (1, 2), "gate": (1, 1), "w_out": (N_BRANCH, 1)}
_MIX_COLS = ((0, COL_Q), (COL_D, COL_GATE)) + tuple(
    (col + g * W_BRANCH, col + (g + 1) * W_BRANCH)
    for g in range(len(DILATIONS)) for col in (COL_Q, COL_K, COL_V))


def _cast_plumbing(jobs, n_steps, step_of):
    kinds, operands, in_specs, out_shapes, out_specs = [], [], [], [], []
    for kind, layer, srcs in jobs:
        kinds.append(kind)
        R, C = srcs[0].shape[1:]
        rows = R // n_steps
        assert rows * n_steps == R and rows % 16 == 0, (kind, R, n_steps)
        for src in srcs:
            operands.append(src)
            in_specs.append(pl.BlockSpec((1, rows, C), lambda *g, layer=layer: (layer, step_of(*g), 0)))
        row_blk = lambda *g: (step_of(*g), 0)
        if kind == "w_out":
            out_shapes.append(jax.ShapeDtypeStruct((len(srcs), R, C), BF16))
            out_specs.append(pl.BlockSpec((len(srcs), rows, C), lambda *g: (0, step_of(*g), 0)))
        elif kind in ("w_in", "gate"):
            widths = (C - COL_GATE,) if kind == "gate" else (sum(b - a for a, b in _MIX_COLS), C - COL_GATE)
            for width in widths:
                out_shapes.append(jax.ShapeDtypeStruct((R, width), BF16))
                out_specs.append(pl.BlockSpec((rows, width), row_blk))
        else:
            out_shapes.append(jax.ShapeDtypeStruct((R, C), BF16))
            out_specs.append(pl.BlockSpec((rows, C), row_blk))
    return tuple(kinds), operands, in_specs, out_shapes, out_specs


def _split_refs(refs, n_in, n_out, kinds):
    n_ci = sum(_CAST_ARITY[k][0] for k in kinds)
    n_co = sum(_CAST_ARITY[k][1] for k in kinds)
    a, b, c = n_in, n_in + n_ci, n_in + n_ci + n_out
    return refs[:a], refs[a:b], refs[b:c], refs[c:c + n_co], refs[c + n_co:]


def _run_casts(kinds, src_refs, dst_refs):
    i = o = 0
    for kind in kinds:
        n_i, n_o = _CAST_ARITY[kind]
        srcs, dsts = src_refs[i:i + n_i], dst_refs[o:o + n_o]
        i, o = i + n_i, o + n_o
        if kind == "plain":
            dsts[0][...] = srcs[0][0].astype(BF16)
        elif kind == "w_out":
            for k, src in enumerate(srcs):
                dsts[0][k] = src[0].astype(BF16)
        elif kind == "gate":
            dsts[0][...] = srcs[0][0, :, COL_GATE:].astype(BF16)
        else:
            c = 0
            for a, b in _MIX_COLS:
                dsts[0][:, c:c + b - a] = srcs[0][0, :, a:b].astype(BF16)
                c += b - a
            dsts[1][...] = srcs[0][0, :, COL_GATE:].astype(BF16)


def _layer_row(ref, layer):
    return ref.at[pl.ds(layer, 1)]


def _mix_kernel(*refs, layer, casts):
    ins, cast_src, outs, cast_dst, (h_scr, bbuf, dbuf) = _split_refs(refs, 10, 6, casts)
    x_ref, g_ref, w_ref, lng_ref, lnb_ref, ws_ref, bs_ref, conv_ref, wpool_ref, dscale_ref = ins
    g_ref, lng_ref, lnb_ref, dscale_ref = (_layer_row(r, layer) for r in (g_ref, lng_ref, lnb_ref, dscale_ref))
    ws_ref, bs_ref, conv_ref, wpool_ref = (r.at[layer] for r in (ws_ref, bs_ref, conv_ref, wpool_ref))
    pa_ref, pb_ref, pd_ref, q0_ref, q1_ref, q2_ref = outs
    _run_casts(casts, cast_src, cast_dst)
    n_sub = MIX_TILE // MIX_SUB
    first = (pl.program_id(0) == 0) & (pl.program_id(1) == 0)

    @pl.when(first)
    def _():
        bbuf[...] = jnp.zeros(bbuf.shape, F32)
        dbuf[...] = jnp.zeros(dbuf.shape, F32)

    for s in range(n_sub):
        _mix_subtile(s, pl.program_id(1) * n_sub + s, x_ref, g_ref, w_ref, lng_ref, lnb_ref, ws_ref,
                     bs_ref, conv_ref, wpool_ref, dscale_ref, pa_ref, pb_ref, pd_ref,
                     (q0_ref, q1_ref, q2_ref), h_scr.at[s], bbuf.at[s], bbuf.at[(s - 1) % n_sub],
                     dbuf.at[s], dbuf.at[(s - 1) % n_sub])


def _mix_subtile(s, i, x_ref, g_ref, w_ref, lng_ref, lnb_ref, ws_ref, bs_ref, conv_ref, wpool_ref,
                 dscale_ref, pa_ref, pb_ref, pd_ref, q_refs, h_scr, bbuf, bbuf_prev, dbuf, dbuf_prev):
    ts = MIX_SUB
    rows = pl.ds(s * ts, ts)
    x = x_ref[0, rows, :]
    h = _rmsnorm(x, g_ref[...])
    for c in range(D_MODEL // 128):
        h_scr[0, c] = h[:, c * 128:(c + 1) * 128]
    hb = h.astype(BF16)
    zl = _dot(hb, w_ref[:, 0:W_LOCAL + W_QKV])

    sub0 = lax.broadcasted_iota(jnp.int32, (ts, 128), 1) < HEAD_DIM
    for g, (q_ref, dil) in enumerate(zip(q_refs, DILATIONS)):
        n = ts // dil
        if dil == 1:
            z = zl[:, W_LOCAL:W_LOCAL + W_QKV]
        else:
            prev_dil = DILATIONS[g - 1]
            src, n_prev, step = h_scr.at[g - 1], ts // prev_dil, dil // prev_dil
            slabs = [jnp.concatenate([src[c, pl.ds((r % prev_dil) * n_prev + r // prev_dil, n, stride=step), :]
                                      for r in range(dil)], axis=0) for c in range(D_MODEL // 128)]
            if g + 1 < len(DILATIONS):
                for c, slab in enumerate(slabs):
                    h_scr[g, c] = slab
            hp = jnp.concatenate(slabs, axis=1).astype(BF16)
            c0 = W_LOCAL + g * W_QKV
            z = _dot(hp, w_ref[:, c0:c0 + W_QKV])
        q = z[:, 0:256] * (HEAD_DIM ** -0.5)
        pieces = []
        for pr in range(2):
            qp = q[:, pr * 128:(pr + 1) * 128]
            pieces.append(jnp.where(sub0, qp, 0.0))
            pieces.append(jnp.where(sub0, 0.0, qp))
            pieces.append(z[:, 256 + pr * 128:256 + (pr + 1) * 128])
            pieces.append(z[:, 512 + pr * 128:512 + (pr + 1) * 128])
        val = jnp.concatenate(pieces, axis=1).astype(BF16)
        for r in range(dil):
            q_ref[0, r, pl.ds(s * n, n), :] = val[r * n:(r + 1) * n]

    u = _gelu_tanh(zl[:, 0:256])
    v = _gelu_tanh(zl[:, 256:512])
    mu = jnp.mean(v, axis=-1, keepdims=True)
    vc = v - mu
    var = jnp.mean(vc * vc, axis=-1, keepdims=True)
    v = vc * lax.rsqrt(var + EPS) * lng_ref[...] + lnb_ref[...]
    row = lax.broadcasted_iota(jnp.int32, (CHUNK, A_GROUPS * CHUNK), 0)
    col = lax.broadcasted_iota(jnp.int32, (CHUNK, A_GROUPS * CHUNK), 1) % CHUNK
    w_all = jnp.concatenate([ws_ref[g] for g in range(A_GROUPS)], axis=1)
    wtril = jnp.where(row >= col, w_all, 0.0).astype(BF16)
    grp = lax.broadcasted_iota(jnp.int32, (CHUNK, W_BRANCH), 1) // (W_BRANCH // A_GROUPS)
    svs = []
    for c in range(ts // CHUNK):
        vch = v[c * CHUNK:(c + 1) * CHUNK]
        stacked = jnp.concatenate(
            [jnp.where(grp == g, vch, 0.0) for g in range(A_GROUPS)], axis=0).astype(BF16)
        svs.append(_dot(wtril, stacked) + bs_ref[...])
    sv = jnp.concatenate(svs, axis=0)
    pa_ref[0, rows, :] = (u * sv).astype(BF16)

    prod = zl[:, 768:1024] * zl[:, 1024:1280]

    bbuf[0:CONV_HALO] = jnp.where(i > 0, bbuf_prev[ts:ts + CONV_HALO], 0.0)
    dbuf[0:POOL_HALO] = jnp.where(i > 0, dbuf_prev[ts:ts + POOL_HALO], 0.0)
    bbuf[CONV_HALO:ts + CONV_HALO] = prod
    cw = conv_ref[...]
    conv = (cw[0:1] * bbuf[pl.ds(CONV_HALO - 2, ts), :]
            + cw[1:2] * bbuf[pl.ds(CONV_HALO - 1, ts), :]
            + cw[2:3] * prod)
    pb_ref[0, rows, :] = (zl[:, 512:768] * conv).astype(BF16)

    dz = zl[:, 1280:1536]
    dbuf[POOL_HALO:ts + POOL_HALO] = dz
    lane = lax.broadcasted_iota(jnp.int32, (ts, 128), 1)
    first_half = lane < 64
    tpos = (lax.broadcasted_iota(jnp.int32, (ts, 128), 0) + (i * ts + 1)).astype(F32)

    def window_sums(e, levels):
        out = []
        s = e
        for k in range(levels):
            s = s + pltpu.roll(s, 1 << k, 0)
            out.append(s[POOL_HALO:])
        return out

    lo = window_sums(dbuf[:, 0:128], 2)
    hi = window_sums(dbuf[:, 128:256], 4)
    pooled_lo = jnp.where(first_half, lo[0], lo[1]) / jnp.minimum(tpos, jnp.where(first_half, 2.0, 4.0))
    pooled_hi = jnp.where(first_half, hi[2], hi[3]) / jnp.minimum(tpos, jnp.where(first_half, 8.0, 16.0))
    y = jnp.concatenate([pooled_lo, pooled_hi], axis=1) - dz
    pd_ref[0, rows, :] = (_dot(y.astype(BF16), wpool_ref[...]) * dscale_ref[...]).astype(BF16)


def _mix_call(layer, x, g, w_mix, lng, lnb, ws, bs_full, conv, wpool_bd, dscale, cast_jobs=()):
    B, S, D = x.shape
    ts = MIX_TILE
    nt = S // ts
    tok = lambda b, i: (b, i, 0)
    casts, c_ops, c_in, c_shapes, c_out = _cast_plumbing(cast_jobs, B * nt, lambda b, i: b * nt + i)
    out_shape = (
        jax.ShapeDtypeStruct((B, S, W_BRANCH), BF16),
        jax.ShapeDtypeStruct((B, S, W_BRANCH), BF16),
        jax.ShapeDtypeStruct((B, S, W_BRANCH), BF16),
    ) + tuple(jax.ShapeDtypeStruct((B, dil, S // dil, W_QKV_OUT), BF16) for dil in DILATIONS)
    in_specs = [
        pl.BlockSpec((1, ts, D), tok),
    ] + [_const_spec(a.shape) for a in (g, w_mix, lng, lnb, ws, bs_full, conv, wpool_bd, dscale)]
    out_specs = (
        pl.BlockSpec((1, ts, W_BRANCH), tok),
        pl.BlockSpec((1, ts, W_BRANCH), tok),
        pl.BlockSpec((1, ts, W_BRANCH), tok),
    ) + tuple(pl.BlockSpec((1, dil, ts // dil, W_QKV_OUT), lambda b, i: (b, 0, i, 0))
              for dil in DILATIONS)
    return pl.pallas_call(
        functools.partial(_mix_kernel, layer=layer, casts=casts),
        grid=(B, nt),
        in_specs=in_specs + c_in,
        out_specs=out_specs + tuple(c_out),
        out_shape=out_shape + tuple(c_shapes),
        scratch_shapes=[
            pltpu.VMEM((ts // MIX_SUB, len(DILATIONS) - 1, D // 128, MIX_SUB, 128), F32),
            pltpu.VMEM((ts // MIX_SUB, MIX_SUB + CONV_HALO, W_BRANCH), F32),
            pltpu.VMEM((ts // MIX_SUB, MIX_SUB + POOL_HALO, W_BRANCH), F32),
        ],
        compiler_params=pltpu.CompilerParams(
            dimension_semantics=("arbitrary", "arbitrary"), vmem_limit_bytes=VMEM_LIMIT),
        name="mix",
    )(x, g, w_mix, lng, lnb, ws, bs_full, conv, wpool_bd, dscale, *c_ops)


def _attn_kernel(*refs, casts):
    ins, cast_src, (out_ref,), cast_dst, (o_scr, l_scr, t_scr) = _split_refs(refs, 5, 1, casts)
    q0_ref, q1_ref, q2_ref, bcur_ref, bprev_ref = ins
    _run_casts(casts, cast_src, cast_dst)
    q_refs = (q0_ref, q1_ref, q2_ref)
    S = out_ref.shape[1]
    pr = pl.program_id(1)
    nb = S // CHUNK

    def cols(g, c0):
        return q_refs[g][0, :, c0:c0 + 128].reshape(nb, CHUNK, 128)

    jb = ATTN_BLOCKS_PER_PIECE
    sub0 = lax.broadcasted_iota(jnp.int32, (jb, CHUNK, 128), 2) < HEAD_DIM
    qk = lambda a, b: jnp.einsum("jqd,jkd->jqk", a, b, preferred_element_type=F32)
    pv = lambda a, b: jnp.einsum("jqk,jkd->jqd", a, b, preferred_element_type=F32)
    hh0 = 2 * pr

    for g, dil in enumerate(DILATIONS):
        blocks_per_seq = nb // dil
        has_prev = blocks_per_seq > 1
        q = jnp.concatenate([cols(g, 0), cols(g, 128)], axis=1)
        k = cols(g, 256)
        v = jnp.concatenate([cols(g, 384), jnp.ones((nb, CHUNK, 128), BF16)], axis=-1)
        bias_cur = jnp.concatenate([bcur_ref[g, hh0], bcur_ref[g, hh0 + 1]], axis=0)[None]
        if has_prev:
            k = jnp.concatenate([jnp.concatenate([k[:1], k[:-1]], axis=0), k], axis=1)
            v = jnp.concatenate([jnp.concatenate([v[:1], v[:-1]], axis=0), v], axis=1)
            bias_prev = jnp.concatenate([bprev_ref[g, hh0], bprev_ref[g, hh0 + 1]], axis=0)
        for j0 in range(0, nb, jb):
            s = qk(q[j0:j0 + jb], k[j0:j0 + jb])
            s_cur = s[:, :, -CHUNK:] + bias_cur
            s_prev = [None if not has_prev or (j0 + jj) % blocks_per_seq == 0
                      else s[jj, :, :CHUNK] + bias_prev for jj in range(jb)]
            top = jnp.stack([s_cur[jj] if s_prev[jj] is None else jnp.maximum(s_cur[jj], s_prev[jj])
                             for jj in range(jb)])
            row_max = jnp.max(top, axis=-1, keepdims=True)
            e = jnp.exp(s_cur - row_max)
            if has_prev:
                e_prev = jnp.stack([jnp.zeros((2 * CHUNK, CHUNK), F32) if s_prev[jj] is None
                                    else jnp.exp(s_prev[jj] - row_max[jj]) for jj in range(jb)])
                e = jnp.concatenate([e_prev, e], axis=-1)
            acc_den = pv(e.astype(BF16), v[j0:j0 + jb])
            den = acc_den[:, :, 128:]
            o2 = acc_den[:, :, :128] / den
            l2 = row_max + jnp.log(den)
            o = jnp.where(sub0, o2[:, :CHUNK], o2[:, CHUNK:])
            lse = jnp.where(sub0, l2[:, :CHUNK], l2[:, CHUNK:])
            two_stage = g >= 2
            prev_dil = DILATIONS[g - 1] if two_stage else 1
            step = dil // prev_dil
            for jj in range(jb):
                r, blk = divmod(j0 + jj, blocks_per_seq)
                start = (r % prev_dil) * (S // prev_dil) + blk * CHUNK * step + r // prev_dil
                rows = pl.ds(start, CHUNK, stride=step) if step > 1 else pl.ds(start, CHUNK)
                if two_stage:
                    t_scr[0, rows, :] = o[jj]
                    t_scr[1, rows, :] = lse[jj]
                else:
                    o_scr[g, rows, :] = o[jj]
                    l_scr[g, rows, :] = lse[jj]
        if two_stage:
            n_prev = S // prev_dil
            for r in range(prev_dil):
                rows = pl.ds(r, n_prev, stride=prev_dil)
                o_scr[g, rows, :] = t_scr[0, r * n_prev:(r + 1) * n_prev, :]
                l_scr[g, rows, :] = t_scr[1, r * n_prev:(r + 1) * n_prev, :]

    rows = 256
    for c in range(S // rows):
        sl = pl.ds(c * rows, rows)
        l0, l1, l2 = l_scr[0, sl, :], l_scr[1, sl, :], l_scr[2, sl, :]
        m = jnp.maximum(jnp.maximum(l0, l1), l2)
        e0, e1, e2 = jnp.exp(l0 - m), jnp.exp(l1 - m), jnp.exp(l2 - m)
        num = e0 * o_scr[0, sl, :] + e1 * o_scr[1, sl, :] + e2 * o_scr[2, sl, :]
        out_ref[0, sl, :] = (num / (e0 + e1 + e2)).astype(BF16)


def _attn_call(qs, bcur, bprev, cast_jobs=()):
    B = qs[0].shape[0]
    S = qs[0].shape[1] * qs[0].shape[2]
    operands = [q.reshape(B, S, W_QKV_OUT) for q in qs]
    spec = pl.BlockSpec((1, S, W_QKV_OUT // 2), lambda b, p: (b, 0, p))
    casts, c_ops, c_in, c_shapes, c_out = _cast_plumbing(cast_jobs, B * 2, lambda b, p: b * 2 + p)
    return pl.pallas_call(
        functools.partial(_attn_kernel, casts=casts),
        grid=(B, 2),
        in_specs=[spec, spec, spec, _const_spec(bcur.shape), _const_spec(bprev.shape)] + c_in,
        out_specs=(pl.BlockSpec((1, S, 128), lambda b, p: (b, 0, p)),) + tuple(c_out),
        out_shape=(jax.ShapeDtypeStruct((B, S, W_BRANCH), BF16),) + tuple(c_shapes),
        scratch_shapes=[
            pltpu.VMEM((3, S, 128), F32),
            pltpu.VMEM((3, S, 128), F32),
            pltpu.VMEM((2, S, 128), F32),
        ],
        compiler_params=pltpu.CompilerParams(
            dimension_semantics=("arbitrary", "arbitrary"), vmem_limit_bytes=VMEM_LIMIT),
        name="attn",
    )(*operands, bcur, bprev, *c_ops)


def _merge_kernel(*refs, layer, casts):
    ins, cast_src, (o_ref,), cast_dst, _ = _split_refs(refs, 9, 1, casts)
    x_ref, pa_ref, pb_ref, pc_ref, pd_ref, g_ref, wg_ref, wout_ref, wo_ref = ins
    g_ref = _layer_row(g_ref, layer)
    _run_casts(casts, cast_src, cast_dst)
    for s in range(TOK_TILE // MERGE_SUB):
        rows = pl.ds(s * MERGE_SUB, MERGE_SUB)
        ys = [_dot(p_ref[rows, :], wout_ref[br]) for br, p_ref in enumerate((pa_ref, pb_ref, pc_ref, pd_ref))]
        x = x_ref[rows, :]
        hb = _rmsnorm(x, g_ref[...]).astype(BF16)
        merged = None
        gates = _dot(hb, wg_ref[...])
        for br, y in enumerate(ys):
            term = jax.nn.sigmoid(gates[:, br * D_MODEL:(br + 1) * D_MODEL]) * y
            merged = term if merged is None else merged + term
        o_ref[rows, :] = x + _dot(merged.astype(BF16), wo_ref[...])


def _merge_call(layer, x, pa, pb, pc, pd, g, w_gate, w_out, w_o, cast_jobs=()):
    T, D = x.shape
    tt = TOK_TILE
    tok = lambda i: (i, 0)
    pspec = pl.BlockSpec((tt, W_BRANCH), tok)
    casts, c_ops, c_in, c_shapes, c_out = _cast_plumbing(cast_jobs, T // tt, lambda i: i)
    return pl.pallas_call(
        functools.partial(_merge_kernel, layer=layer, casts=casts),
        grid=(T // tt,),
        in_specs=[pl.BlockSpec((tt, D), tok), pspec, pspec, pspec, pspec,
                  _const_spec(g.shape), _const_spec(w_gate.shape), _const_spec(w_out.shape),
                  _const_spec(w_o.shape)] + c_in,
        out_specs=(pl.BlockSpec((tt, D), tok),) + tuple(c_out),
        out_shape=(jax.ShapeDtypeStruct((T, D), F32),) + tuple(c_shapes),
        compiler_params=pltpu.CompilerParams(
            dimension_semantics=("arbitrary",), vmem_limit_bytes=VMEM_LIMIT),
        name="merge",
    )(x, pa, pb, pc, pd, g, w_gate, w_out, w_o, *c_ops)


def _ffn_kernel(*refs, layer, final_norm, casts):
    (x_ref, g_ref, w1_ref, w2_ref, fg_ref), cast_src, (o_ref,), cast_dst, _ = _split_refs(refs, 5, 1, casts)
    g_ref = _layer_row(g_ref, layer)
    _run_casts(casts, cast_src, cast_dst)
    for s in range(TOK_TILE // FFN_SUB):
        rows = pl.ds(s * FFN_SUB, FFN_SUB)
        x = x_ref[rows, :]
        hb = _rmsnorm(x, g_ref[...]).astype(BF16)
        a = _dot(hb, w1_ref[...])
        y = x + _dot(jnp.square(jnp.maximum(a, 0.0)).astype(BF16), w2_ref[...])
        if final_norm:
            y = _rmsnorm(y, fg_ref[...])
        o_ref[rows, :] = y


def _ffn_call(layer, x, g, w1, w2, final_g, final_norm, cast_jobs=()):
    T, D = x.shape
    tt = TOK_TILE
    tok = lambda i: (i, 0)
    casts, c_ops, c_in, c_shapes, c_out = _cast_plumbing(cast_jobs, T // tt, lambda i: i)
    return pl.pallas_call(
        functools.partial(_ffn_kernel, layer=layer, final_norm=final_norm, casts=casts),
        grid=(T // tt,),
        in_specs=[pl.BlockSpec((tt, D), tok), _const_spec(g.shape), _const_spec(w1.shape),
                  _const_spec(w2.shape), _const_spec((1, D))] + c_in,
        out_specs=(pl.BlockSpec((tt, D), tok),) + tuple(c_out),
        out_shape=(jax.ShapeDtypeStruct((T, D), F32),) + tuple(c_shapes),
        compiler_params=pltpu.CompilerParams(
            dimension_semantics=("arbitrary",), vmem_limit_bytes=VMEM_LIMIT),
        name="ffn",
    )(x, g, w1, w2, final_g, *c_ops)


def _t5_bucket_np(dist):
    max_exact = N_BUCKETS // 2
    d_f = np.maximum(dist, 1).astype(np.float32)
    ratio = np.log(d_f / np.float32(max_exact)) / np.float32(math.log(MAX_DISTANCE / max_exact))
    large = max_exact + (ratio * np.float32(N_BUCKETS - max_exact)).astype(np.int32)
    large = np.minimum(large, N_BUCKETS - 1)
    return np.where(dist < max_exact, dist, large)


def _bucket_indices():
    qi = np.arange(CHUNK)[:, None]
    ki = np.arange(CHUNK)[None, :]
    d_cur = qi - ki
    d_prev = qi + CHUNK - ki
    cur = [np.where(d_cur >= 0, _t5_bucket_np(np.clip(d_cur, 0, None) * dil), -1) for dil in DILATIONS]
    prev = [np.where(d_prev <= CHUNK, _t5_bucket_np(d_prev * dil), -1) for dil in DILATIONS[:2]]
    return np.stack(cur).astype(np.int32), np.stack(prev).astype(np.int32)


def _bias_kernel(tab_ref, icur_ref, iprev_ref, bcur_ref, bprev_ref):
    for idx_ref, out_ref in ((icur_ref, bcur_ref), (iprev_ref, bprev_ref)):
        for g in range(idx_ref.shape[0]):
            idx = idx_ref[g]
            accs = [jnp.full((CHUNK, CHUNK), NEG_BIG, F32) for _ in range(HEADS_PER_GROUP)]
            for b in range(N_BUCKETS):
                hit = idx == b
                for h in range(HEADS_PER_GROUP):
                    accs[h] = jnp.where(hit, tab_ref[b, g * HEADS_PER_GROUP + h], accs[h])
            for h in range(HEADS_PER_GROUP):
                out_ref[g, h] = accs[h]


def _bias_tables(rel_bias):
    icur, iprev = _bucket_indices()
    vm = pl.BlockSpec(memory_space=pltpu.VMEM)
    return pl.pallas_call(
        _bias_kernel,
        in_specs=[pl.BlockSpec(memory_space=pltpu.SMEM), vm, vm],
        out_specs=(vm, vm),
        out_shape=(jax.ShapeDtypeStruct((3, HEADS_PER_GROUP, CHUNK, CHUNK), F32),
                   jax.ShapeDtypeStruct((2, HEADS_PER_GROUP, CHUNK, CHUNK), F32)),
        name="bias",
    )(rel_bias, jnp.asarray(icur), jnp.asarray(iprev))


def kernel(x, norm_mix_g, w_in, a_ln_g, a_ln_b, a_ws, a_bs, w_a_out, b_conv, w_b_out, rel_bias,
           w_c_out, d_w, d_scale, w_d_out, w_o, norm_ff_g, w_ff1, w_ff2, final_g):
    B, S, D = x.shape
    depth = w_in.shape[0]
    bcur, bprev = _bias_tables(rel_bias)
    w_outs = (w_a_out, w_b_out, w_c_out, w_d_out)
    w_mix_b = jnp.concatenate([w_in[0][:, a:b] for a, b in _MIX_COLS], axis=1).astype(BF16)
    bs_full = jnp.repeat(jnp.swapaxes(a_bs, 1, 2), W_BRANCH // A_GROUPS, axis=2)
    eye = jnp.eye(POOL_GROUPS, dtype=d_w.dtype)
    wpool_bd = jnp.einsum("lgde,gh->lgdhe", d_w, eye).reshape(depth, W_BRANCH, W_BRANCH).astype(BF16)
    final_row = final_g[None]
    T = B * S
    for l in range(depth):
        more = l + 1 < depth

        mix_jobs = (("plain", 0, (w_o,)), ("w_out", 0, w_outs)) if l == 0 else ()
        pa, pb, pd, q0, q1, q2, *cast = _mix_call(
            l, x, norm_mix_g, w_mix_b, a_ln_g, a_ln_b, a_ws, bs_full, b_conv, wpool_bd, d_scale,
            cast_jobs=mix_jobs)
        if l == 0:
            w_o_b, w_out_b = cast

        attn_jobs = (("plain", 0, (w_ff1,)), ("plain", 0, (w_ff2,)), ("gate", 0, (w_in,))) if l == 0 else ()
        pc, *cast = _attn_call((q0, q1, q2), bcur, bprev, cast_jobs=attn_jobs)
        if l == 0:
            w_ff1_b, w_ff2_b, w_gate_b = cast

        merge_jobs = (("w_in", l + 1, (w_in,)),) if more else ()
        x2, *cast = _merge_call(l, x.reshape(T, D), pa.reshape(T, -1), pb.reshape(T, -1),
                                pc.reshape(T, -1), pd.reshape(T, -1), norm_mix_g,
                                w_gate_b, w_out_b, w_o_b, cast_jobs=merge_jobs)
        if more:
            w_mix_b, w_gate_b = cast

        ffn_jobs = (("plain", l + 1, (w_ff1,)), ("plain", l + 1, (w_ff2,)), ("plain", l + 1, (w_o,)),
                    ("w_out", l + 1, w_outs)) if more else ()
        x2, *cast = _ffn_call(l, x2, norm_ff_g, w_ff1_b, w_ff2_b, final_row,
                              final_norm=not more, cast_jobs=ffn_jobs)
        if more:
            w_ff1_b, w_ff2_b, w_o_b, w_out_b = cast
        x = x2.reshape(B, S, D)
    return x
```

```python
import functools
import math

import jax
import jax.numpy as jnp
import numpy as np
from jax import lax
from jax.experimental import pallas as pl
from jax.experimental.pallas import tpu as pltpu

F32 = jnp.float32
BF16 = jnp.bfloat16

D_MODEL = 1024
W_BRANCH = 256
A_GROUPS = 4
CHUNK = 128
CONV_WIDTH = 3
DILATIONS = (1, 4, 16)
WINDOWS = (128, 512, 2048)
assert all(w == CHUNK * d for w, d in zip(WINDOWS, DILATIONS))
HEADS_PER_GROUP = 4
HEAD_DIM = 64
POOL_WINDOWS = (2, 4, 8, 16)
POOL_GROUPS = len(POOL_WINDOWS)
POOL_HALO = 16
CONV_HALO = 8
N_BRANCH = 4
N_BUCKETS = 32
MAX_DISTANCE = 2048
EPS = 1e-6
NEG_BIG = -1e30
COL_Q = 5 * W_BRANCH
COL_K = COL_Q + 3 * W_BRANCH
COL_V = COL_K + 3 * W_BRANCH
COL_D = COL_V + 3 * W_BRANCH
COL_GATE = COL_D + W_BRANCH
W_LOCAL = 6 * W_BRANCH
W_QKV = 3 * W_BRANCH
W_QKV_OUT = 4 * W_BRANCH

MIX_TILE = 1024
MIX_SUB = 256
TOK_TILE = 1024
MERGE_SUB = 256
FFN_SUB = 512
ATTN_BLOCKS_PER_PIECE = 8
VMEM_LIMIT = 56 * 1024 * 1024


def _rmsnorm(x, g):
    return x * lax.rsqrt(jnp.mean(x * x, axis=-1, keepdims=True) + EPS) * g


def _gelu_tanh(x):
    c = math.sqrt(2.0 / math.pi)
    return x * (0.5 * (1.0 + jnp.tanh(c * (x + 0.044715 * (x * x * x)))))


def _dot(a, b):
    return jnp.dot(a, b, preferred_element_type=F32)


def _const_spec(shape):
    n = len(shape)
    return pl.BlockSpec(shape, lambda *_: (0,) * n, pipeline_mode=pl.Buffered(1))


_CAST_ARITY = {"plain": (1, 1), "w_in": (1, 2), "gate": (1, 1), "w_out": (N_BRANCH, 1)}
_MIX_COLS = ((0, COL_Q), (COL_D, COL_GATE)) + tuple(
    (col + g * W_BRANCH, col + (g + 1) * W_BRANCH)
    for g in range(len(DILATIONS)) for col in (COL_Q, COL_K, COL_V))


def _cast_plumbing(jobs, n_steps, step_of):
    kinds, operands, in_specs, out_shapes, out_specs = [], [], [], [], []
    for kind, layer, srcs in jobs:
        kinds.append(kind)
        R, C = srcs[0].shape[1:]
        rows = R // n_steps
        assert rows * n_steps == R and rows % 16 == 0, (kind, R, n_steps)
        for src in srcs:
            operands.append(src)
            in_specs.append(pl.BlockSpec((1, rows, C), lambda *g, layer=layer: (layer, step_of(*g), 0)))
        row_blk = lambda *g: (step_of(*g), 0)
        if kind == "w_out":
            out_shapes.append(jax.ShapeDtypeStruct((len(srcs), R, C), BF16))
            out_specs.append(pl.BlockSpec((len(srcs), rows, C), lambda *g: (0, step_of(*g), 0)))
        elif kind in ("w_in", "gate"):
            widths = (C - COL_GATE,) if kind == "gate" else (sum(b - a for a, b in _MIX_COLS), C - COL_GATE)
            for width in widths:
                out_shapes.append(jax.ShapeDtypeStruct((R, width), BF16))
                out_specs.append(pl.BlockSpec((rows, width), row_blk))
        else:
            out_shapes.append(jax.ShapeDtypeStruct((R, C), BF16))
            out_specs.append(pl.BlockSpec((rows, C), row_blk))
    return tuple(kinds), operands, in_specs, out_shapes, out_specs


def _split_refs(refs, n_in, n_out, kinds):
    n_ci = sum(_CAST_ARITY[k][0] for k in kinds)
    n_co = sum(_CAST_ARITY[k][1] for k in kinds)
    a, b, c = n_in, n_in + n_ci, n_in + n_ci + n_out
    return refs[:a], refs[a:b], refs[b:c], refs[c:c + n_co], refs[c + n_co:]


def _run_casts(kinds, src_refs, dst_refs):
    i = o = 0
    for kind in kinds:
        n_i, n_o = _CAST_ARITY[kind]
        srcs, dsts = src_refs[i:i + n_i], dst_refs[o:o + n_o]
        i, o = i + n_i, o + n_o
        if kind == "plain":
            dsts[0][...] = srcs[0][0].astype(BF16)
        elif kind == "w_out":
            for k, src in enumerate(srcs):
                dsts[0][k] = src[0].astype(BF16)
        elif kind == "gate":
            dsts[0][...] = srcs[0][0, :, COL_GATE:].astype(BF16)
        else:
            c = 0
            for a, b in _MIX_COLS:
                dsts[0][:, c:c + b - a] = srcs[0][0, :, a:b].astype(BF16)
                c += b - a
            dsts[1][...] = srcs[0][0, :, COL_GATE:].astype(BF16)


def _layer_row(ref, layer):
    return ref.at[pl.ds(layer, 1)]


def _mix_kernel(*refs, layer, casts):
    ins, cast_src, outs, cast_dst, (h_scr, bbuf, dbuf) = _split_refs(refs, 10, 6, casts)
    x_ref, g_ref, w_ref, lng_ref, lnb_ref, ws_ref, bs_ref, conv_ref, wpool_ref, dscale_ref = ins
    g_ref, lng_ref, lnb_ref, dscale_ref = (_layer_row(r, layer) for r in (g_ref, lng_ref, lnb_ref, dscale_ref))
    ws_ref, bs_ref, conv_ref, wpool_ref = (r.at[layer] for r in (ws_ref, bs_ref, conv_ref, wpool_ref))
    pa_ref, pb_ref, pd_ref, q0_ref, q1_ref, q2_ref = outs
    _run_casts(casts, cast_src, cast_dst)
    n_sub = MIX_TILE // MIX_SUB
    first = (pl.program_id(0) == 0) & (pl.program_id(1) == 0)

    @pl.when(first)
    def _():
        bbuf[...] = jnp.zeros(bbuf.shape, F32)
        dbuf[...] = jnp.zeros(dbuf.shape, F32)

    for s in range(n_sub):
        _mix_subtile(s, pl.program_id(1) * n_sub + s, x_ref, g_ref, w_ref, lng_ref, lnb_ref, ws_ref,
                     bs_ref, conv_ref, wpool_ref, dscale_ref, pa_ref, pb_ref, pd_ref,
                     (q0_ref, q1_ref, q2_ref), h_scr.at[s], bbuf.at[s], bbuf.at[(s - 1) % n_sub],
                     dbuf.at[s], dbuf.at[(s - 1) % n_sub])


def _mix_subtile(s, i, x_ref, g_ref, w_ref, lng_ref, lnb_ref, ws_ref, bs_ref, conv_ref, wpool_ref,
                 dscale_ref, pa_ref, pb_ref, pd_ref, q_refs, h_scr, bbuf, bbuf_prev, dbuf, dbuf_prev):
    ts = MIX_SUB
    rows = pl.ds(s * ts, ts)
    x = x_ref[0, rows, :]
    h = _rmsnorm(x, g_ref[...])
    for c in range(D_MODEL // 128):
        h_scr[0, c] = h[:, c * 128:(c + 1) * 128]
    hb = h.astype(BF16)
    zl = _dot(hb, w_ref[:, 0:W_LOCAL + W_QKV])

    sub0 = lax.broadcasted_iota(jnp.int32, (ts, 128), 1) < HEAD_DIM
    for g, (q_ref, dil) in enumerate(zip(q_refs, DILATIONS)):
        n = ts // dil
        if dil == 1:
            z = zl[:, W_LOCAL:W_LOCAL + W_QKV]
        else:
            prev_dil = DILATIONS[g - 1]
            src, n_prev, step = h_scr.at[g - 1], ts // prev_dil, dil // prev_dil
            slabs = [jnp.concatenate([src[c, pl.ds((r % prev_dil) * n_prev + r // prev_dil, n, stride=step), :]
                                      for r in range(dil)], axis=0) for c in range(D_MODEL // 128)]
            if g + 1 < len(DILATIONS):
                for c, slab in enumerate(slabs):
                    h_scr[g, c] = slab
            hp = jnp.concatenate(slabs, axis=1).astype(BF16)
            c0 = W_LOCAL + g * W_QKV
            z = _dot(hp, w_ref[:, c0:c0 + W_QKV])
        q = z[:, 0:256] * (HEAD_DIM ** -0.5)
        pieces = []
        for pr in range(2):
            qp = q[:, pr * 128:(pr + 1) * 128]
            pieces.append(jnp.where(sub0, qp, 0.0))
            pieces.append(jnp.where(sub0, 0.0, qp))
            pieces.append(z[:, 256 + pr * 128:256 + (pr + 1) * 128])
            pieces.append(z[:, 512 + pr * 128:512 + (pr + 1) * 128])
        val = jnp.concatenate(pieces, axis=1).astype(BF16)
        for r in range(dil):
            q_ref[0, r, pl.ds(s * n, n), :] = val[r * n:(r + 1) * n]

    u = _gelu_tanh(zl[:, 0:256])
    v = _gelu_tanh(zl[:, 256:512])
    mu = jnp.mean(v, axis=-1, keepdims=True)
    vc = v - mu
    var = jnp.mean(vc * vc, axis=-1, keepdims=True)
    v = vc * lax.rsqrt(var + EPS) * lng_ref[...] + lnb_ref[...]
    row = lax.broadcasted_iota(jnp.int32, (CHUNK, A_GROUPS * CHUNK), 0)
    col = lax.broadcasted_iota(jnp.int32, (CHUNK, A_GROUPS * CHUNK), 1) % CHUNK
    w_all = jnp.concatenate([ws_ref[g] for g in range(A_GROUPS)], axis=1)
    wtril = jnp.where(row >= col, w_all, 0.0).astype(BF16)
    grp = lax.broadcasted_iota(jnp.int32, (CHUNK, W_BRANCH), 1) // (W_BRANCH // A_GROUPS)
    svs = []
    for c in range(ts // CHUNK):
        vch = v[c * CHUNK:(c + 1) * CHUNK]
        stacked = jnp.concatenate(
            [jnp.where(grp == g, vch, 0.0) for g in range(A_GROUPS)], axis=0).astype(BF16)
        svs.append(_dot(wtril, stacked) + bs_ref[...])
    sv = jnp.concatenate(svs, axis=0)
    pa_ref[0, rows, :] = (u * sv).astype(BF16)

    prod = zl[:, 768:1024] * zl[:, 1024:1280]

    bbuf[0:CONV_HALO] = jnp.where(i > 0, bbuf_prev[ts:ts + CONV_HALO], 0.0)
    dbuf[0:POOL_HALO] = jnp.where(i > 0, dbuf_prev[ts:ts + POOL_HALO], 0.0)
    bbuf[CONV_HALO:ts + CONV_HALO] = prod
    cw = conv_ref[...]
    conv = (cw[0:1] * bbuf[pl.ds(CONV_HALO - 2, ts), :]
            + cw[1:2] * bbuf[pl.ds(CONV_HALO - 1, ts), :]
            + cw[2:3] * prod)
    pb_ref[0, rows, :] = (zl[:, 512:768] * conv).astype(BF16)

    dz = zl[:, 1280:1536]
    dbuf[POOL_HALO:ts + POOL_HALO] = dz
    lane = lax.broadcasted_iota(jnp.int32, (ts, 128), 1)
    first_half = lane < 64
    tpos = (lax.broadcasted_iota(jnp.int32, (ts, 128), 0) + (i * ts + 1)).astype(F32)

    def window_sums(e, levels):
        out = []
        s = e
        for k in range(levels):
            s = s + pltpu.roll(s, 1 << k, 0)
            out.append(s[POOL_HALO:])
        return out

    lo = window_sums(dbuf[:, 0:128], 2)
    hi = window_sums(dbuf[:, 128:256], 4)
    pooled_lo = jnp.where(first_half, lo[0], lo[1]) / jnp.minimum(tpos, jnp.where(first_half, 2.0, 4.0))
    pooled_hi = jnp.where(first_half, hi[2], hi[3]) / jnp.minimum(tpos, jnp.where(first_half, 8.0, 16.0))
    y = jnp.concatenate([pooled_lo, pooled_hi], axis=1) - dz
    pd_ref[0, rows, :] = (_dot(y.astype(BF16), wpool_ref[...]) * dscale_ref[...]).astype(BF16)


def _mix_call(layer, x, g, w_mix, lng, lnb, ws, bs_full, conv, wpool_bd, dscale, cast_jobs=()):
    B, S, D = x.shape
    ts = MIX_TILE
    nt = S // ts
    tok = lambda b, i: (b, i, 0)
    casts, c_ops, c_in, c_shapes, c_out = _cast_plumbing(cast_jobs, B * nt, lambda b, i: b * nt + i)
    out_shape = (
        jax.ShapeDtypeStruct((B, S, W_BRANCH), BF16),
        jax.ShapeDtypeStruct((B, S, W_BRANCH), BF16),
        jax.ShapeDtypeStruct((B, S, W_BRANCH), BF16),
    ) + tuple(jax.ShapeDtypeStruct((B, dil, S // dil, W_QKV_OUT), BF16) for dil in DILATIONS)
    in_specs = [
        pl.BlockSpec((1, ts, D), tok),
    ] + [_const_spec(a.shape) for a in (g, w_mix, lng, lnb, ws, bs_full, conv, wpool_bd, dscale)]
    out_specs = (
        pl.BlockSpec((1, ts, W_BRANCH), tok),
        pl.BlockSpec((1, ts, W_BRANCH), tok),
        pl.BlockSpec((1, ts, W_BRANCH), tok),
    ) + tuple(pl.BlockSpec((1, dil, ts // dil, W_QKV_OUT), lambda b, i: (b, 0, i, 0))
              for dil in DILATIONS)
    return pl.pallas_call(
        functools.partial(_mix_kernel, layer=layer, casts=casts),
        grid=(B, nt),
        in_specs=in_specs + c_in,
        out_specs=out_specs + tuple(c_out),
        out_shape=out_shape + tuple(c_shapes),
        scratch_shapes=[
            pltpu.VMEM((ts // MIX_SUB, len(DILATIONS) - 1, D // 128, MIX_SUB, 128), F32),
            pltpu.VMEM((ts // MIX_SUB, MIX_SUB + CONV_HALO, W_BRANCH), F32),
            pltpu.VMEM((ts // MIX_SUB, MIX_SUB + POOL_HALO, W_BRANCH), F32),
        ],
        compiler_params=pltpu.CompilerParams(
            dimension_semantics=("arbitrary", "arbitrary"), vmem_limit_bytes=VMEM_LIMIT),
        name="mix",
    )(x, g, w_mix, lng, lnb, ws, bs_full, conv, wpool_bd, dscale, *c_ops)


def _attn_kernel(*refs, casts):
    ins, cast_src, (out_ref,), cast_dst, (o_scr, l_scr, t_scr) = _split_refs(refs, 5, 1, casts)
    q0_ref, q1_ref, q2_ref, bcur_ref, bprev_ref = ins
    _run_casts(casts, cast_src, cast_dst)
    q_refs = (q0_ref, q1_ref, q2_ref)
    S = out_ref.shape[1]
    pr = pl.program_id(1)
    nb = S // CHUNK

    def cols(g, c0):
        return q_refs[g][0, :, c0:c0 + 128].reshape(nb, CHUNK, 128)

    jb = ATTN_BLOCKS_PER_PIECE
    sub0 = lax.broadcasted_iota(jnp.int32, (jb, CHUNK, 128), 2) < HEAD_DIM
    qk = lambda a, b: jnp.einsum("jqd,jkd->jqk", a, b, preferred_element_type=F32)
    pv = lambda a, b: jnp.einsum("jqk,jkd->jqd", a, b, preferred_element_type=F32)
    hh0 = 2 * pr

    for g, dil in enumerate(DILATIONS):
        blocks_per_seq = nb // dil
        has_prev = blocks_per_seq > 1
        q = jnp.concatenate([cols(g, 0), cols(g, 128)], axis=1)
        k = cols(g, 256)
        v = jnp.concatenate([cols(g, 384), jnp.ones((nb, CHUNK, 128), BF16)], axis=-1)
        bias_cur = jnp.concatenate([bcur_ref[g, hh0], bcur_ref[g, hh0 + 1]], axis=0)[None]
        if has_prev:
            k = jnp.concatenate([jnp.concatenate([k[:1], k[:-1]], axis=0), k], axis=1)
            v = jnp.concatenate([jnp.concatenate([v[:1], v[:-1]], axis=0), v], axis=1)
            bias_prev = jnp.concatenate([bprev_ref[g, hh0], bprev_ref[g, hh0 + 1]], axis=0)
        for j0 in range(0, nb, jb):
            s = qk(q[j0:j0 + jb], k[j0:j0 + jb])
            s_cur = s[:, :, -CHUNK:] + bias_cur
            s_prev = [None if not has_prev or (j0 + jj) % blocks_per_seq == 0
                      else s[jj, :, :CHUNK] + bias_prev for jj in range(jb)]
            top = jnp.stack([s_cur[jj] if s_prev[jj] is None else jnp.maximum(s_cur[jj], s_prev[jj])
                             for jj in range(jb)])
            row_max = jnp.max(top, axis=-1, keepdims=True)
            e = jnp.exp(s_cur - row_max)
            if has_prev:
                e_prev = jnp.stack([jnp.zeros((2 * CHUNK, CHUNK), F32) if s_prev[jj] is None
                                    else jnp.exp(s_prev[jj] - row_max[jj]) for jj in range(jb)])
                e = jnp.concatenate([e_prev, e], axis=-1)
            acc_den = pv(e.astype(BF16), v[j0:j0 + jb])
            pick = lambda a: jnp.where(sub0, a[:, :CHUNK], a[:, CHUNK:])
            den = pick(acc_den[:, :, 128:])
            o = pick(acc_den[:, :, :128]) / den
            lse = pick(jnp.broadcast_to(row_max, (jb, 2 * CHUNK, 128))) + jnp.log(den)
            two_stage = g >= 2
            prev_dil = DILATIONS[g - 1] if two_stage else 1
            step = dil // prev_dil
            for jj in range(jb):
                r, blk = divmod(j0 + jj, blocks_per_seq)
                start = (r % prev_dil) * (S // prev_dil) + blk * CHUNK * step + r // prev_dil
                rows = pl.ds(start, CHUNK, stride=step) if step > 1 else pl.ds(start, CHUNK)
                if two_stage:
                    t_scr[0, rows, :] = o[jj]
                    t_scr[1, rows, :] = lse[jj]
                else:
                    o_scr[g, rows, :] = o[jj]
                    l_scr[g, rows, :] = lse[jj]
        if two_stage:
            n_prev = S // prev_dil
            for r in range(prev_dil):
                rows = pl.ds(r, n_prev, stride=prev_dil)
                o_scr[g, rows, :] = t_scr[0, r * n_prev:(r + 1) * n_prev, :]
                l_scr[g, rows, :] = t_scr[1, r * n_prev:(r + 1) * n_prev, :]

    rows = 256
    for c in range(S // rows):
        sl = pl.ds(c * rows, rows)
        l0, l1, l2 = l_scr[0, sl, :], l_scr[1, sl, :], l_scr[2, sl, :]
        m = jnp.maximum(jnp.maximum(l0, l1), l2)
        e0, e1, e2 = jnp.exp(l0 - m), jnp.exp(l1 - m), jnp.exp(l2 - m)
        num = e0 * o_scr[0, sl, :] + e1 * o_scr[1, sl, :] + e2 * o_scr[2, sl, :]
        out_ref[0, sl, :] = (num / (e0 + e1 + e2)).astype(BF16)


def _attn_call(qs, bcur, bprev, cast_jobs=()):
    B = qs[0].shape[0]
    S = qs[0].shape[1] * qs[0].shape[2]
    operands = [q.reshape(B, S, W_QKV_OUT) for q in qs]
    spec = pl.BlockSpec((1, S, W_QKV_OUT // 2), lambda b, p: (b, 0, p))
    casts, c_ops, c_in, c_shapes, c_out = _cast_plumbing(cast_jobs, B * 2, lambda b, p: b * 2 + p)
    return pl.pallas_call(
        functools.partial(_attn_kernel, casts=casts),
        grid=(B, 2),
        in_specs=[spec, spec, spec, _const_spec(bcur.shape), _const_spec(bprev.shape)] + c_in,
        out_specs=(pl.BlockSpec((1, S, 128), lambda b, p: (b, 0, p)),) + tuple(c_out),
        out_shape=(jax.ShapeDtypeStruct((B, S, W_BRANCH), BF16),) + tuple(c_shapes),
        scratch_shapes=[
            pltpu.VMEM((3, S, 128), F32),
            pltpu.VMEM((3, S, 128), F32),
            pltpu.VMEM((2, S, 128), F32),
        ],
        compiler_params=pltpu.CompilerParams(
            dimension_semantics=("arbitrary", "arbitrary"), vmem_limit_bytes=VMEM_LIMIT),
        name="attn",
    )(*operands, bcur, bprev, *c_ops)


def _merge_kernel(*refs, layer, casts):
    ins, cast_src, (o_ref,), cast_dst, _ = _split_refs(refs, 9, 1, casts)
    x_ref, pa_ref, pb_ref, pc_ref, pd_ref, g_ref, wg_ref, wout_ref, wo_ref = ins
    g_ref = _layer_row(g_ref, layer)
    _run_casts(casts, cast_src, cast_dst)
    for s in range(TOK_TILE // MERGE_SUB):
        rows = pl.ds(s * MERGE_SUB, MERGE_SUB)
        ys = [_dot(p_ref[rows, :], wout_ref[br]) for br, p_ref in enumerate((pa_ref, pb_ref, pc_ref, pd_ref))]
        x = x_ref[rows, :]
        hb = _rmsnorm(x, g_ref[...]).astype(BF16)
        merged = None
        gates = _dot(hb, wg_ref[...])
        for br, y in enumerate(ys):
            term = jax.nn.sigmoid(gates[:, br * D_MODEL:(br + 1) * D_MODEL]) * y
            merged = term if merged is None else merged + term
        o_ref[rows, :] = x + _dot(merged.astype(BF16), wo_ref[...])


def _merge_call(layer, x, pa, pb, pc, pd, g, w_gate, w_out, w_o, cast_jobs=()):
    T, D = x.shape
    tt = TOK_TILE
    tok = lambda i: (i, 0)
    pspec = pl.BlockSpec((tt, W_BRANCH), tok)
    casts, c_ops, c_in, c_shapes, c_out = _cast_plumbing(cast_jobs, T // tt, lambda i: i)
    return pl.pallas_call(
        functools.partial(_merge_kernel, layer=layer, casts=casts),
        grid=(T // tt,),
        in_specs=[pl.BlockSpec((tt, D), tok), pspec, pspec, pspec, pspec,
                  _const_spec(g.shape), _const_spec(w_gate.shape), _const_spec(w_out.shape),
                  _const_spec(w_o.shape)] + c_in,
        out_specs=(pl.BlockSpec((tt, D), tok),) + tuple(c_out),
        out_shape=(jax.ShapeDtypeStruct((T, D), F32),) + tuple(c_shapes),
        compiler_params=pltpu.CompilerParams(
            dimension_semantics=("arbitrary",), vmem_limit_bytes=VMEM_LIMIT),
        name="merge",
    )(x, pa, pb, pc, pd, g, w_gate, w_out, w_o, *c_ops)


def _ffn_kernel(*refs, layer, final_norm, casts):
    (x_ref, g_ref, w1_ref, w2_ref, fg_ref), cast_src, (o_ref,), cast_dst, _ = _split_refs(refs, 5, 1, casts)
    g_ref = _layer_row(g_ref, layer)
    _run_casts(casts, cast_src, cast_dst)
    for s in range(TOK_TILE // FFN_SUB):
        rows = pl.ds(s * FFN_SUB, FFN_SUB)
        x = x_ref[rows, :]
        hb = _rmsnorm(x, g_ref[...]).astype(BF16)
        a = _dot(hb, w1_ref[...])
        y = x + _dot(jnp.square(jnp.maximum(a, 0.0)).astype(BF16), w2_ref[...])
        if final_norm:
            y = _rmsnorm(y, fg_ref[...])
        o_ref[rows, :] = y


def _ffn_call(layer, x, g, w1, w2, final_g, final_norm, cast_jobs=()):
    T, D = x.shape
    tt = TOK_TILE
    tok = lambda i: (i, 0)
    casts, c_ops, c_in, c_shapes, c_out = _cast_plumbing(cast_jobs, T // tt, lambda i: i)
    return pl.pallas_call(
        functools.partial(_ffn_kernel, layer=layer, final_norm=final_norm, casts=casts),
        grid=(T // tt,),
        in_specs=[pl.BlockSpec((tt, D), tok), _const_spec(g.shape), _const_spec(w1.shape),
                  _const_spec(w2.shape), _const_spec((1, D))] + c_in,
        out_specs=(pl.BlockSpec((tt, D), tok),) + tuple(c_out),
        out_shape=(jax.ShapeDtypeStruct((T, D), F32),) + tuple(c_shapes),
        compiler_params=pltpu.CompilerParams(
            dimension_semantics=("arbitrary",), vmem_limit_bytes=VMEM_LIMIT),
        name="ffn",
    )(x, g, w1, w2, final_g, *c_ops)


def _t5_bucket_np(dist):
    max_exact = N_BUCKETS // 2
    d_f = np.maximum(dist, 1).astype(np.float32)
    ratio = np.log(d_f / np.float32(max_exact)) / np.float32(math.log(MAX_DISTANCE / max_exact))
    large = max_exact + (ratio * np.float32(N_BUCKETS - max_exact)).astype(np.int32)
    large = np.minimum(large, N_BUCKETS - 1)
    return np.where(dist < max_exact, dist, large)


def _bucket_indices():
    qi = np.arange(CHUNK)[:, None]
    ki = np.arange(CHUNK)[None, :]
    d_cur = qi - ki
    d_prev = qi + CHUNK - ki
    cur = [np.where(d_cur >= 0, _t5_bucket_np(np.clip(d_cur, 0, None) * dil), -1) for dil in DILATIONS]
    prev = [np.where(d_prev <= CHUNK, _t5_bucket_np(d_prev * dil), -1) for dil in DILATIONS[:2]]
    return np.stack(cur).astype(np.int32), np.stack(prev).astype(np.int32)


def _bias_kernel(tab_ref, icur_ref, iprev_ref, bcur_ref, bprev_ref):
    for idx_ref, out_ref in ((icur_ref, bcur_ref), (iprev_ref, bprev_ref)):
        for g in range(idx_ref.shape[0]):
            idx = idx_ref[g]
            accs = [jnp.full((CHUNK, CHUNK), NEG_BIG, F32) for _ in range(HEADS_PER_GROUP)]
            for b in range(N_BUCKETS):
                hit = idx == b
                for h in range(HEADS_PER_GROUP):
                    accs[h] = jnp.where(hit, tab_ref[b, g * HEADS_PER_GROUP + h], accs[h])
            for h in range(HEADS_PER_GROUP):
                out_ref[g, h] = accs[h]


def _bias_tables(rel_bias):
    icur, iprev = _bucket_indices()
    vm = pl.BlockSpec(memory_space=pltpu.VMEM)
    return pl.pallas_call(
        _bias_kernel,
        in_specs=[pl.BlockSpec(memory_space=pltpu.SMEM), vm, vm],
        out_specs=(vm, vm),
        out_shape=(jax.ShapeDtypeStruct((3, HEADS_PER_GROUP, CHUNK, CHUNK), F32),
                   jax.ShapeDtypeStruct((2, HEADS_PER_GROUP, CHUNK, CHUNK), F32)),
        name="bias",
    )(rel_bias, jnp.asarray(icur), jnp.asarray(iprev))


def kernel(x, norm_mix_g, w_in, a_ln_g, a_ln_b, a_ws, a_bs, w_a_out, b_conv, w_b_out, rel_bias,
           w_c_out, d_w, d_scale, w_d_out, w_o, norm_ff_g, w_ff1, w_ff2, final_g):
    B, S, D = x.shape
    depth = w_in.shape[0]
    bcur, bprev = _bias_tables(rel_bias)
    w_outs = (w_a_out, w_b_out, w_c_out, w_d_out)
    w_mix_b = jnp.concatenate([w_in[0][:, a:b] for a, b in _MIX_COLS], axis=1).astype(BF16)
    bs_full = jnp.repeat(jnp.swapaxes(a_bs, 1, 2), W_BRANCH // A_GROUPS, axis=2)
    eye = jnp.eye(POOL_GROUPS, dtype=d_w.dtype)
    wpool_bd = jnp.einsum("lgde,gh->lgdhe", d_w, eye).reshape(depth, W_BRANCH, W_BRANCH).astype(BF16)
    final_row = final_g[None]
    T = B * S
    for l in range(depth):
        more = l + 1 < depth

        mix_jobs = (("plain", 0, (w_o,)), ("w_out", 0, w_outs)) if l == 0 else ()
        pa, pb, pd, q0, q1, q2, *cast = _mix_call(
            l, x, norm_mix_g, w_mix_b, a_ln_g, a_ln_b, a_ws, bs_full, b_conv, wpool_bd, d_scale,
            cast_jobs=mix_jobs)
        if l == 0:
            w_o_b, w_out_b = cast

        attn_jobs = (("plain", 0, (w_ff1,)), ("plain", 0, (w_ff2,)), ("gate", 0, (w_in,))) if l == 0 else ()
        pc, *cast = _attn_call((q0, q1, q2), bcur, bprev, cast_jobs=attn_jobs)
        if l == 0:
            w_ff1_b, w_ff2_b, w_gate_b = cast

        merge_jobs = (("w_in", l + 1, (w_in,)),) if more else ()
        x2, *cast = _merge_call(l, x.reshape(T, D), pa.reshape(T, -1), pb.reshape(T, -1),
                                pc.reshape(T, -1), pd.reshape(T, -1), norm_mix_g,
                                w_gate_b, w_out_b, w_o_b, cast_jobs=merge_jobs)
        if more:
            w_mix_b, w_gate_b = cast

        ffn_jobs = (("plain", l + 1, (w_ff1,)), ("plain", l + 1, (w_ff2,)), ("plain", l + 1, (w_o,)),
                    ("w_out", l + 1, w_outs)) if more else ()
        x2, *cast = _ffn_call(l, x2, norm_ff_g, w_ff1_b, w_ff2_b, final_row,
                              final_norm=not more, cast_jobs=ffn_jobs)
        if more:
            w_ff1_b, w_ff2_b, w_o_b, w_out_b = cast
        x = x2.reshape(B, S, D)
    return x
```
